```python
import math
import jax
import jax.numpy as jnp
from jax import lax
import numpy as np

D_MODEL = 2048
BATCH = 2
SEQ = 4096
DEPTH = 4
DEC_BATCH = 128
DEC_SEQ = 1
PAST_LEN = 8192
PAGE_SIZE = 128

N_A_LAYERS = DEPTH // 2
N_B_LAYERS = DEPTH - N_A_LAYERS
N_SUB = 3
EPS = 1e-6
NEG_INF = -1e30

MLA_HEADS = D_MODEL // 128
MLA_NOPE = 128
MLA_ROPE = 64
MLA_V = 128
MLA_Q_LORA = D_MODEL // 4
MLA_KV_LORA = D_MODEL // 4
MLA_LAT = MLA_KV_LORA + MLA_ROPE
MLA_SCALE = (MLA_NOPE + MLA_ROPE) ** -0.5
ROPE_THETA = 10000.0

SWA_HEAD_DIM = 64
SWA_HEADS = D_MODEL // SWA_HEAD_DIM
SWA_KV_HEADS = SWA_HEADS // 8
SWA_GROUP = SWA_HEADS // SWA_KV_HEADS
SWA_SCALE = SWA_HEAD_DIM ** -0.5
WINDOW = 128
BLOCK = 128

NUM_BUCKETS = 32
MAX_DISTANCE = 128

D_FF = (D_MODEL * 11) // 4

kernel_name = 'yoco_mla_swa_sink_macaron_adaln_step'


def rmsnorm(x, g):
    xf = x.astype(jnp.float32)
    y = xf * lax.rsqrt(jnp.mean(xf * xf, axis=-1, keepdims=True) + EPS)
    return (y * g.astype(jnp.float32)).astype(x.dtype)


def modulate(x, g, shift, scale):
    return rmsnorm(x, g) * (1.0 + scale[:, None, :]) + shift[:, None, :]


def ada_params(c, w, b, n):
    mod = jax.nn.silu(c) @ w + b
    return jnp.moveaxis(mod.reshape(c.shape[0], n, D_MODEL), 1, 0)


def swiglu(h, w_in, w_out):
    gate, up = jnp.split(h @ w_in, 2, axis=-1)
    return (jax.nn.silu(gate) * up) @ w_out


def ffn_half(x, shift, scale, gate, g, w_in, w_out):
    return x + 0.5 * gate[:, None, :] * swiglu(modulate(x, g, shift, scale), w_in, w_out)


def rope(x, pos):
    half = x.shape[-1] // 2
    inv = ROPE_THETA ** (-jnp.arange(half, dtype=jnp.float32) / half)
    ang = pos.astype(jnp.float32)[:, None] * inv
    cos = jnp.cos(ang)[None, :, None, :]
    sin = jnp.sin(ang)[None, :, None, :]
    xf = x.astype(jnp.float32)
    x1, x2 = xf[..., :half], xf[..., half:]
    return jnp.concatenate([x1 * cos - x2 * sin, x2 * cos + x1 * sin], axis=-1).astype(x.dtype)


def t5_bucket(dist):
    max_exact = NUM_BUCKETS // 2
    n = jnp.maximum(dist, 0)
    nf = jnp.maximum(n, 1).astype(jnp.float32)
    large = max_exact + (jnp.log(nf / max_exact) / math.log(MAX_DISTANCE / max_exact)
                         * (NUM_BUCKETS - max_exact)).astype(jnp.int32)
    large = jnp.minimum(large, NUM_BUCKETS - 1)
    return jnp.where(n < max_exact, n, large)


def rel_bias(table, q_pos, k_pos):
    b = table[t5_bucket(q_pos[:, None] - k_pos[None, :])]
    b = jnp.transpose(b, (2, 0, 1)).astype(jnp.float32)
    return b.reshape(SWA_KV_HEADS, SWA_GROUP, q_pos.shape[0], k_pos.shape[0])


def mla_project(h, pos, w_dq, q_norm, w_uq, w_dkv, kv_norm, w_uk):
    B, T, _ = h.shape
    q = (rmsnorm(h @ w_dq, q_norm) @ w_uq).reshape(B, T, MLA_HEADS, MLA_NOPE + MLA_ROPE)
    q_lat = jnp.einsum('bthd,lhd->bthl', q[..., :MLA_NOPE], w_uk)
    q_full = jnp.concatenate([q_lat, rope(q[..., MLA_NOPE:], pos)], axis=-1)
    ckv = h @ w_dkv
    row = jnp.concatenate([rmsnorm(ckv[..., :MLA_KV_LORA], kv_norm),
                           rope(ckv[..., None, MLA_KV_LORA:], pos)[:, :, 0]], axis=-1)
    return q_full, row


def mla_core(q, parts, q_pos, k_pos):
    s = jnp.concatenate([jnp.einsum('bqhc,bkc->bhqk', q, r) for r in parts], axis=-1)
    s = s.astype(jnp.float32) * MLA_SCALE
    s = jnp.where(k_pos[None, :] <= q_pos[:, None], s, NEG_INF)
    p = jax.nn.softmax(s, axis=-1).astype(q.dtype)
    o = None
    start = 0
    for r in parts:
        n = r.shape[1]
        term = jnp.einsum('bhqk,bkl->bqhl', p[..., start:start + n], r[..., :MLA_KV_LORA])
        o = term if o is None else o + term
        start += n
    return o


def mla_prompt_attend(q, rows, pos):
    B, T = q.shape[:2]
    nb = T // BLOCK

    def one_block(i):
        start = i * BLOCK
        qb = lax.dynamic_slice_in_dim(q, start, BLOCK, axis=1)
        qp = lax.dynamic_slice_in_dim(pos, start, BLOCK)
        return mla_core(qb, [rows], qp, pos)

    o = lax.map(one_block, jnp.arange(nb))
    return jnp.moveaxis(o, 0, 1).reshape(B, T, MLA_HEADS, MLA_KV_LORA)


def mla_out(o_lat, w_uv, w_o):
    B, T = o_lat.shape[:2]
    o = jnp.einsum('bthl,lhd->bthd', o_lat, w_uv)
    return o.reshape(B, T, MLA_HEADS * MLA_V) @ w_o


def shared_kv(x, c, g, ada_w, ada_b, w_kv):
    shift, scale = ada_params(c, ada_w, ada_b, 2)
    B, T, _ = x.shape
    kv = (modulate(x, g, shift, scale) @ w_kv).reshape(B, T, 2, SWA_KV_HEADS, SWA_HEAD_DIM)
    return kv[:, :, 0], kv[:, :, 1]


def swa_core(q, k, v, bias, mask, sinks):
    s = jnp.einsum('bnqhgd,bnshd->bnhgqs', q, k).astype(jnp.float32) * SWA_SCALE + bias
    s = jnp.where(mask, s, NEG_INF)
    sink = jnp.broadcast_to(sinks.astype(jnp.float32).reshape(SWA_KV_HEADS, SWA_GROUP, 1, 1),
                            s.shape[:-1] + (1,))
    p = jax.nn.softmax(jnp.concatenate([s, sink], axis=-1), axis=-1)[..., :-1]
    return jnp.einsum('bnhgqs,bnshd->bnqhgd', p.astype(v.dtype), v)


def swa_prompt(h, k, v, w_q, w_o, sinks, bias, mask):
    B, T, _ = h.shape
    nb = T // BLOCK
    q = (h @ w_q).reshape(B, nb, BLOCK, SWA_KV_HEADS, SWA_GROUP, SWA_HEAD_DIM)
    pad = ((0, 0), (1, 0), (0, 0), (0, 0), (0, 0))
    kb = k.reshape(B, nb, BLOCK, SWA_KV_HEADS, SWA_HEAD_DIM)
    vb = v.reshape(B, nb, BLOCK, SWA_KV_HEADS, SWA_HEAD_DIM)
    kk = jnp.concatenate([jnp.pad(kb, pad)[:, :-1], kb], axis=2)
    vv = jnp.concatenate([jnp.pad(vb, pad)[:, :-1], vb], axis=2)
    o = swa_core(q, kk, vv, bias, mask, sinks)
    return o.reshape(B, T, SWA_HEADS * SWA_HEAD_DIM) @ w_o


def swa_sample(h, keys, vals, w_q, w_o, sinks, bias, mask):
    B, S, _ = h.shape
    q = (h @ w_q).reshape(B, 1, S, SWA_KV_HEADS, SWA_GROUP, SWA_HEAD_DIM)
    o = swa_core(q, keys[:, None], vals[:, None], bias, mask, sinks)
    return o.reshape(B, S, SWA_HEADS * SWA_HEAD_DIM) @ w_o


def setup_inputs(seed: int = 0) -> dict:
    key = jax.random.key(seed)
    ks = jax.random.split(key, 40)
    f32 = jnp.float32
    n_pages = PAST_LEN // PAGE_SIZE
    n_pool = (DEC_BATCH * n_pages * 5) // 4
    win_rows = min(WINDOW, PAST_LEN)

    def nrm(k, shape, scale):
        return jax.random.normal(k, shape, f32) * scale

    def gain(k, shape):
        return 1.0 + 0.05 * jax.random.normal(k, shape, f32)

    page_table = jax.random.permutation(ks[0], n_pool)[:DEC_BATCH * n_pages]
    page_table = page_table.reshape(DEC_BATCH, n_pages).astype(jnp.int32)
    return {
        'x_prompt': nrm(ks[1], (BATCH, SEQ, D_MODEL), 1.0),
        'x_sample': nrm(ks[2], (DEC_BATCH, DEC_SEQ, D_MODEL), 1.0),
        'c_prompt': nrm(ks[3], (BATCH, D_MODEL), 1.0),
        'c_sample': nrm(ks[4], (DEC_BATCH, D_MODEL), 1.0),
        'cache_mla': nrm(ks[5], (n_pool, N_A_LAYERS, PAGE_SIZE, MLA_LAT), 1.0),
        'state_swa_k': nrm(ks[6], (DEC_BATCH, win_rows, SWA_KV_HEADS, SWA_HEAD_DIM), 1.0),
        'state_swa_v': nrm(ks[7], (DEC_BATCH, win_rows, SWA_KV_HEADS, SWA_HEAD_DIM), 1.0),
        'page_table': page_table,
        'ada_w': nrm(ks[8], (DEPTH, D_MODEL, 3 * N_SUB * D_MODEL), 0.5 * D_MODEL ** -0.5),
        'ada_b': nrm(ks[9], (DEPTH, 3 * N_SUB * D_MODEL), 0.01),
        'norm_g': gain(ks[10], (DEPTH, N_SUB, D_MODEL)),
        'ffn_w_in': nrm(ks[11], (DEPTH, 2, D_MODEL, 2 * D_FF), D_MODEL ** -0.5),
        'ffn_w_out': nrm(ks[12], (DEPTH, 2, D_FF, D_MODEL), D_FF ** -0.5),
        'mla_w_dq': nrm(ks[13], (N_A_LAYERS, D_MODEL, MLA_Q_LORA), D_MODEL ** -0.5),
        'mla_q_norm': gain(ks[14], (N_A_LAYERS, MLA_Q_LORA)),
        'mla_w_uq': nrm(ks[15], (N_A_LAYERS, MLA_Q_LORA, MLA_HEADS * (MLA_NOPE + MLA_ROPE)), MLA_Q_LORA ** -0.5),
        'mla_w_dkv': nrm(ks[16], (N_A_LAYERS, D_MODEL, MLA_LAT), D_MODEL ** -0.5),
        'mla_kv_norm': gain(ks[17], (N_A_LAYERS, MLA_KV_LORA)),
        'mla_w_uk': nrm(ks[18], (N_A_LAYERS, MLA_KV_LORA, MLA_HEADS, MLA_NOPE), MLA_KV_LORA ** -0.5),
        'mla_w_uv': nrm(ks[19], (N_A_LAYERS, MLA_KV_LORA, MLA_HEADS, MLA_V), MLA_KV_LORA ** -0.5),
        'mla_w_o': nrm(ks[20], (N_A_LAYERS, MLA_HEADS * MLA_V, D_MODEL), (MLA_HEADS * MLA_V) ** -0.5),
        'kv_ada_w': nrm(ks[21], (D_MODEL, 2 * D_MODEL), 0.5 * D_MODEL ** -0.5),
        'kv_ada_b': nrm(ks[22], (2 * D_MODEL,), 0.01),
        'kv_norm_g': gain(ks[23], (D_MODEL,)),
        'kv_w': nrm(ks[24], (D_MODEL, 2 * SWA_KV_HEADS * SWA_HEAD_DIM), D_MODEL ** -0.5),
        'swa_w_q': nrm(ks[25], (N_B_LAYERS, D_MODEL, SWA_HEADS * SWA_HEAD_DIM), D_MODEL ** -0.5),
        'swa_w_o': nrm(ks[26], (N_B_LAYERS, SWA_HEADS * SWA_HEAD_DIM, D_MODEL), (SWA_HEADS * SWA_HEAD_DIM) ** -0.5),
        'swa_sinks': nrm(ks[27], (N_B_LAYERS, SWA_HEADS), 0.5),
        'rel_bias_table': nrm(ks[28], (NUM_BUCKETS, SWA_HEADS), 0.5),
        'final_norm_g': gain(ks[29], (D_MODEL,)),
    }


def reference(x_prompt, x_sample, c_prompt, c_sample, cache_mla, state_swa_k, state_swa_v, page_table,
              ada_w, ada_b, norm_g, ffn_w_in, ffn_w_out,
              mla_w_dq, mla_q_norm, mla_w_uq, mla_w_dkv, mla_kv_norm, mla_w_uk, mla_w_uv, mla_w_o,
              kv_ada_w, kv_ada_b, kv_norm_g, kv_w,
              swa_w_q, swa_w_o, swa_sinks, rel_bias_table, final_norm_g):
    n_pages = PAST_LEN // PAGE_SIZE
    win_s = min(WINDOW, PAST_LEN)
    win_p = min(WINDOW, SEQ)
    pos_p = jnp.arange(SEQ, dtype=jnp.int32)
    pos_s = PAST_LEN + jnp.arange(DEC_SEQ, dtype=jnp.int32)
    k_pos_mla_s = jnp.arange(PAST_LEN + DEC_SEQ, dtype=jnp.int32)

    nb = SEQ // BLOCK
    qi = jnp.arange(BLOCK, dtype=jnp.int32)
    km = jnp.arange(2 * BLOCK, dtype=jnp.int32) - BLOCK
    d_p = qi[:, None] - km[None, :]
    win_mask = (d_p >= 0) & (d_p <= WINDOW)
    k_abs = jnp.arange(nb, dtype=jnp.int32)[:, None] * BLOCK + km[None, :]
    mask_p = (win_mask[None] & (k_abs >= 0)[:, None, :])[None, :, None, None]
    bias_p = rel_bias(rel_bias_table, qi, km)
    k_pos_swa_s = jnp.concatenate([PAST_LEN - win_s + jnp.arange(win_s, dtype=jnp.int32), pos_s])
    d_s = pos_s[:, None] - k_pos_swa_s[None, :]
    mask_s = (d_s >= 0) & (d_s <= WINDOW)
    bias_s = rel_bias(rel_bias_table, pos_s, k_pos_swa_s)

    x_p, x_s = x_prompt, x_sample
    rows_p, rows_s = [], []
    k_p = v_p = keys_s = vals_s = None
    for l in range(DEPTH):
        if l == N_A_LAYERS:
            k_p, v_p = shared_kv(x_p, c_prompt, kv_norm_g, kv_ada_w, kv_ada_b, kv_w)
            k_new, v_new = shared_kv(x_s, c_sample, kv_norm_g, kv_ada_w, kv_ada_b, kv_w)
            keys_s = jnp.concatenate([state_swa_k, k_new], axis=1)
            vals_s = jnp.concatenate([state_swa_v, v_new], axis=1)
        mod_p = ada_params(c_prompt, ada_w[l], ada_b[l], 3 * N_SUB)
        mod_s = ada_params(c_sample, ada_w[l], ada_b[l], 3 * N_SUB)
        x_p = ffn_half(x_p, mod_p[0], mod_p[1], mod_p[2], norm_g[l, 0], ffn_w_in[l, 0], ffn_w_out[l, 0])
        x_s = ffn_half(x_s, mod_s[0], mod_s[1], mod_s[2], norm_g[l, 0], ffn_w_in[l, 0], ffn_w_out[l, 0])
        h_p = modulate(x_p, norm_g[l, 1], mod_p[3], mod_p[4])
        h_s = modulate(x_s, norm_g[l, 1], mod_s[3], mod_s[4])
        if l < N_A_LAYERS:
            a = l
            q_full_p, row_p = mla_project(h_p, pos_p, mla_w_dq[a], mla_q_norm[a], mla_w_uq[a],
                                          mla_w_dkv[a], mla_kv_norm[a], mla_w_uk[a])
            mix_p = mla_out(mla_prompt_attend(q_full_p, row_p, pos_p), mla_w_uv[a], mla_w_o[a])
            q_full_s, row_s = mla_project(h_s, pos_s, mla_w_dq[a], mla_q_norm[a], mla_w_uq[a],
                                          mla_w_dkv[a], mla_kv_norm[a], mla_w_uk[a])
            past = cache_mla[page_table, a].reshape(DEC_BATCH, n_pages * PAGE_SIZE, MLA_LAT)
            mix_s = mla_out(mla_core(q_full_s, [past, row_s], pos_s, k_pos_mla_s), mla_w_uv[a], mla_w_o[a])
            rows_p.append(row_p)
            rows_s.append(row_s)
        else:
            b = l - N_A_LAYERS
            mix_p = swa_prompt(h_p, k_p, v_p, swa_w_q[b], swa_w_o[b], swa_sinks[b], bias_p, mask_p)
            mix_s = swa_sample(h_s, keys_s, vals_s, swa_w_q[b], swa_w_o[b], swa_sinks[b], bias_s, mask_s)
        x_p = x_p + mod_p[5][:, None, :] * mix_p
        x_s = x_s + mod_s[5][:, None, :] * mix_s
        x_p = ffn_half(x_p, mod_p[6], mod_p[7], mod_p[8], norm_g[l, 2], ffn_w_in[l, 1], ffn_w_out[l, 1])
        x_s = ffn_half(x_s, mod_s[6], mod_s[7], mod_s[8], norm_g[l, 2], ffn_w_in[l, 1], ffn_w_out[l, 1])

    y_prompt = rmsnorm(x_p, final_norm_g)
    y_sample = rmsnorm(x_s, final_norm_g)
    new_mla_prompt = jnp.stack(rows_p, axis=1)
    new_mla_sample = jnp.stack(rows_s, axis=1)
    new_swa_k_prompt = k_p[:, SEQ - win_p:]
    new_swa_v_prompt = v_p[:, SEQ - win_p:]
    new_swa_k_sample = keys_s[:, keys_s.shape[1] - win_s:]
    new_swa_v_sample = vals_s[:, vals_s.shape[1] - win_s:]
    return (y_prompt, y_sample, new_mla_prompt, new_mla_sample,
            new_swa_k_prompt, new_swa_v_prompt, new_swa_k_sample, new_swa_v_sample)
```

```python
import functools
import math

import jax
import jax.numpy as jnp
import numpy as np
from jax import lax
from jax.experimental import pallas as pl
from jax.experimental.pallas import tpu as pltpu

EPS = 1e-6
NEG_INF = -1e30
ROPE_THETA = 10000.0
PAGE_SIZE = 128
WINDOW = 128
BLOCK = 128
NUM_BUCKETS = 32
MAX_DISTANCE = 128
MLA_NOPE = 128
MLA_ROPE = 64
MLA_V = 128
SWA_HEAD_DIM = 64
SWA_GROUP = 8
N_SUB = 3

LANES = 128
V7X_VMEM_LIMIT_BYTES = 56 * 1024 * 1024

BF = jnp.bfloat16
F32 = jnp.float32


def _params(*sem):
    return pltpu.CompilerParams(dimension_semantics=sem, vmem_limit_bytes=V7X_VMEM_LIMIT_BYTES)


def _pick(n, prefs):
    for p in prefs:
        if p <= n and n % p == 0:
            return p
    return n


def _row_tile(m, seq_rows, prefs):
    return _pick(m if seq_rows == 1 else seq_rows, prefs)


def _const_spec(arr):
    nd = arr.ndim
    return pl.BlockSpec(arr.shape, lambda *_: (0,) * nd, pipeline_mode=pl.Buffered(1))


def _silu(x):
    return x * (1.0 / (1.0 + jnp.exp(-x)))


def _rms(x, g):
    return x * lax.rsqrt(jnp.mean(x * x, axis=-1, keepdims=True) + EPS) * g


def _modulate(x, g, shift, scale):
    return _rms(x, g) * (1.0 + scale) + shift


def _dot(a, b):
    return jnp.dot(a, b, preferred_element_type=F32)


def _dot_nt(a, b):
    return lax.dot_general(a, b, (((1,), (1,)), ((), ())), preferred_element_type=F32)


def _mod_spec(arr, tm, seq_rows):
    _, r, d = arr.shape
    if r == 1:
        return pl.BlockSpec((None, 1, d), lambda i, *_: ((i * tm) // seq_rows, 0, 0))
    return pl.BlockSpec((None, tm, d), lambda i, *_: (0, i, 0))


def _pos_spec(arr, tm):
    p = arr.shape[0]
    if p == 1:
        return pl.BlockSpec((1, LANES), lambda i, *_: (0, 0))
    nblk = p // tm
    return pl.BlockSpec((tm, LANES), lambda i, *_: (i % nblk, 0))


def _ada_kernel(c_ref, w_ref, b_ref, o_ref):
    a = _silu(c_ref[...]).astype(BF)
    o_ref[...] = _dot(a, w_ref[...].astype(BF)) + b_ref[...]


def _ada(c, w, b):
    nl, d, n = w.shape
    m = c.shape[0]
    tn = _pick(n, (1024, 512, 256, 128))
    return pl.pallas_call(
        _ada_kernel,
        grid=(nl, n // tn),
        in_specs=[
            pl.BlockSpec((m, d), lambda l, j: (0, 0)),
            pl.BlockSpec((None, d, tn), lambda l, j: (l, 0, j)),
            pl.BlockSpec((None, 1, tn), lambda l, j: (l, 0, j)),
        ],
        out_specs=pl.BlockSpec((None, m, tn), lambda l, j: (l, 0, j)),
        out_shape=jax.ShapeDtypeStruct((nl, m, n), F32),
        compiler_params=_params("parallel", "parallel"),
        name="ada_params",
    )(c, w, b)


def _ffn_kernel(x_ref, g_ref, sh_ref, sc_ref, gt_ref, wg_ref, wu_ref, wo_ref, o_ref, h_ref):
    j = pl.program_id(1)

    @pl.when(j == 0)
    def _():
        h_ref[...] = _modulate(x_ref[...], g_ref[...], sh_ref[...], sc_ref[...]).astype(BF)

    h = h_ref[...]
    gate = _dot(h, wg_ref[...])
    up = _dot(h, wu_ref[...])
    y = _dot((_silu(gate) * up).astype(BF), wo_ref[...])

    @pl.when(j == 0)
    def _():
        o_ref[...] = y

    @pl.when(j > 0)
    def _():
        o_ref[...] += y

    @pl.when(j == pl.num_programs(1) - 1)
    def _():
        o_ref[...] = x_ref[...] + 0.5 * gt_ref[...] * o_ref[...]


def _ffn(x, g, shift, scale, gate, w_in, w_out, *, seq_rows):
    m, d = x.shape
    f = w_out.shape[0]
    tm = _row_tile(m, seq_rows, (512, 256, 128))
    tf = _pick(f, (512, 256, 128))
    nf = f // tf
    ms = _mod_spec(shift, tm, seq_rows)
    return pl.pallas_call(
        _ffn_kernel,
        grid=(m // tm, nf),
        in_specs=[
            pl.BlockSpec((tm, d), lambda i, j: (i, 0)),
            pl.BlockSpec((1, d), lambda i, j: (0, 0)),
            ms, ms, ms,
            pl.BlockSpec((d, tf), lambda i, j: (0, j)),
            pl.BlockSpec((d, tf), lambda i, j: (0, j + nf)),
            pl.BlockSpec((tf, d), lambda i, j: (j, 0)),
        ],
        out_specs=pl.BlockSpec((tm, d), lambda i, j: (i, 0)),
        out_shape=jax.ShapeDtypeStruct((m, d), F32),
        scratch_shapes=[pltpu.VMEM((tm, d), BF)],
        compiler_params=_params("parallel", "arbitrary"),
        name="ffn_half",
    )(x, g, shift, scale, gate, w_in, w_in, w_out)


def _mod_mm_kernel(x_ref, g_ref, sh_ref, sc_ref, w_ref, *o_refs):
    h = _modulate(x_ref[...], g_ref[...], sh_ref[...], sc_ref[...]).astype(BF)
    y = _dot(h, w_ref[...])
    for o_ref in o_refs:
        o_ref[...] = y.astype(o_ref.dtype)


def _mod_mm(x, g, shift, scale, w, out_dtypes, *, seq_rows):
    m, d = x.shape
    n = w.shape[1]
    tm = _row_tile(m, seq_rows, (512, 256, 128))
    ms = _mod_spec(shift, tm, seq_rows)
    outs = pl.pallas_call(
        _mod_mm_kernel,
        grid=(m // tm,),
        in_specs=[
            pl.BlockSpec((tm, d), lambda i: (i, 0)),
            pl.BlockSpec((1, d), lambda i: (0, 0)),
            ms, ms,
            _const_spec(w),
        ],
        out_specs=[pl.BlockSpec((tm, n), lambda i: (i, 0)) for _ in out_dtypes],
        out_shape=[jax.ShapeDtypeStruct((m, n), dt) for dt in out_dtypes],
        compiler_params=_params("parallel"),
        name="mod_matmul",
    )(x, g, shift, scale, w)
    return outs


def _mm_resid_kernel(a_ref, w_ref, x_ref, gt_ref, o_ref):
    o_ref[...] = x_ref[...] + gt_ref[...] * _dot(a_ref[...], w_ref[...])


def _mm_resid(a, w, x, gate, *, seq_rows):
    m, k = a.shape
    n = w.shape[1]
    tm = _row_tile(m, seq_rows, (512, 256, 128))
    return pl.pallas_call(
        _mm_resid_kernel,
        grid=(m // tm,),
        in_specs=[
            pl.BlockSpec((tm, k), lambda i: (i, 0)),
            _const_spec(w),
            pl.BlockSpec((tm, n), lambda i: (i, 0)),
            _mod_spec(gate, tm, seq_rows),
        ],
        out_specs=pl.BlockSpec((tm, n), lambda i: (i, 0)),
        out_shape=jax.ShapeDtypeStruct((m, n), F32),
        compiler_params=_params("parallel"),
        name="matmul_residual",
    )(a, w, x, gate)


def _mla_pre_kernel(x_ref, g_ref, sh_ref, sc_ref, cos_ref, sin_ref, w1_ref, qn_ref, kn_ref, w2_ref, w3_ref,
                    row_ref, kcat_ref, qcat_ref, *, ql, kl, nh):
    cos_t = cos_ref[...]
    sin_t = sin_ref[...]
    h = _modulate(x_ref[...], g_ref[...], sh_ref[...], sc_ref[...]).astype(BF)
    p1 = _dot(h, w1_ref[...])
    cq = _rms(p1[:, :ql], qn_ref[...]).astype(BF)
    lat = _rms(p1[:, ql:ql + kl], kn_ref[...])
    a = ql + kl
    kr = p1[:, a:a + LANES] * cos_t + p1[:, a + LANES:a + 2 * LANES] * sin_t
    row_ref[:, :kl] = lat
    row_ref[:, kl:] = kr[:, :MLA_ROPE]
    kcat_ref[:, :kl] = lat.astype(BF)
    kcat_ref[:, kl:] = kr.astype(BF)
    q2 = _dot(cq, w2_ref[...])
    hn = nh * LANES
    for hd in range(nh):
        lo = hd * LANES
        qn = q2[:, lo:lo + LANES].astype(BF)
        qcat_ref[hd, :, :kl] = _dot(qn, w3_ref[hd]).astype(BF)
        qr = q2[:, hn + lo:hn + lo + LANES] * cos_t + q2[:, 2 * hn + lo:2 * hn + lo + LANES] * sin_t
        qcat_ref[hd, :, kl:] = qr.astype(BF)


def _mla_pre(x, g, shift, scale, cos_t, sin_t, w1, qnorm, knorm, w2, w3, *, seq_rows):
    m, d = x.shape
    ql = qnorm.shape[1]
    kl = knorm.shape[1]
    nh = w3.shape[0]
    c = kl + LANES
    tm = _row_tile(m, seq_rows, (256, 128))
    ms = _mod_spec(shift, tm, seq_rows)
    ps = _pos_spec(cos_t, tm)
    return pl.pallas_call(
        functools.partial(_mla_pre_kernel, ql=ql, kl=kl, nh=nh),
        grid=(m // tm,),
        in_specs=[
            pl.BlockSpec((tm, d), lambda i: (i, 0)),
            pl.BlockSpec((1, d), lambda i: (0, 0)),
            ms, ms, ps, ps,
            _const_spec(w1), _const_spec(qnorm), _const_spec(knorm), _const_spec(w2), _const_spec(w3),
        ],
        out_specs=[
            pl.BlockSpec((tm, kl + MLA_ROPE), lambda i: (i, 0)),
            pl.BlockSpec((tm, c), lambda i: (i, 0)),
            pl.BlockSpec((nh, tm, c), lambda i: (0, i, 0)),
        ],
        out_shape=[
            jax.ShapeDtypeStruct((m, kl + MLA_ROPE), F32),
            jax.ShapeDtypeStruct((m, c), BF),
            jax.ShapeDtypeStruct((nh, m, c), BF),
        ],
        compiler_params=_params("parallel"),
        name="mla_project",
    )(x, g, shift, scale, cos_t, sin_t, w1, qnorm, knorm, w2, w3)


def _mla_flash_kernel(q_ref, k_ref, wuv_ref, o_ref, m_sc, l_sc, acc_sc, *, tq, tk, kl, nh, scale):
    i = pl.program_id(1)
    j = pl.program_id(2)
    rows = nh * tq
    j_last = ((i + 1) * tq - 1) // tk

    @pl.when(j == 0)
    def _():
        m_sc[...] = jnp.full(m_sc.shape, NEG_INF, F32)
        l_sc[...] = jnp.zeros(l_sc.shape, F32)
        acc_sc[...] = jnp.zeros(acc_sc.shape, F32)

    def step(masked):
        q = q_ref[...].reshape(rows, q_ref.shape[-1])
        k = k_ref[...]
        s = _dot_nt(q, k) * scale
        if masked:
            qpos = i * tq + (lax.broadcasted_iota(jnp.int32, (rows, tk), 0) & (tq - 1))
            kpos = j * tk + lax.broadcasted_iota(jnp.int32, (rows, tk), 1)
            s = jnp.where(kpos <= qpos, s, NEG_INF)
        m_old = m_sc[...]
        m_new = jnp.maximum(m_old, jnp.max(s, axis=-1, keepdims=True))
        alpha = jnp.exp(m_old - m_new)
        p = jnp.exp(s - m_new)
        l_sc[...] = alpha * l_sc[...] + jnp.sum(p, axis=-1, keepdims=True)
        acc_sc[...] = alpha * acc_sc[...] + _dot(p.astype(BF), k[:, :kl])
        m_sc[...] = m_new

    needs_mask = (j + 1) * tk - 1 > i * tq

    @pl.when((j <= j_last) & needs_mask)
    def _():
        step(True)

    @pl.when((j <= j_last) & jnp.logical_not(needs_mask))
    def _():
        step(False)

    @pl.when(j == j_last)
    def _():
        o = (acc_sc[...] * (1.0 / l_sc[...])).astype(BF)
        for hd in range(nh):
            o_ref[:, hd * MLA_V:(hd + 1) * MLA_V] = _dot(o[hd * tq:(hd + 1) * tq], wuv_ref[hd]).astype(BF)


def _mla_flash(qcat, kcat, wuv, *, batch, seq, scale):
    nh, m, c = qcat.shape
    kl = wuv.shape[1]
    tq = _pick(seq, (128,))
    tk = _pick(seq, (512, 256, 128))
    nq = seq // tq
    nk = seq // tk

    def k_map(b, i, j):
        return (b * nk + jnp.minimum(j, ((i + 1) * tq - 1) // tk), 0)

    return pl.pallas_call(
        functools.partial(_mla_flash_kernel, tq=tq, tk=tk, kl=kl, nh=nh, scale=scale),
        grid=(batch, nq, nk),
        in_specs=[
            pl.BlockSpec((nh, tq, c), lambda b, i, j: (0, b * nq + i, 0)),
            pl.BlockSpec((tk, c), k_map),
            _const_spec(wuv),
        ],
        out_specs=pl.BlockSpec((tq, nh * MLA_V), lambda b, i, j: (b * nq + i, 0)),
        out_shape=jax.ShapeDtypeStruct((m, nh * MLA_V), BF),
        scratch_shapes=[
            pltpu.VMEM((nh * tq, 1), F32),
            pltpu.VMEM((nh * tq, 1), F32),
            pltpu.VMEM((nh * tq, kl), F32),
        ],
        compiler_params=_params("parallel", "parallel", "arbitrary"),
        name="mla_prompt_attention",
    )(qcat, kcat, wuv)


def _mla_decode_kernel(pt_ref, q_ref, new_ref, *rest, npg, kl, scale):
    page_refs = rest[:npg]
    o_ref = rest[npg]
    kbuf, m_sc, l_sc, acc_sc = rest[npg + 1:]
    c = pl.program_id(1)

    @pl.when(c == 0)
    def _():
        m_sc[...] = jnp.full(m_sc.shape, NEG_INF, F32)
        l_sc[...] = jnp.zeros(l_sc.shape, F32)
        acc_sc[...] = jnp.zeros(acc_sc.shape, F32)

    for k, ref in enumerate(page_refs):
        kbuf[k * PAGE_SIZE:(k + 1) * PAGE_SIZE, :] = ref[...].astype(BF)

    q = q_ref[...]
    keys = kbuf[...]
    s = (_dot_nt(q[:, :kl], keys[:, :kl]) + _dot_nt(q[:, kl:kl + MLA_ROPE], keys[:, kl:])) * scale
    m_old = m_sc[...]
    m_new = jnp.maximum(m_old, jnp.max(s, axis=-1, keepdims=True))
    alpha = jnp.exp(m_old - m_new)
    p = jnp.exp(s - m_new)
    l_sc[...] = alpha * l_sc[...] + jnp.sum(p, axis=-1, keepdims=True)
    acc_sc[...] = alpha * acc_sc[...] + _dot(p.astype(BF), keys[:, :kl])
    m_sc[...] = m_new

    @pl.when(c == pl.num_programs(1) - 1)
    def _():
        new = new_ref[...].astype(F32)
        s_new = jnp.sum(q.astype(F32) * new, axis=-1, keepdims=True) * scale
        m_old = m_sc[...]
        m_new = jnp.maximum(m_old, s_new)
        alpha = jnp.exp(m_old - m_new)
        p_new = jnp.exp(s_new - m_new)
        l_new = alpha * l_sc[...] + p_new
        acc = alpha * acc_sc[...] + p_new.astype(BF).astype(F32) * new[:, :kl]
        o_ref[...] = (acc * (1.0 / l_new)).astype(BF)


def _mla_decode(page_table, qcat, kcat, cache, *, layer, scale):
    nb, nh, c = qcat.shape
    lat = cache.shape[-1]
    kl = lat - MLA_ROPE
    n_pages = page_table.shape[1]
    npg = _pick(n_pages, (16, 8, 4, 2, 1))
    pt = page_table.reshape(-1)

    def page_spec(k):
        return pl.BlockSpec((None, None, PAGE_SIZE, lat),
                            lambda b, ch, pt_ref: (pt_ref[b * n_pages + ch * npg + k], layer, 0, 0))

    grid_spec = pltpu.PrefetchScalarGridSpec(
        num_scalar_prefetch=1,
        grid=(nb, n_pages // npg),
        in_specs=[
            pl.BlockSpec((None, nh, c), lambda b, ch, pt_ref: (b, 0, 0)),
            pl.BlockSpec((None, 1, c), lambda b, ch, pt_ref: (b, 0, 0)),
        ] + [page_spec(k) for k in range(npg)],
        out_specs=pl.BlockSpec((None, nh, kl), lambda b, ch, pt_ref: (b, 0, 0)),
        scratch_shapes=[
            pltpu.VMEM((npg * PAGE_SIZE, lat), BF),
            pltpu.VMEM((nh, 1), F32),
            pltpu.VMEM((nh, 1), F32),
            pltpu.VMEM((nh, kl), F32),
        ],
    )
    return pl.pallas_call(
        functools.partial(_mla_decode_kernel, npg=npg, kl=kl, scale=scale),
        grid_spec=grid_spec,
        out_shape=jax.ShapeDtypeStruct((nb, nh, kl), BF),
        compiler_params=_params("parallel", "arbitrary"),
        name="mla_sample_attention",
    )(pt, qcat, kcat, *([cache] * npg))


def _uv_kernel(o_ref, wuv_ref, out_ref):
    nh = o_ref.shape[0]
    for hd in range(nh):
        out_ref[:, hd * MLA_V:(hd + 1) * MLA_V] = _dot(o_ref[hd], wuv_ref[hd]).astype(BF)


def _uv(o_lat, wuv):
    nh, m, kl = o_lat.shape
    return pl.pallas_call(
        _uv_kernel,
        grid=(1,),
        in_specs=[
            pl.BlockSpec(o_lat.shape, lambda i: (0, 0, 0)),
            pl.BlockSpec(wuv.shape, lambda i: (0, 0, 0)),
        ],
        out_specs=pl.BlockSpec((m, nh * MLA_V), lambda i: (0, 0)),
        out_shape=jax.ShapeDtypeStruct((m, nh * MLA_V), BF),
        compiler_params=_params("arbitrary"),
        name="mla_value_up",
    )(o_lat, wuv)


def _swa_prompt_kernel(sink_ref, q_ref, kvp_ref, kvc_ref, bias_ref, mask_ref, o_ref, *, n_kv, scale):
    kvw = n_kv * LANES
    pairs_per_kv = SWA_GROUP // 2
    mask = mask_ref[...] > 0.5
    nkeys = 2 * BLOCK
    for c in range(n_kv):
        def sect(s):
            lo = s * kvw + c * LANES
            return jnp.concatenate([kvp_ref[:, lo:lo + LANES], kvc_ref[:, lo:lo + LANES]], axis=0)
        k2 = jnp.concatenate([sect(0), sect(1)], axis=0)
        v2 = jnp.concatenate([sect(2), sect(3)], axis=0)
        for pp in range(pairs_per_kv):
            pr = c * pairs_per_kv + pp
            qp = q_ref[:, pr * LANES:(pr + 1) * LANES]
            s = _dot_nt(qp, k2) * scale + bias_ref[pr]
            s = jnp.where(mask, s, NEG_INF)
            halves = []
            for hf in range(2):
                sh = s[:, hf * nkeys:(hf + 1) * nkeys]
                sink = sink_ref[2 * pr + hf]
                m = jnp.maximum(jnp.max(sh, axis=-1, keepdims=True), sink)
                e = jnp.exp(sh - m)
                den = jnp.sum(e, axis=-1, keepdims=True) + jnp.exp(sink - m)
                halves.append((e * (1.0 / den)).astype(BF))
            p = jnp.concatenate(halves, axis=1)
            o_ref[:, pr * LANES:(pr + 1) * LANES] = _dot(p, v2).astype(BF)


def _swa_prompt(q, kv4, bias_pairs, mask2, sinks, *, batch, seq, n_kv, scale):
    m, dq = q.shape
    nb = seq // BLOCK
    w4 = kv4.shape[1]
    return pl.pallas_call(
        functools.partial(_swa_prompt_kernel, n_kv=n_kv, scale=scale),
        grid=(batch, nb),
        in_specs=[
            pl.BlockSpec(memory_space=pltpu.SMEM),
            pl.BlockSpec((BLOCK, dq), lambda b, j: (b * nb + j, 0)),
            pl.BlockSpec((BLOCK, w4), lambda b, j: (b * nb + jnp.maximum(j - 1, 0), 0)),
            pl.BlockSpec((BLOCK, w4), lambda b, j: (b * nb + j, 0)),
            _const_spec(bias_pairs),
            pl.BlockSpec((None, BLOCK, 4 * BLOCK), lambda b, j: (jnp.minimum(j, 1), 0, 0)),
        ],
        out_specs=pl.BlockSpec((BLOCK, dq), lambda b, j: (b * nb + j, 0)),
        out_shape=jax.ShapeDtypeStruct((m, dq), BF),
        compiler_params=_params("parallel", "arbitrary"),
        name="swa_prompt_attention",
    )(sinks, q, kv4, kv4, bias_pairs, mask2)


def _swa_sample_kernel(q_ref, k_ref, v_ref, bias_ref, sink_ref, o_ref, *, n_kv, n_keys, scale):
    for c in range(n_kv):
        q = q_ref[:, c * SWA_GROUP:(c + 1) * SWA_GROUP, :]
        k = k_ref[:, c]
        v = v_ref[:, c]
        s = jnp.einsum("bgd,bkd->bgk", q, k, preferred_element_type=F32) * scale + bias_ref[c][None]
        col = lax.broadcasted_iota(jnp.int32, s.shape, 2)
        s = jnp.where(col < n_keys, s, NEG_INF)
        sink = sink_ref[c][None]
        m = jnp.maximum(jnp.max(s, axis=-1, keepdims=True), sink)
        e = jnp.exp(s - m)
        den = jnp.sum(e, axis=-1, keepdims=True) + jnp.exp(sink - m)
        p = (e * (1.0 / den)).astype(BF)
        o = jnp.einsum("bgk,bkd->bgd", p, v, preferred_element_type=F32)
        o_ref[:, c * SWA_GROUP:(c + 1) * SWA_GROUP, :] = o.astype(BF)


def _swa_sample(q, keys, vals, bias, sinks, *, n_keys, scale):
    nb, nh, dh = q.shape
    n_kv, kp = keys.shape[1], keys.shape[2]
    bb = _pick(nb, (16, 8))
    return pl.pallas_call(
        functools.partial(_swa_sample_kernel, n_kv=n_kv, n_keys=n_keys, scale=scale),
        grid=(nb // bb,),
        in_specs=[
            pl.BlockSpec((bb, nh, dh), lambda i: (i, 0, 0)),
            pl.BlockSpec((bb, n_kv, kp, dh), lambda i: (i, 0, 0, 0)),
            pl.BlockSpec((bb, n_kv, kp, dh), lambda i: (i, 0, 0, 0)),
            pl.BlockSpec(bias.shape, lambda i: (0, 0, 0)),
            pl.BlockSpec(sinks.shape, lambda i: (0, 0, 0)),
        ],
        out_specs=pl.BlockSpec((bb, nh, dh), lambda i: (i, 0, 0)),
        out_shape=jax.ShapeDtypeStruct((nb, nh, dh), BF),
        compiler_params=_params("parallel"),
        name="swa_sample_attention",
    )(q, keys, vals, bias, sinks)


def _final_norm_kernel(x_ref, g_ref, o_ref):
    o_ref[...] = _rms(x_ref[...], g_ref[...])


def _final_norm(x, g):
    m, d = x.shape
    tm = _pick(m, (512, 256, 128))
    return pl.pallas_call(
        _final_norm_kernel,
        grid=(m // tm,),
        in_specs=[pl.BlockSpec((tm, d), lambda i: (i, 0)), pl.BlockSpec((1, d), lambda i: (0, 0))],
        out_specs=pl.BlockSpec((tm, d), lambda i: (i, 0)),
        out_shape=jax.ShapeDtypeStruct((m, d), F32),
        compiler_params=_params("parallel"),
        name="final_norm",
    )(x, g)


def _rope_tables(pos):
    half = MLA_ROPE // 2
    inv = ROPE_THETA ** (-jnp.arange(half, dtype=F32) / half)
    ang = pos.astype(F32)[:, None] * inv
    cos, sin = jnp.cos(ang), jnp.sin(ang)
    pad = jnp.zeros((pos.shape[0], LANES - MLA_ROPE), F32)
    return jnp.concatenate([cos, cos, pad], axis=1), jnp.concatenate([-sin, sin, pad], axis=1)


def _swap_halves(w):
    half = w.shape[-1] // 2
    return jnp.concatenate([w[..., half:], w[..., :half]], axis=-1)


def _pad_lanes(w):
    return jnp.pad(w, [(0, 0)] * (w.ndim - 1) + [(0, LANES - w.shape[-1])])


def _t5_bucket(dist):
    max_exact = NUM_BUCKETS // 2
    n = jnp.maximum(dist, 0)
    nf = jnp.maximum(n, 1).astype(F32)
    large = max_exact + (jnp.log(nf / max_exact) / math.log(MAX_DISTANCE / max_exact)
                         * (NUM_BUCKETS - max_exact)).astype(jnp.int32)
    large = jnp.minimum(large, NUM_BUCKETS - 1)
    return jnp.where(n < max_exact, n, large)


def kernel(x_prompt, x_sample, c_prompt, c_sample, cache_mla, state_swa_k, state_swa_v, page_table, ada_w, ada_b, norm_g, ffn_w_in, ffn_w_out, mla_w_dq, mla_q_norm, mla_w_uq, mla_w_dkv, mla_kv_norm, mla_w_uk, mla_w_uv, mla_w_o, kv_ada_w, kv_ada_b, kv_norm_g, kv_w, swa_w_q, swa_w_o, swa_sinks, rel_bias_table, final_norm_g):
    batch, seq, d = x_prompt.shape
    dec_batch, dec_seq, _ = x_sample.shape
    assert dec_seq == 1
    depth = ada_w.shape[0]
    n_a = mla_w_dq.shape[0]
    n_pages = page_table.shape[1]
    past_len = n_pages * PAGE_SIZE
    ql = mla_w_dq.shape[2]
    kl = mla_kv_norm.shape[1]
    nh = mla_w_uk.shape[2]
    swa_heads = swa_w_q.shape[2] // SWA_HEAD_DIM
    n_kv = swa_heads // SWA_GROUP
    win_s = state_swa_k.shape[1]
    assert win_s == WINDOW and seq % BLOCK == 0
    mla_scale = float((MLA_NOPE + MLA_ROPE) ** -0.5)
    swa_scale = float(SWA_HEAD_DIM ** -0.5)
    mp = batch * seq

    c_all = jnp.concatenate([c_prompt, c_sample], axis=0)
    n_seq = c_all.shape[0]
    c_all = jnp.pad(c_all, ((0, (-n_seq) % 16), (0, 0)))
    mod = _ada(c_all, ada_w, ada_b[:, None, :])
    kvmod = _ada(c_all, kv_ada_w[None], kv_ada_b[None, None, :])[0]

    def mods_p(arr, n):
        return arr[:batch, n * d:(n + 1) * d][:, None, :]

    def mods_s(arr, n):
        return arr[batch:batch + dec_batch, n * d:(n + 1) * d][None]

    cos_p, sin_p = _rope_tables(jnp.arange(seq, dtype=jnp.int32))
    cos_s, sin_s = _rope_tables(jnp.full((1,), past_len, jnp.int32))

    qi = jnp.arange(BLOCK, dtype=jnp.int32)
    km = jnp.arange(2 * BLOCK, dtype=jnp.int32) - BLOCK
    d_p = qi[:, None] - km[None, :]
    win_mask = (d_p >= 0) & (d_p <= WINDOW)
    bias_p = jnp.transpose(rel_bias_table[_t5_bucket(d_p)], (2, 0, 1)).astype(F32)
    bias_pairs = bias_p.reshape(swa_heads // 2, 2, BLOCK, 2 * BLOCK).transpose(0, 2, 1, 3)
    bias_pairs = bias_pairs.reshape(swa_heads // 2, BLOCK, 4 * BLOCK)
    mask_first = win_mask & (km >= 0)[None, :]
    mask2 = jnp.stack([jnp.tile(mask_first, (1, 2)), jnp.tile(win_mask, (1, 2))]).astype(F32)
    kp_pad = 2 * WINDOW
    n_keys_s = win_s + 1
    d_s = jnp.concatenate([jnp.arange(win_s, 0, -1, dtype=jnp.int32), jnp.zeros((1,), jnp.int32)])
    bias_s = jnp.transpose(rel_bias_table[_t5_bucket(d_s)], (1, 0)).astype(F32)
    bias_s = jnp.pad(bias_s, ((0, 0), (0, kp_pad - n_keys_s))).reshape(n_kv, SWA_GROUP, kp_pad)

    x_p = x_prompt.reshape(mp, d)
    x_s = x_sample.reshape(dec_batch, d)
    rows_p, rows_s = [], []
    kv4_p = keys_s = vals_s = keys_s_bf = vals_s_bf = None
    k_p = v_p = None

    for l in range(depth):
        if l == n_a:
            kv_k = kv_w[:, :n_kv * SWA_HEAD_DIM].reshape(d, n_kv, SWA_HEAD_DIM)
            kv_v = kv_w[:, n_kv * SWA_HEAD_DIM:].reshape(d, n_kv, SWA_HEAD_DIM)
            zero = jnp.zeros_like(kv_k)
            lo = lambda w: jnp.concatenate([w, zero], axis=-1).reshape(d, n_kv * LANES)
            hi = lambda w: jnp.concatenate([zero, w], axis=-1).reshape(d, n_kv * LANES)
            nkv = 2 * n_kv * SWA_HEAD_DIM
            w_sh = jnp.concatenate([kv_w, lo(kv_k), hi(kv_k), lo(kv_v), hi(kv_v)], axis=1).astype(BF)
            kvg = kv_norm_g[None]
            kv_f32_p, kv_bf_p = _mod_mm(x_p, kvg, mods_p(kvmod, 0), mods_p(kvmod, 1), w_sh, (F32, BF),
                                        seq_rows=seq)
            k_p = kv_f32_p[:, :nkv // 2].reshape(batch, seq, n_kv, SWA_HEAD_DIM)
            v_p = kv_f32_p[:, nkv // 2:nkv].reshape(batch, seq, n_kv, SWA_HEAD_DIM)
            kv4_p = kv_bf_p[:, nkv:]
            (kv_f32_s,) = _mod_mm(x_s, kvg, mods_s(kvmod, 0), mods_s(kvmod, 1), w_sh[:, :nkv], (F32,),
                                  seq_rows=1)
            k_new = kv_f32_s[:, :nkv // 2].reshape(dec_batch, 1, n_kv, SWA_HEAD_DIM)
            v_new = kv_f32_s[:, nkv // 2:].reshape(dec_batch, 1, n_kv, SWA_HEAD_DIM)
            keys_s = jnp.concatenate([state_swa_k, k_new], axis=1)
            vals_s = jnp.concatenate([state_swa_v, v_new], axis=1)
            padk = lambda t: jnp.pad(jnp.transpose(t, (0, 2, 1, 3)).astype(BF),
                                     ((0, 0), (0, 0), (0, kp_pad - n_keys_s), (0, 0)))
            keys_s_bf, vals_s_bf = padk(keys_s), padk(vals_s)

        g = norm_g[l]
        w_in = [ffn_w_in[l, t].astype(BF) for t in range(2)]
        w_out = [ffn_w_out[l, t].astype(BF) for t in range(2)]
        mp_l = [mods_p(mod[l], n) for n in range(3 * N_SUB)]
        ms_l = [mods_s(mod[l], n) for n in range(3 * N_SUB)]

        x_p = _ffn(x_p, g[0][None], mp_l[0], mp_l[1], mp_l[2], w_in[0], w_out[0], seq_rows=seq)
        x_s = _ffn(x_s, g[0][None], ms_l[0], ms_l[1], ms_l[2], w_in[0], w_out[0], seq_rows=1)

        if l < n_a:
            a = l
            dkv = mla_w_dkv[a]
            w1 = jnp.concatenate([mla_w_dq[a], dkv[:, :kl], _pad_lanes(dkv[:, kl:]),
                                  _pad_lanes(_swap_halves(dkv[:, kl:]))], axis=1).astype(BF)
            wuq = mla_w_uq[a].reshape(ql, nh, MLA_NOPE + MLA_ROPE)
            wuq_r = wuq[:, :, MLA_NOPE:]
            w2 = jnp.concatenate([wuq[:, :, :MLA_NOPE].reshape(ql, nh * MLA_NOPE),
                                  _pad_lanes(wuq_r).reshape(ql, nh * LANES),
                                  _pad_lanes(_swap_halves(wuq_r)).reshape(ql, nh * LANES)], axis=1).astype(BF)
            w3 = jnp.transpose(mla_w_uk[a], (1, 2, 0)).astype(BF)
            wuv = jnp.transpose(mla_w_uv[a], (1, 0, 2)).astype(BF)
            w_o = mla_w_o[a].astype(BF)
            qn, kn = mla_q_norm[a][None], mla_kv_norm[a][None]

            row_p, kcat_p, qcat_p = _mla_pre(x_p, g[1][None], mp_l[3], mp_l[4], cos_p, sin_p,
                                             w1, qn, kn, w2, w3, seq_rows=seq)
            oh_p = _mla_flash(qcat_p, kcat_p, wuv, batch=batch, seq=seq, scale=mla_scale)
            x_p = _mm_resid(oh_p, w_o, x_p, mp_l[5], seq_rows=seq)

            row_s, kcat_s, qcat_s = _mla_pre(x_s, g[1][None], ms_l[3], ms_l[4], cos_s, sin_s,
                                             w1, qn, kn, w2, w3, seq_rows=1)
            o_lat_s = _mla_decode(page_table, jnp.transpose(qcat_s, (1, 0, 2)), kcat_s[:, None, :],
                                  cache_mla, layer=a, scale=mla_scale)
            oh_s = _uv(jnp.transpose(o_lat_s, (1, 0, 2)), wuv)
            x_s = _mm_resid(oh_s, w_o, x_s, ms_l[5], seq_rows=1)
            rows_p.append(row_p.reshape(batch, seq, kl + MLA_ROPE))
            rows_s.append(row_s.reshape(dec_batch, 1, kl + MLA_ROPE))
        else:
            b = l - n_a
            w_q = swa_w_q[b].astype(BF)
            w_o = swa_w_o[b].astype(BF)
            (q_p,) = _mod_mm(x_p, g[1][None], mp_l[3], mp_l[4], w_q, (BF,), seq_rows=seq)
            o_p = _swa_prompt(q_p, kv4_p, bias_pairs, mask2, swa_sinks[b], batch=batch, seq=seq, n_kv=n_kv,
                              scale=swa_scale)
            x_p = _mm_resid(o_p, w_o, x_p, mp_l[5], seq_rows=seq)

            (q_s,) = _mod_mm(x_s, g[1][None], ms_l[3], ms_l[4], w_q, (BF,), seq_rows=1)
            o_s = _swa_sample(q_s.reshape(dec_batch, swa_heads, SWA_HEAD_DIM), keys_s_bf, vals_s_bf, bias_s,
                              swa_sinks[b].reshape(n_kv, SWA_GROUP, 1), n_keys=n_keys_s, scale=swa_scale)
            x_s = _mm_resid(o_s.reshape(dec_batch, swa_heads * SWA_HEAD_DIM), w_o, x_s, ms_l[5], seq_rows=1)

        x_p = _ffn(x_p, g[2][None], mp_l[6], mp_l[7], mp_l[8], w_in[1], w_out[1], seq_rows=seq)
        x_s = _ffn(x_s, g[2][None], ms_l[6], ms_l[7], ms_l[8], w_in[1], w_out[1], seq_rows=1)

    y_prompt = _final_norm(x_p, final_norm_g[None]).reshape(batch, seq, d)
    y_sample = _final_norm(x_s, final_norm_g[None]).reshape(dec_batch, 1, d)
    new_mla_prompt = jnp.stack(rows_p, axis=1)
    new_mla_sample = jnp.stack(rows_s, axis=1)
    win_p = min(WINDOW, seq)
    return (y_prompt, y_sample, new_mla_prompt, new_mla_sample,
            k_p[:, seq - win_p:], v_p[:, seq - win_p:],
            keys_s[:, keys_s.shape[1] - win_s:], vals_s[:, vals_s.shape[1] - win_s:])
```

```python
import functools
import math

import jax
import jax.numpy as jnp
import numpy as np
from jax import lax
from jax.experimental import pallas as pl
from jax.experimental.pallas import tpu as pltpu

EPS = 1e-6
NEG_INF = -1e30
ROPE_THETA = 10000.0
PAGE_SIZE = 128
WINDOW = 128
BLOCK = 128
NUM_BUCKETS = 32
MAX_DISTANCE = 128
MLA_NOPE = 128
MLA_ROPE = 64
MLA_V = 128
SWA_HEAD_DIM = 64
SWA_GROUP = 8
N_SUB = 3

LANES = 128
V7X_VMEM_LIMIT_BYTES = 56 * 1024 * 1024

BF = jnp.bfloat16
F32 = jnp.float32


def _params(*sem):
    return pltpu.CompilerParams(dimension_semantics=sem, vmem_limit_bytes=V7X_VMEM_LIMIT_BYTES)


def _pick(n, prefs):
    for p in prefs:
        if p <= n and n % p == 0:
            return p
    return n


def _row_tile(m, seq_rows, prefs):
    return _pick(m if seq_rows == 1 else seq_rows, prefs)


def _const_spec(arr):
    nd = arr.ndim
    return pl.BlockSpec(arr.shape, lambda *_: (0,) * nd, pipeline_mode=pl.Buffered(1))


def _silu(x):
    return x * (1.0 / (1.0 + jnp.exp(-x)))


def _rms(x, g):
    return x * lax.rsqrt(jnp.mean(x * x, axis=-1, keepdims=True) + EPS) * g


def _modulate(x, g, shift, scale):
    return _rms(x, g) * (1.0 + scale) + shift


def _dot(a, b):
    return jnp.dot(a, b, preferred_element_type=F32)


def _dot_nt(a, b):
    return lax.dot_general(a, b, (((1,), (1,)), ((), ())), preferred_element_type=F32)


def _mod_spec(arr, tm, seq_rows):
    _, r, d = arr.shape
    if r == 1:
        return pl.BlockSpec((None, 1, d), lambda i, *_: ((i * tm) // seq_rows, 0, 0))
    return pl.BlockSpec((None, tm, d), lambda i, *_: (0, i, 0))


def _pos_spec(arr, tm):
    p = arr.shape[0]
    if p == 1:
        return pl.BlockSpec((1, LANES), lambda i, *_: (0, 0))
    nblk = p // tm
    return pl.BlockSpec((tm, LANES), lambda i, *_: (i % nblk, 0))


def _ada_kernel(c_ref, w_ref, b_ref, o_ref):
    a = _silu(c_ref[...]).astype(BF)
    o_ref[...] = _dot(a, w_ref[...].astype(BF)) + b_ref[...]


def _ada(c, w, b):
    nl, d, n = w.shape
    m = c.shape[0]
    tn = _pick(n, (1024, 512, 256, 128))
    return pl.pallas_call(
        _ada_kernel,
        grid=(nl, n // tn),
        in_specs=[
            pl.BlockSpec((m, d), lambda l, j: (0, 0)),
            pl.BlockSpec((None, d, tn), lambda l, j: (l, 0, j)),
            pl.BlockSpec((None, 1, tn), lambda l, j: (l, 0, j)),
        ],
        out_specs=pl.BlockSpec((None, m, tn), lambda l, j: (l, 0, j)),
        out_shape=jax.ShapeDtypeStruct((nl, m, n), F32),
        compiler_params=_params("parallel", "parallel"),
        name="ada_params",
    )(c, w, b)


def _ffn_kernel(x_ref, g_ref, sh_ref, sc_ref, gt_ref, wg_ref, wu_ref, wo_ref, o_ref, h_ref):
    j = pl.program_id(1)

    @pl.when(j == 0)
    def _():
        h_ref[...] = _modulate(x_ref[...], g_ref[...], sh_ref[...], sc_ref[...]).astype(BF)

    h = h_ref[...]
    gate = _dot(h, wg_ref[...])
    up = _dot(h, wu_ref[...])
    y = _dot((_silu(gate) * up).astype(BF), wo_ref[...])

    @pl.when(j == 0)
    def _():
        o_ref[...] = y

    @pl.when(j > 0)
    def _():
        o_ref[...] += y

    @pl.when(j == pl.num_programs(1) - 1)
    def _():
        o_ref[...] = x_ref[...] + 0.5 * gt_ref[...] * o_ref[...]


def _ffn(x, g, shift, scale, gate, w_in, w_out, *, seq_rows):
    m, d = x.shape
    f = w_out.shape[0]
    tm = _row_tile(m, seq_rows, (1024, 512, 256, 128))
    tf = _pick(f, (512, 256, 128))
    nf = f // tf
    ms = _mod_spec(shift, tm, seq_rows)
    once = pl.Buffered(1)
    return pl.pallas_call(
        _ffn_kernel,
        grid=(m // tm, nf),
        in_specs=[
            pl.BlockSpec((tm, d), lambda i, j: (i, 0), pipeline_mode=once),
            pl.BlockSpec((1, d), lambda i, j: (0, 0)),
            ms, ms, ms,
            pl.BlockSpec((d, tf), lambda i, j: (0, j)),
            pl.BlockSpec((d, tf), lambda i, j: (0, j + nf)),
            pl.BlockSpec((tf, d), lambda i, j: (j, 0)),
        ],
        out_specs=pl.BlockSpec((tm, d), lambda i, j: (i, 0), pipeline_mode=once),
        out_shape=jax.ShapeDtypeStruct((m, d), F32),
        scratch_shapes=[pltpu.VMEM((tm, d), BF)],
        compiler_params=_params("parallel", "arbitrary"),
        name="ffn_half",
    )(x, g, shift, scale, gate, w_in, w_in, w_out)


def _mod_mm_kernel(x_ref, g_ref, sh_ref, sc_ref, w_ref, *o_refs):
    h = _modulate(x_ref[...], g_ref[...], sh_ref[...], sc_ref[...]).astype(BF)
    y = _dot(h, w_ref[...])
    for o_ref in o_refs:
        o_ref[...] = y.astype(o_ref.dtype)


def _mod_mm(x, g, shift, scale, w, out_dtypes, *, seq_rows):
    m, d = x.shape
    n = w.shape[1]
    tm = _row_tile(m, seq_rows, (512, 256, 128))
    ms = _mod_spec(shift, tm, seq_rows)
    outs = pl.pallas_call(
        _mod_mm_kernel,
        grid=(m // tm,),
        in_specs=[
            pl.BlockSpec((tm, d), lambda i: (i, 0)),
            pl.BlockSpec((1, d), lambda i: (0, 0)),
            ms, ms,
            _const_spec(w),
        ],
        out_specs=[pl.BlockSpec((tm, n), lambda i: (i, 0)) for _ in out_dtypes],
        out_shape=[jax.ShapeDtypeStruct((m, n), dt) for dt in out_dtypes],
        compiler_params=_params("parallel"),
        name="mod_matmul",
    )(x, g, shift, scale, w)
    return outs


def _mm_resid_kernel(a_ref, w_ref, x_ref, gt_ref, o_ref):
    o_ref[...] = x_ref[...] + gt_ref[...] * _dot(a_ref[...], w_ref[...])


def _mm_resid(a, w, x, gate, *, seq_rows):
    m, k = a.shape
    n = w.shape[1]
    tm = _row_tile(m, seq_rows, (512, 256, 128))
    return pl.pallas_call(
        _mm_resid_kernel,
        grid=(m // tm,),
        in_specs=[
            pl.BlockSpec((tm, k), lambda i: (i, 0)),
            _const_spec(w),
            pl.BlockSpec((tm, n), lambda i: (i, 0)),
            _mod_spec(gate, tm, seq_rows),
        ],
        out_specs=pl.BlockSpec((tm, n), lambda i: (i, 0)),
        out_shape=jax.ShapeDtypeStruct((m, n), F32),
        compiler_params=_params("parallel"),
        name="matmul_residual",
    )(a, w, x, gate)


def _mla_pre_kernel(x_ref, g_ref, sh_ref, sc_ref, cos_ref, sin_ref, w1_ref, qn_ref, kn_ref, w2_ref, w3_ref,
                    row_ref, kcat_ref, qcat_ref, *, ql, kl, nh):
    cos_t = cos_ref[...]
    sin_t = sin_ref[...]
    h = _modulate(x_ref[...], g_ref[...], sh_ref[...], sc_ref[...]).astype(BF)
    p1 = _dot(h, w1_ref[...])
    cq = _rms(p1[:, :ql], qn_ref[...]).astype(BF)
    lat = _rms(p1[:, ql:ql + kl], kn_ref[...])
    a = ql + kl
    kr = p1[:, a:a + LANES] * cos_t + p1[:, a + LANES:a + 2 * LANES] * sin_t
    row_ref[:, :kl] = lat
    row_ref[:, kl:] = kr[:, :MLA_ROPE]
    kcat_ref[:, :kl] = lat.astype(BF)
    kcat_ref[:, kl:] = kr.astype(BF)
    q2 = _dot(cq, w2_ref[...])
    hn = nh * LANES
    for hd in range(nh):
        lo = hd * LANES
        qn = q2[:, lo:lo + LANES].astype(BF)
        qcat_ref[hd, :, :kl] = _dot(qn, w3_ref[hd]).astype(BF)
        qr = q2[:, hn + lo:hn + lo + LANES] * cos_t + q2[:, 2 * hn + lo:2 * hn + lo + LANES] * sin_t
        qcat_ref[hd, :, kl:] = qr.astype(BF)


def _mla_pre(x, g, shift, scale, cos_t, sin_t, w1, qnorm, knorm, w2, w3, *, seq_rows):
    m, d = x.shape
    ql = qnorm.shape[1]
    kl = knorm.shape[1]
    nh = w3.shape[0]
    c = kl + LANES
    tm = _row_tile(m, seq_rows, (256, 128))
    ms = _mod_spec(shift, tm, seq_rows)
    ps = _pos_spec(cos_t, tm)
    return pl.pallas_call(
        functools.partial(_mla_pre_kernel, ql=ql, kl=kl, nh=nh),
        grid=(m // tm,),
        in_specs=[
            pl.BlockSpec((tm, d), lambda i: (i, 0)),
            pl.BlockSpec((1, d), lambda i: (0, 0)),
            ms, ms, ps, ps,
            _const_spec(w1), _const_spec(qnorm), _const_spec(knorm), _const_spec(w2), _const_spec(w3),
        ],
        out_specs=[
            pl.BlockSpec((tm, kl + MLA_ROPE), lambda i: (i, 0)),
            pl.BlockSpec((tm, c), lambda i: (i, 0)),
            pl.BlockSpec((nh, tm, c), lambda i: (0, i, 0)),
        ],
        out_shape=[
            jax.ShapeDtypeStruct((m, kl + MLA_ROPE), F32),
            jax.ShapeDtypeStruct((m, c), BF),
            jax.ShapeDtypeStruct((nh, m, c), BF),
        ],
        compiler_params=_params("parallel"),
        name="mla_project",
    )(x, g, shift, scale, cos_t, sin_t, w1, qnorm, knorm, w2, w3)


def _mla_flash_kernel(q_ref, k_ref, wuv_ref, o_ref, m_sc, l_sc, acc_sc, *, tq, tk, kl, nh, scale):
    i = pl.program_id(1)
    j = pl.program_id(2)
    rows = nh * tq
    j_last = ((i + 1) * tq - 1) // tk

    @pl.when(j == 0)
    def _():
        m_sc[...] = jnp.full(m_sc.shape, NEG_INF, F32)
        l_sc[...] = jnp.zeros(l_sc.shape, F32)
        acc_sc[...] = jnp.zeros(acc_sc.shape, F32)

    def step(masked):
        q = q_ref[...].reshape(rows, q_ref.shape[-1])
        k = k_ref[...]
        s = _dot_nt(q, k) * scale
        if masked:
            qpos = i * tq + (lax.broadcasted_iota(jnp.int32, (rows, tk), 0) & (tq - 1))
            kpos = j * tk + lax.broadcasted_iota(jnp.int32, (rows, tk), 1)
            s = jnp.where(kpos <= qpos, s, NEG_INF)
        m_old = m_sc[...]
        m_new = jnp.maximum(m_old, jnp.max(s, axis=-1, keepdims=True))
        alpha = jnp.exp(m_old - m_new)
        p = jnp.exp(s - m_new)
        l_sc[...] = alpha * l_sc[...] + jnp.sum(p, axis=-1, keepdims=True)
        acc_sc[...] = alpha * acc_sc[...] + _dot(p.astype(BF), k[:, :kl])
        m_sc[...] = m_new

    needs_mask = (j + 1) * tk - 1 > i * tq

    @pl.when((j <= j_last) & needs_mask)
    def _():
        step(True)

    @pl.when((j <= j_last) & jnp.logical_not(needs_mask))
    def _():
        step(False)

    @pl.when(j == j_last)
    def _():
        o = (acc_sc[...] * (1.0 / l_sc[...])).astype(BF)
        for hd in range(nh):
            o_ref[:, hd * MLA_V:(hd + 1) * MLA_V] = _dot(o[hd * tq:(hd + 1) * tq], wuv_ref[hd]).astype(BF)


def _mla_flash(qcat, kcat, wuv, *, batch, seq, scale):
    nh, m, c = qcat.shape
    kl = wuv.shape[1]
    tq = _pick(seq, (128,))
    tk = _pick(seq, (512, 256, 128))
    nq = seq // tq
    nk = seq // tk

    def k_map(b, i, j):
        return (b * nk + jnp.minimum(j, ((i + 1) * tq - 1) // tk), 0)

    return pl.pallas_call(
        functools.partial(_mla_flash_kernel, tq=tq, tk=tk, kl=kl, nh=nh, scale=scale),
        grid=(batch, nq, nk),
        in_specs=[
            pl.BlockSpec((nh, tq, c), lambda b, i, j: (0, b * nq + i, 0)),
            pl.BlockSpec((tk, c), k_map),
            _const_spec(wuv),
        ],
        out_specs=pl.BlockSpec((tq, nh * MLA_V), lambda b, i, j: (b * nq + i, 0)),
        out_shape=jax.ShapeDtypeStruct((m, nh * MLA_V), BF),
        scratch_shapes=[
            pltpu.VMEM((nh * tq, 1), F32),
            pltpu.VMEM((nh * tq, 1), F32),
            pltpu.VMEM((nh * tq, kl), F32),
        ],
        compiler_params=_params("parallel", "parallel", "arbitrary"),
        name="mla_prompt_attention",
    )(qcat, kcat, wuv)


def _mla_decode_kernel(pt_ref, q_ref, new_ref, *rest, npg, kl, scale):
    page_refs = rest[:npg]
    o_ref = rest[npg]
    kbuf, m_sc, l_sc, acc_sc = rest[npg + 1:]
    c = pl.program_id(1)

    @pl.when(c == 0)
    def _():
        m_sc[...] = jnp.full(m_sc.shape, NEG_INF, F32)
        l_sc[...] = jnp.zeros(l_sc.shape, F32)
        acc_sc[...] = jnp.zeros(acc_sc.shape, F32)

    for k, ref in enumerate(page_refs):
        kbuf[:, k * PAGE_SIZE:(k + 1) * PAGE_SIZE] = ref[...].astype(BF)

    q = q_ref[...]
    keys_t = kbuf[...]
    s = (_dot(q[:, :kl], keys_t[:kl]) + _dot(q[:, kl:kl + MLA_ROPE], keys_t[kl:])) * scale
    m_old = m_sc[...]
    m_new = jnp.maximum(m_old, jnp.max(s, axis=-1, keepdims=True))
    alpha = jnp.exp(m_old - m_new)
    p = jnp.exp(s - m_new)
    l_sc[...] = alpha * l_sc[...] + jnp.sum(p, axis=-1, keepdims=True)
    acc_sc[...] = alpha * acc_sc[...] + _dot_nt(p.astype(BF), keys_t[:kl])
    m_sc[...] = m_new

    @pl.when(c == pl.num_programs(1) - 1)
    def _():
        new = new_ref[...].astype(F32)
        s_new = jnp.sum(q.astype(F32) * new, axis=-1, keepdims=True) * scale
        m_old = m_sc[...]
        m_new = jnp.maximum(m_old, s_new)
        alpha = jnp.exp(m_old - m_new)
        p_new = jnp.exp(s_new - m_new)
        l_new = alpha * l_sc[...] + p_new
        acc = alpha * acc_sc[...] + p_new.astype(BF).astype(F32) * new[:, :kl]
        o_ref[...] = (acc * (1.0 / l_new)).astype(BF)


def _mla_decode(page_table, qcat, kcat, cache_t, *, layer, scale):
    nb, nh, c = qcat.shape
    lat = cache_t.shape[2]
    kl = lat - MLA_ROPE
    n_pages = page_table.shape[1]
    npg = _pick(n_pages, (32, 16, 8, 4, 2, 1))
    pt = page_table.reshape(-1)

    def page_spec(k):
        return pl.BlockSpec((None, None, lat, PAGE_SIZE),
                            lambda b, ch, pt_ref: (pt_ref[b * n_pages + ch * npg + k], layer, 0, 0))

    grid_spec = pltpu.PrefetchScalarGridSpec(
        num_scalar_prefetch=1,
        grid=(nb, n_pages // npg),
        in_specs=[
            pl.BlockSpec((None, nh, c), lambda b, ch, pt_ref: (b, 0, 0)),
            pl.BlockSpec((None, 1, c), lambda b, ch, pt_ref: (b, 0, 0)),
        ] + [page_spec(k) for k in range(npg)],
        out_specs=pl.BlockSpec((None, nh, kl), lambda b, ch, pt_ref: (b, 0, 0)),
        scratch_shapes=[
            pltpu.VMEM((lat, npg * PAGE_SIZE), BF),
            pltpu.VMEM((nh, 1), F32),
            pltpu.VMEM((nh, 1), F32),
            pltpu.VMEM((nh, kl), F32),
        ],
    )
    return pl.pallas_call(
        functools.partial(_mla_decode_kernel, npg=npg, kl=kl, scale=scale),
        grid_spec=grid_spec,
        out_shape=jax.ShapeDtypeStruct((nb, nh, kl), BF),
        compiler_params=_params("parallel", "arbitrary"),
        name="mla_sample_attention",
    )(pt, qcat, kcat, *([cache_t] * npg))


def _uv_kernel(o_ref, wuv_ref, out_ref):
    nh = o_ref.shape[0]
    for hd in range(nh):
        out_ref[:, hd * MLA_V:(hd + 1) * MLA_V] = _dot(o_ref[hd], wuv_ref[hd]).astype(BF)


def _uv(o_lat, wuv):
    nh, m, kl = o_lat.shape
    return pl.pallas_call(
        _uv_kernel,
        grid=(1,),
        in_specs=[
            pl.BlockSpec(o_lat.shape, lambda i: (0, 0, 0)),
            pl.BlockSpec(wuv.shape, lambda i: (0, 0, 0)),
        ],
        out_specs=pl.BlockSpec((m, nh * MLA_V), lambda i: (0, 0)),
        out_shape=jax.ShapeDtypeStruct((m, nh * MLA_V), BF),
        compiler_params=_params("arbitrary"),
        name="mla_value_up",
    )(o_lat, wuv)


def _swa_prompt_kernel(sink_ref, q_ref, kvp_ref, kvc_ref, bias_ref, mask_ref, o_ref, *, n_kv, scale):
    kvw = n_kv * LANES
    pairs_per_kv = SWA_GROUP // 2
    mask = mask_ref[...] > 0.5
    nkeys = 2 * BLOCK
    for c in range(n_kv):
        def sect(s):
            lo = s * kvw + c * LANES
            return jnp.concatenate([kvp_ref[:, lo:lo + LANES], kvc_ref[:, lo:lo + LANES]], axis=0)
        k2 = jnp.concatenate([sect(0), sect(1)], axis=0)
        v2 = jnp.concatenate([sect(2), sect(3)], axis=0)
        for pp in range(pairs_per_kv):
            pr = c * pairs_per_kv + pp
            qp = q_ref[:, pr * LANES:(pr + 1) * LANES]
            s = _dot_nt(qp, k2) * scale + bias_ref[pr]
            s = jnp.where(mask, s, NEG_INF)
            halves = []
            for hf in range(2):
                sh = s[:, hf * nkeys:(hf + 1) * nkeys]
                sink = sink_ref[2 * pr + hf]
                m = jnp.maximum(jnp.max(sh, axis=-1, keepdims=True), sink)
                e = jnp.exp(sh - m)
                den = jnp.sum(e, axis=-1, keepdims=True) + jnp.exp(sink - m)
                halves.append((e * (1.0 / den)).astype(BF))
            p = jnp.concatenate(halves, axis=1)
            o_ref[:, pr * LANES:(pr + 1) * LANES] = _dot(p, v2).astype(BF)


def _swa_prompt(q, kv4, bias_pairs, mask2, sinks, *, batch, seq, n_kv, scale):
    m, dq = q.shape
    nb = seq // BLOCK
    w4 = kv4.shape[1]
    return pl.pallas_call(
        functools.partial(_swa_prompt_kernel, n_kv=n_kv, scale=scale),
        grid=(batch, nb),
        in_specs=[
            pl.BlockSpec(memory_space=pltpu.SMEM),
            pl.BlockSpec((BLOCK, dq), lambda b, j: (b * nb + j, 0)),
            pl.BlockSpec((BLOCK, w4), lambda b, j: (b * nb + jnp.maximum(j - 1, 0), 0)),
            pl.BlockSpec((BLOCK, w4), lambda b, j: (b * nb + j, 0)),
            _const_spec(bias_pairs),
            pl.BlockSpec((None, BLOCK, 4 * BLOCK), lambda b, j: (jnp.minimum(j, 1), 0, 0)),
        ],
        out_specs=pl.BlockSpec((BLOCK, dq), lambda b, j: (b * nb + j, 0)),
        out_shape=jax.ShapeDtypeStruct((m, dq), BF),
        compiler_params=_params("parallel", "arbitrary"),
        name="swa_prompt_attention",
    )(sinks, q, kv4, kv4, bias_pairs, mask2)


def _swa_sample_kernel(q_ref, k_ref, v_ref, bias_ref, sink_ref, o_ref, *, n_kv, n_keys, scale):
    for c in range(n_kv):
        q = q_ref[:, c * SWA_GROUP:(c + 1) * SWA_GROUP, :]
        k = k_ref[:, c]
        v = v_ref[:, c]
        s = jnp.einsum("bgd,bkd->bgk", q, k, preferred_element_type=F32) * scale + bias_ref[c][None]
        col = lax.broadcasted_iota(jnp.int32, s.shape, 2)
        s = jnp.where(col < n_keys, s, NEG_INF)
        sink = sink_ref[c][None]
        m = jnp.maximum(jnp.max(s, axis=-1, keepdims=True), sink)
        e = jnp.exp(s - m)
        den = jnp.sum(e, axis=-1, keepdims=True) + jnp.exp(sink - m)
        p = (e * (1.0 / den)).astype(BF)
        o = jnp.einsum("bgk,bkd->bgd", p, v, preferred_element_type=F32)
        o_ref[:, c * SWA_GROUP:(c + 1) * SWA_GROUP, :] = o.astype(BF)


def _swa_sample(q, keys, vals, bias, sinks, *, n_keys, scale):
    nb, nh, dh = q.shape
    n_kv, kp = keys.shape[1], keys.shape[2]
    bb = _pick(nb, (16, 8))
    return pl.pallas_call(
        functools.partial(_swa_sample_kernel, n_kv=n_kv, n_keys=n_keys, scale=scale),
        grid=(nb // bb,),
        in_specs=[
            pl.BlockSpec((bb, nh, dh), lambda i: (i, 0, 0)),
            pl.BlockSpec((bb, n_kv, kp, dh), lambda i: (i, 0, 0, 0)),
            pl.BlockSpec((bb, n_kv, kp, dh), lambda i: (i, 0, 0, 0)),
            pl.BlockSpec(bias.shape, lambda i: (0, 0, 0)),
            pl.BlockSpec(sinks.shape, lambda i: (0, 0, 0)),
        ],
        out_specs=pl.BlockSpec((bb, nh, dh), lambda i: (i, 0, 0)),
        out_shape=jax.ShapeDtypeStruct((nb, nh, dh), BF),
        compiler_params=_params("parallel"),
        name="swa_sample_attention",
    )(q, keys, vals, bias, sinks)


def _final_norm_kernel(x_ref, g_ref, o_ref):
    o_ref[...] = _rms(x_ref[...], g_ref[...])


def _final_norm(x, g):
    m, d = x.shape
    tm = _pick(m, (512, 256, 128))
    return pl.pallas_call(
        _final_norm_kernel,
        grid=(m // tm,),
        in_specs=[pl.BlockSpec((tm, d), lambda i: (i, 0)), pl.BlockSpec((1, d), lambda i: (0, 0))],
        out_specs=pl.BlockSpec((tm, d), lambda i: (i, 0)),
        out_shape=jax.ShapeDtypeStruct((m, d), F32),
        compiler_params=_params("parallel"),
        name="final_norm",
    )(x, g)


def _rope_tables(pos):
    half = MLA_ROPE // 2
    inv = ROPE_THETA ** (-jnp.arange(half, dtype=F32) / half)
    ang = pos.astype(F32)[:, None] * inv
    cos, sin = jnp.cos(ang), jnp.sin(ang)
    pad = jnp.zeros((pos.shape[0], LANES - MLA_ROPE), F32)
    return jnp.concatenate([cos, cos, pad], axis=1), jnp.concatenate([-sin, sin, pad], axis=1)


def _swap_halves(w):
    half = w.shape[-1] // 2
    return jnp.concatenate([w[..., half:], w[..., :half]], axis=-1)


def _pad_lanes(w):
    return jnp.pad(w, [(0, 0)] * (w.ndim - 1) + [(0, LANES - w.shape[-1])])


def _t5_bucket(dist):
    max_exact = NUM_BUCKETS // 2
    n = jnp.maximum(dist, 0)
    nf = jnp.maximum(n, 1).astype(F32)
    large = max_exact + (jnp.log(nf / max_exact) / math.log(MAX_DISTANCE / max_exact)
                         * (NUM_BUCKETS - max_exact)).astype(jnp.int32)
    large = jnp.minimum(large, NUM_BUCKETS - 1)
    return jnp.where(n < max_exact, n, large)


def kernel(x_prompt, x_sample, c_prompt, c_sample, cache_mla, state_swa_k, state_swa_v, page_table, ada_w, ada_b, norm_g, ffn_w_in, ffn_w_out, mla_w_dq, mla_q_norm, mla_w_uq, mla_w_dkv, mla_kv_norm, mla_w_uk, mla_w_uv, mla_w_o, kv_ada_w, kv_ada_b, kv_norm_g, kv_w, swa_w_q, swa_w_o, swa_sinks, rel_bias_table, final_norm_g):
    batch, seq, d = x_prompt.shape
    dec_batch, dec_seq, _ = x_sample.shape
    assert dec_seq == 1
    depth = ada_w.shape[0]
    n_a = mla_w_dq.shape[0]
    n_pages = page_table.shape[1]
    past_len = n_pages * PAGE_SIZE
    ql = mla_w_dq.shape[2]
    kl = mla_kv_norm.shape[1]
    nh = mla_w_uk.shape[2]
    swa_heads = swa_w_q.shape[2] // SWA_HEAD_DIM
    n_kv = swa_heads // SWA_GROUP
    win_s = state_swa_k.shape[1]
    assert win_s == WINDOW and seq % BLOCK == 0
    mla_scale = float((MLA_NOPE + MLA_ROPE) ** -0.5)
    swa_scale = float(SWA_HEAD_DIM ** -0.5)
    mp = batch * seq

    c_all = jnp.concatenate([c_prompt, c_sample], axis=0)
    n_seq = c_all.shape[0]
    c_all = jnp.pad(c_all, ((0, (-n_seq) % 16), (0, 0)))
    mod = _ada(c_all, ada_w, ada_b[:, None, :])
    kvmod = _ada(c_all, kv_ada_w[None], kv_ada_b[None, None, :])[0]

    def mods_p(arr, n):
        return arr[:batch, n * d:(n + 1) * d][:, None, :]

    def mods_s(arr, n):
        return arr[batch:batch + dec_batch, n * d:(n + 1) * d][None]

    cos_p, sin_p = _rope_tables(jnp.arange(seq, dtype=jnp.int32))
    cos_s, sin_s = _rope_tables(jnp.full((1,), past_len, jnp.int32))

    qi = jnp.arange(BLOCK, dtype=jnp.int32)
    km = jnp.arange(2 * BLOCK, dtype=jnp.int32) - BLOCK
    d_p = qi[:, None] - km[None, :]
    win_mask = (d_p >= 0) & (d_p <= WINDOW)
    onehot_p = (_t5_bucket(d_p)[:, :, None] == jnp.arange(NUM_BUCKETS, dtype=jnp.int32)).astype(F32)
    bias_p = jnp.einsum("qkb,bh->hqk", onehot_p, rel_bias_table.astype(F32),
                        precision=lax.Precision.HIGHEST)
    bias_pairs = bias_p.reshape(swa_heads // 2, 2, BLOCK, 2 * BLOCK).transpose(0, 2, 1, 3)
    bias_pairs = bias_pairs.reshape(swa_heads // 2, BLOCK, 4 * BLOCK)
    mask_first = win_mask & (km >= 0)[None, :]
    mask2 = jnp.stack([jnp.tile(mask_first, (1, 2)), jnp.tile(win_mask, (1, 2))]).astype(F32)
    kp_pad = 2 * WINDOW
    n_keys_s = win_s + 1
    d_s = jnp.concatenate([jnp.arange(win_s, 0, -1, dtype=jnp.int32), jnp.zeros((1,), jnp.int32)])
    bias_s = jnp.transpose(rel_bias_table[_t5_bucket(d_s)], (1, 0)).astype(F32)
    bias_s = jnp.pad(bias_s, ((0, 0), (0, kp_pad - n_keys_s))).reshape(n_kv, SWA_GROUP, kp_pad)

    cache_t = jnp.transpose(cache_mla, (0, 1, 3, 2))

    x_p = x_prompt.reshape(mp, d)
    x_s = x_sample.reshape(dec_batch, d)
    rows_p, rows_s = [], []
    kv4_p = keys_s = vals_s = keys_s_bf = vals_s_bf = None
    k_p = v_p = None

    for l in range(depth):
        if l == n_a:
            kv_k = kv_w[:, :n_kv * SWA_HEAD_DIM].reshape(d, n_kv, SWA_HEAD_DIM)
            kv_v = kv_w[:, n_kv * SWA_HEAD_DIM:].reshape(d, n_kv, SWA_HEAD_DIM)
            zero = jnp.zeros_like(kv_k)
            lo = lambda w: jnp.concatenate([w, zero], axis=-1).reshape(d, n_kv * LANES)
            hi = lambda w: jnp.concatenate([zero, w], axis=-1).reshape(d, n_kv * LANES)
            nkv = 2 * n_kv * SWA_HEAD_DIM
            w_sh = jnp.concatenate([kv_w, lo(kv_k), hi(kv_k), lo(kv_v), hi(kv_v)], axis=1).astype(BF)
            kvg = kv_norm_g[None]
            kv_f32_p, kv_bf_p = _mod_mm(x_p, kvg, mods_p(kvmod, 0), mods_p(kvmod, 1), w_sh, (F32, BF),
                                        seq_rows=seq)
            k_p = kv_f32_p[:, :nkv // 2].reshape(batch, seq, n_kv, SWA_HEAD_DIM)
            v_p = kv_f32_p[:, nkv // 2:nkv].reshape(batch, seq, n_kv, SWA_HEAD_DIM)
            kv4_p = kv_bf_p[:, nkv:]
            (kv_f32_s,) = _mod_mm(x_s, kvg, mods_s(kvmod, 0), mods_s(kvmod, 1), w_sh[:, :nkv], (F32,),
                                  seq_rows=1)
            k_new = kv_f32_s[:, :nkv // 2].reshape(dec_batch, 1, n_kv, SWA_HEAD_DIM)
            v_new = kv_f32_s[:, nkv // 2:].reshape(dec_batch, 1, n_kv, SWA_HEAD_DIM)
            keys_s = jnp.concatenate([state_swa_k, k_new], axis=1)
            vals_s = jnp.concatenate([state_swa_v, v_new], axis=1)
            padk = lambda t: jnp.pad(jnp.transpose(t, (0, 2, 1, 3)).astype(BF),
                                     ((0, 0), (0, 0), (0, kp_pad - n_keys_s), (0, 0)))
            keys_s_bf, vals_s_bf = padk(keys_s), padk(vals_s)

        g = norm_g[l]
        w_in = [ffn_w_in[l, t].astype(BF) for t in range(2)]
        w_out = [ffn_w_out[l, t].astype(BF) for t in range(2)]
        mp_l = [mods_p(mod[l], n) for n in range(3 * N_SUB)]
        ms_l = [mods_s(mod[l], n) for n in range(3 * N_SUB)]

        x_p = _ffn(x_p, g[0][None], mp_l[0], mp_l[1], mp_l[2], w_in[0], w_out[0], seq_rows=seq)
        x_s = _ffn(x_s, g[0][None], ms_l[0], ms_l[1], ms_l[2], w_in[0], w_out[0], seq_rows=1)

        if l < n_a:
            a = l
            dkv = mla_w_dkv[a]
            w1 = jnp.concatenate([mla_w_dq[a], dkv[:, :kl], _pad_lanes(dkv[:, kl:]),
                                  _pad_lanes(_swap_halves(dkv[:, kl:]))], axis=1).astype(BF)
            wuq = mla_w_uq[a].reshape(ql, nh, MLA_NOPE + MLA_ROPE)
            wuq_r = wuq[:, :, MLA_NOPE:]
            w2 = jnp.concatenate([wuq[:, :, :MLA_NOPE].reshape(ql, nh * MLA_NOPE),
                                  _pad_lanes(wuq_r).reshape(ql, nh * LANES),
                                  _pad_lanes(_swap_halves(wuq_r)).reshape(ql, nh * LANES)], axis=1).astype(BF)
            w3 = jnp.transpose(mla_w_uk[a], (1, 2, 0)).astype(BF)
            wuv = jnp.transpose(mla_w_uv[a], (1, 0, 2)).astype(BF)
            w_o = mla_w_o[a].astype(BF)
            qn, kn = mla_q_norm[a][None], mla_kv_norm[a][None]

            row_p, kcat_p, qcat_p = _mla_pre(x_p, g[1][None], mp_l[3], mp_l[4], cos_p, sin_p,
                                             w1, qn, kn, w2, w3, seq_rows=seq)
            oh_p = _mla_flash(qcat_p, kcat_p, wuv, batch=batch, seq=seq, scale=mla_scale)
            x_p = _mm_resid(oh_p, w_o, x_p, mp_l[5], seq_rows=seq)

            row_s, kcat_s, qcat_s = _mla_pre(x_s, g[1][None], ms_l[3], ms_l[4], cos_s, sin_s,
                                             w1, qn, kn, w2, w3, seq_rows=1)
            o_lat_s = _mla_decode(page_table, jnp.transpose(qcat_s, (1, 0, 2)), kcat_s[:, None, :],
                                  cache_t, layer=a, scale=mla_scale)
            oh_s = _uv(jnp.transpose(o_lat_s, (1, 0, 2)), wuv)
            x_s = _mm_resid(oh_s, w_o, x_s, ms_l[5], seq_rows=1)
            rows_p.append(row_p.reshape(batch, seq, kl + MLA_ROPE))
            rows_s.append(row_s.reshape(dec_batch, 1, kl + MLA_ROPE))
        else:
            b = l - n_a
            w_q = swa_w_q[b].astype(BF)
            w_o = swa_w_o[b].astype(BF)
            (q_p,) = _mod_mm(x_p, g[1][None], mp_l[3], mp_l[4], w_q, (BF,), seq_rows=seq)
            o_p = _swa_prompt(q_p, kv4_p, bias_pairs, mask2, swa_sinks[b], batch=batch, seq=seq, n_kv=n_kv,
                              scale=swa_scale)
            x_p = _mm_resid(o_p, w_o, x_p, mp_l[5], seq_rows=seq)

            (q_s,) = _mod_mm(x_s, g[1][None], ms_l[3], ms_l[4], w_q, (BF,), seq_rows=1)
            o_s = _swa_sample(q_s.reshape(dec_batch, swa_heads, SWA_HEAD_DIM), keys_s_bf, vals_s_bf, bias_s,
                              swa_sinks[b].reshape(n_kv, SWA_GROUP, 1), n_keys=n_keys_s, scale=swa_scale)
            x_s = _mm_resid(o_s.reshape(dec_batch, swa_heads * SWA_HEAD_DIM), w_o, x_s, ms_l[5], seq_rows=1)

        x_p = _ffn(x_p, g[2][None], mp_l[6], mp_l[7], mp_l[8], w_in[1], w_out[1], seq_rows=seq)
        x_s = _ffn(x_s, g[2][None], ms_l[6], ms_l[7], ms_l[8], w_in[1], w_out[1], seq_rows=1)

    y_prompt = _final_norm(x_p, final_norm_g[None]).reshape(batch, seq, d)
    y_sample = _final_norm(x_s, final_norm_g[None]).reshape(dec_batch, 1, d)
    new_mla_prompt = jnp.stack(rows_p, axis=1)
    new_mla_sample = jnp.stack(rows_s, axis=1)
    win_p = min(WINDOW, seq)
    return (y_prompt, y_sample, new_mla_prompt, new_mla_sample,
            k_p[:, seq - win_p:], v_p[:, seq - win_p:],
            keys_s[:, keys_s.shape[1] - win_s:], vals_s[:, vals_s.shape[1] - win_s:])
```

```python
import functools
import math

import jax
import jax.numpy as jnp
import numpy as np
from jax import lax
from jax.experimental import pallas as pl
from jax.experimental.pallas import tpu as pltpu

EPS = 1e-6
NEG_INF = -1e30
ROPE_THETA = 10000.0
PAGE_SIZE = 128
WINDOW = 128
BLOCK = 128
NUM_BUCKETS = 32
MAX_DISTANCE = 128
MLA_NOPE = 128
MLA_ROPE = 64
MLA_V = 128
SWA_HEAD_DIM = 64
SWA_GROUP = 8
N_SUB = 3

LANES = 128
V7X_VMEM_LIMIT_BYTES = 56 * 1024 * 1024

BF = jnp.bfloat16
F32 = jnp.float32


def _params(*sem):
    return pltpu.CompilerParams(dimension_semantics=sem, vmem_limit_bytes=V7X_VMEM_LIMIT_BYTES)


def _pick(n, prefs):
    for p in prefs:
        if p <= n and n % p == 0:
            return p
    return n


def _row_tile(m, seq_rows, prefs):
    return _pick(m if seq_rows == 1 else seq_rows, prefs)


def _const_spec(arr):
    nd = arr.ndim
    return pl.BlockSpec(arr.shape, lambda *_: (0,) * nd, pipeline_mode=pl.Buffered(1))


def _silu(x):
    return x * (1.0 / (1.0 + jnp.exp(-x)))


def _rms(x, g):
    return x * lax.rsqrt(jnp.mean(x * x, axis=-1, keepdims=True) + EPS) * g


def _modulate(x, g, shift, scale):
    return _rms(x, g) * (1.0 + scale) + shift


def _dot(a, b):
    return jnp.dot(a, b, preferred_element_type=F32)


def _dot_nt(a, b):
    return lax.dot_general(a, b, (((1,), (1,)), ((), ())), preferred_element_type=F32)


def _mod_spec(arr, tm, seq_rows):
    _, r, d = arr.shape
    if r == 1:
        return pl.BlockSpec((None, 1, d), lambda i, *_: ((i * tm) // seq_rows, 0, 0))
    return pl.BlockSpec((None, tm, d), lambda i, *_: (0, i, 0))


def _pos_spec(arr, tm):
    p = arr.shape[0]
    if p == 1:
        return pl.BlockSpec((1, LANES), lambda i, *_: (0, 0))
    nblk = p // tm
    return pl.BlockSpec((tm, LANES), lambda i, *_: (i % nblk, 0))


def _ada_kernel(c_ref, w_ref, b_ref, o_ref):
    a = _silu(c_ref[...]).astype(BF)
    o_ref[...] = _dot(a, w_ref[...].astype(BF)) + b_ref[...]


def _ada(c, w, b):
    nl, d, n = w.shape
    m = c.shape[0]
    tn = _pick(n, (1024, 512, 256, 128))
    return pl.pallas_call(
        _ada_kernel,
        grid=(nl, n // tn),
        in_specs=[
            pl.BlockSpec((m, d), lambda l, j: (0, 0)),
            pl.BlockSpec((None, d, tn), lambda l, j: (l, 0, j)),
            pl.BlockSpec((None, 1, tn), lambda l, j: (l, 0, j)),
        ],
        out_specs=pl.BlockSpec((None, m, tn), lambda l, j: (l, 0, j)),
        out_shape=jax.ShapeDtypeStruct((nl, m, n), F32),
        compiler_params=_params("parallel", "parallel"),
        name="ada_params",
    )(c, w, b)


def _ffn_kernel(x_ref, g_ref, sh_ref, sc_ref, wg_ref, wu_ref, wo_ref, xc_ref, gt_ref, o_ref, h_ref, a_ref, *, nf, tf):
    j = pl.program_id(1)

    @pl.when(j == 0)
    def _():
        h_ref[...] = _modulate(x_ref[...], g_ref[...], sh_ref[...], sc_ref[...]).astype(BF)

    @pl.when(j < nf)
    def _():
        h = h_ref[...]
        gate = _dot(h, wg_ref[...])
        up = _dot(h, wu_ref[...])
        a_ref[j] = (_silu(gate) * up).astype(BF)

    @pl.when(j >= nf)
    def _():
        y = _dot(a_ref[0], wo_ref[0:tf, :])
        for c in range(1, nf):
            y = y + _dot(a_ref[c], wo_ref[c * tf:(c + 1) * tf, :])
        o_ref[...] = xc_ref[...] + 0.5 * gt_ref[...] * y


def _ffn(x, g, shift, scale, gate, w_in, w_out, layer, half, *, seq_rows):
    m, d = x.shape
    f = w_out.shape[2]
    tm = _row_tile(m, seq_rows, (512, 256, 128))
    tf = _pick(f, (512, 256, 128))
    tn = _pick(d, (512, 256, 128))
    nf = f // tf
    ms = _mod_spec(shift, tm, seq_rows)
    col = lambda j: jnp.maximum(j - nf, 0)
    if gate.shape[1] == 1:
        gate_spec = pl.BlockSpec((None, 1, tn), lambda i, j: ((i * tm) // seq_rows, 0, col(j)))
    else:
        gate_spec = pl.BlockSpec((None, tm, tn), lambda i, j: (0, i, col(j)))
    return pl.pallas_call(
        functools.partial(_ffn_kernel, nf=nf, tf=tf),
        grid=(m // tm, nf + d // tn),
        in_specs=[
            pl.BlockSpec((tm, d), lambda i, j: (i, 0)),
            pl.BlockSpec((1, d), lambda i, j: (0, 0)),
            ms, ms,
            pl.BlockSpec((None, None, d, tf), lambda i, j: (layer, half, 0, jnp.minimum(j, nf - 1))),
            pl.BlockSpec((None, None, d, tf), lambda i, j: (layer, half, 0, jnp.minimum(j, nf - 1) + nf)),
            pl.BlockSpec((None, None, f, tn), lambda i, j: (layer, half, 0, col(j))),
            pl.BlockSpec((tm, tn), lambda i, j: (i, col(j))),
            gate_spec,
        ],
        out_specs=pl.BlockSpec((tm, tn), lambda i, j: (i, col(j))),
        out_shape=jax.ShapeDtypeStruct((m, d), F32),
        scratch_shapes=[pltpu.VMEM((tm, d), BF), pltpu.VMEM((nf, tm, tf), BF)],
        compiler_params=_params("parallel", "arbitrary"),
        name="ffn_half",
    )(x, g, shift, scale, w_in, w_in, w_out, x, gate)


def _mod_mm_kernel(x_ref, g_ref, sh_ref, sc_ref, w_ref, *o_refs):
    h = _modulate(x_ref[...], g_ref[...], sh_ref[...], sc_ref[...]).astype(BF)
    y = _dot(h, w_ref[...])
    for o_ref in o_refs:
        o_ref[...] = y.astype(o_ref.dtype)


def _mod_mm(x, g, shift, scale, w, out_dtypes, *, seq_rows):
    m, d = x.shape
    n = w.shape[1]
    tm = _row_tile(m, seq_rows, (512, 256, 128))
    ms = _mod_spec(shift, tm, seq_rows)
    outs = pl.pallas_call(
        _mod_mm_kernel,
        grid=(m // tm,),
        in_specs=[
            pl.BlockSpec((tm, d), lambda i: (i, 0)),
            pl.BlockSpec((1, d), lambda i: (0, 0)),
            ms, ms,
            _const_spec(w),
        ],
        out_specs=[pl.BlockSpec((tm, n), lambda i: (i, 0)) for _ in out_dtypes],
        out_shape=[jax.ShapeDtypeStruct((m, n), dt) for dt in out_dtypes],
        compiler_params=_params("parallel"),
        name="mod_matmul",
    )(x, g, shift, scale, w)
    return outs


def _mm_resid_kernel(a_ref, w_ref, x_ref, gt_ref, o_ref):
    o_ref[...] = x_ref[...] + gt_ref[...] * _dot(a_ref[...], w_ref[...])


def _mm_resid(a, w, x, gate, *, seq_rows):
    m, k = a.shape
    n = w.shape[1]
    tm = _row_tile(m, seq_rows, (512, 256, 128))
    return pl.pallas_call(
        _mm_resid_kernel,
        grid=(m // tm,),
        in_specs=[
            pl.BlockSpec((tm, k), lambda i: (i, 0)),
            _const_spec(w),
            pl.BlockSpec((tm, n), lambda i: (i, 0)),
            _mod_spec(gate, tm, seq_rows),
        ],
        out_specs=pl.BlockSpec((tm, n), lambda i: (i, 0)),
        out_shape=jax.ShapeDtypeStruct((m, n), F32),
        compiler_params=_params("parallel"),
        name="matmul_residual",
    )(a, w, x, gate)


def _mla_pre_kernel(x_ref, g_ref, sh_ref, sc_ref, cos_ref, sin_ref, w1_ref, qn_ref, kn_ref, w2_ref, w3_ref,
                    row_ref, kcat_ref, qcat_ref, *, ql, kl, nh):
    cos_t = cos_ref[...]
    sin_t = sin_ref[...]
    h = _modulate(x_ref[...], g_ref[...], sh_ref[...], sc_ref[...]).astype(BF)
    p1 = _dot(h, w1_ref[...])
    cq = _rms(p1[:, :ql], qn_ref[...]).astype(BF)
    lat = _rms(p1[:, ql:ql + kl], kn_ref[...])
    a = ql + kl
    kr = p1[:, a:a + LANES] * cos_t + p1[:, a + LANES:a + 2 * LANES] * sin_t
    row_ref[:, :kl] = lat
    row_ref[:, kl:] = kr[:, :MLA_ROPE]
    kcat_ref[:, :kl] = lat.astype(BF)
    kcat_ref[:, kl:] = kr.astype(BF)
    q2 = _dot(cq, w2_ref[...])
    hn = nh * LANES
    for hd in range(nh):
        lo = hd * LANES
        qn = q2[:, lo:lo + LANES].astype(BF)
        qcat_ref[hd, :, :kl] = _dot(qn, w3_ref[hd]).astype(BF)
        qr = q2[:, hn + lo:hn + lo + LANES] * cos_t + q2[:, 2 * hn + lo:2 * hn + lo + LANES] * sin_t
        qcat_ref[hd, :, kl:] = qr.astype(BF)


def _mla_pre(x, g, shift, scale, cos_t, sin_t, w1, qnorm, knorm, w2, w3, *, seq_rows):
    m, d = x.shape
    ql = qnorm.shape[1]
    kl = knorm.shape[1]
    nh = w3.shape[0]
    c = kl + LANES
    tm = _row_tile(m, seq_rows, (256, 128))
    ms = _mod_spec(shift, tm, seq_rows)
    ps = _pos_spec(cos_t, tm)
    return pl.pallas_call(
        functools.partial(_mla_pre_kernel, ql=ql, kl=kl, nh=nh),
        grid=(m // tm,),
        in_specs=[
            pl.BlockSpec((tm, d), lambda i: (i, 0)),
            pl.BlockSpec((1, d), lambda i: (0, 0)),
            ms, ms, ps, ps,
            _const_spec(w1), _const_spec(qnorm), _const_spec(knorm), _const_spec(w2), _const_spec(w3),
        ],
        out_specs=[
            pl.BlockSpec((tm, kl + MLA_ROPE), lambda i: (i, 0)),
            pl.BlockSpec((tm, c), lambda i: (i, 0)),
            pl.BlockSpec((nh, tm, c), lambda i: (0, i, 0)),
        ],
        out_shape=[
            jax.ShapeDtypeStruct((m, kl + MLA_ROPE), F32),
            jax.ShapeDtypeStruct((m, c), BF),
            jax.ShapeDtypeStruct((nh, m, c), BF),
        ],
        compiler_params=_params("parallel"),
        name="mla_project",
    )(x, g, shift, scale, cos_t, sin_t, w1, qnorm, knorm, w2, w3)


def _mla_flash_kernel(q_ref, k_ref, wuv_ref, o_ref, m_sc, l_sc, acc_sc, *, tq, tk, kl, nh, scale):
    i = pl.program_id(1)
    j = pl.program_id(2)
    rows = nh * tq
    j_last = ((i + 1) * tq - 1) // tk

    @pl.when(j == 0)
    def _():
        m_sc[...] = jnp.full(m_sc.shape, NEG_INF, F32)
        l_sc[...] = jnp.zeros(l_sc.shape, F32)
        acc_sc[...] = jnp.zeros(acc_sc.shape, F32)

    def step(masked):
        q = q_ref[...].reshape(rows, q_ref.shape[-1])
        k = k_ref[...]
        s = _dot_nt(q, k) * scale
        if masked:
            qpos = i * tq + (lax.broadcasted_iota(jnp.int32, (rows, tk), 0) & (tq - 1))
            kpos = j * tk + lax.broadcasted_iota(jnp.int32, (rows, tk), 1)
            s = jnp.where(kpos <= qpos, s, NEG_INF)
        m_old = m_sc[...]
        m_new = jnp.maximum(m_old, jnp.max(s, axis=-1, keepdims=True))
        alpha = jnp.exp(m_old - m_new)
        p = jnp.exp(s - m_new)
        l_sc[...] = alpha * l_sc[...] + jnp.sum(p, axis=-1, keepdims=True)
        acc_sc[...] = alpha * acc_sc[...] + _dot(p.astype(BF), k[:, :kl])
        m_sc[...] = m_new

    needs_mask = (j + 1) * tk - 1 > i * tq

    @pl.when((j <= j_last) & needs_mask)
    def _():
        step(True)

    @pl.when((j <= j_last) & jnp.logical_not(needs_mask))
    def _():
        step(False)

    @pl.when(j == j_last)
    def _():
        o = (acc_sc[...] * (1.0 / l_sc[...])).astype(BF)
        for hd in range(nh):
            o_ref[:, hd * MLA_V:(hd + 1) * MLA_V] = _dot(o[hd * tq:(hd + 1) * tq], wuv_ref[hd]).astype(BF)


def _mla_flash(qcat, kcat, wuv, *, batch, seq, scale):
    nh, m, c = qcat.shape
    kl = wuv.shape[1]
    tq = _pick(seq, (128,))
    tk = _pick(seq, (512, 256, 128))
    nq = seq // tq
    nk = seq // tk

    def k_map(b, i, j):
        return (b * nk + jnp.minimum(j, ((i + 1) * tq - 1) // tk), 0)

    return pl.pallas_call(
        functools.partial(_mla_flash_kernel, tq=tq, tk=tk, kl=kl, nh=nh, scale=scale),
        grid=(batch, nq, nk),
        in_specs=[
            pl.BlockSpec((nh, tq, c), lambda b, i, j: (0, b * nq + i, 0)),
            pl.BlockSpec((tk, c), k_map),
            _const_spec(wuv),
        ],
        out_specs=pl.BlockSpec((tq, nh * MLA_V), lambda b, i, j: (b * nq + i, 0)),
        out_shape=jax.ShapeDtypeStruct((m, nh * MLA_V), BF),
        scratch_shapes=[
            pltpu.VMEM((nh * tq, 1), F32),
            pltpu.VMEM((nh * tq, 1), F32),
            pltpu.VMEM((nh * tq, kl), F32),
        ],
        compiler_params=_params("parallel", "parallel", "arbitrary"),
        name="mla_prompt_attention",
    )(qcat, kcat, wuv)


def _mla_decode_kernel(pt_ref, q_ref, new_ref, *rest, npg, kl, scale):
    page_refs = rest[:npg]
    o_ref = rest[npg]
    kbuf, m_sc, l_sc, acc_sc = rest[npg + 1:]
    c = pl.program_id(1)

    @pl.when(c == 0)
    def _():
        m_sc[...] = jnp.full(m_sc.shape, NEG_INF, F32)
        l_sc[...] = jnp.zeros(l_sc.shape, F32)
        acc_sc[...] = jnp.zeros(acc_sc.shape, F32)

    for k, ref in enumerate(page_refs):
        kbuf[:, k * PAGE_SIZE:(k + 1) * PAGE_SIZE] = ref[...].astype(BF)

    q = q_ref[...]
    keys_t = kbuf[...]
    s = (_dot(q[:, :kl], keys_t[:kl]) + _dot(q[:, kl:kl + MLA_ROPE], keys_t[kl:])) * scale
    m_old = m_sc[...]
    m_new = jnp.maximum(m_old, jnp.max(s, axis=-1, keepdims=True))
    alpha = jnp.exp(m_old - m_new)
    p = jnp.exp(s - m_new)
    l_sc[...] = alpha * l_sc[...] + jnp.sum(p, axis=-1, keepdims=True)
    acc_sc[...] = alpha * acc_sc[...] + _dot_nt(p.astype(BF), keys_t[:kl])
    m_sc[...] = m_new

    @pl.when(c == pl.num_programs(1) - 1)
    def _():
        new = new_ref[...].astype(F32)
        s_new = jnp.sum(q.astype(F32) * new, axis=-1, keepdims=True) * scale
        m_old = m_sc[...]
        m_new = jnp.maximum(m_old, s_new)
        alpha = jnp.exp(m_old - m_new)
        p_new = jnp.exp(s_new - m_new)
        l_new = alpha * l_sc[...] + p_new
        acc = alpha * acc_sc[...] + p_new.astype(BF).astype(F32) * new[:, :kl]
        o_ref[...] = (acc * (1.0 / l_new)).astype(BF)


def _mla_decode(page_table, qcat, kcat, cache_t, *, layer, scale):
    nb, nh, c = qcat.shape
    lat = cache_t.shape[2]
    kl = lat - MLA_ROPE
    n_pages = page_table.shape[1]
    npg = _pick(n_pages, (32, 16, 8, 4, 2, 1))
    pt = page_table.reshape(-1)

    def page_spec(k):
        return pl.BlockSpec((None, None, lat, PAGE_SIZE),
                            lambda b, ch, pt_ref: (pt_ref[b * n_pages + ch * npg + k], layer, 0, 0))

    grid_spec = pltpu.PrefetchScalarGridSpec(
        num_scalar_prefetch=1,
        grid=(nb, n_pages // npg),
        in_specs=[
            pl.BlockSpec((None, nh, c), lambda b, ch, pt_ref: (b, 0, 0)),
            pl.BlockSpec((None, 1, c), lambda b, ch, pt_ref: (b, 0, 0)),
        ] + [page_spec(k) for k in range(npg)],
        out_specs=pl.BlockSpec((None, nh, kl), lambda b, ch, pt_ref: (b, 0, 0)),
        scratch_shapes=[
            pltpu.VMEM((lat, npg * PAGE_SIZE), BF),
            pltpu.VMEM((nh, 1), F32),
            pltpu.VMEM((nh, 1), F32),
            pltpu.VMEM((nh, kl), F32),
        ],
    )
    return pl.pallas_call(
        functools.partial(_mla_decode_kernel, npg=npg, kl=kl, scale=scale),
        grid_spec=grid_spec,
        out_shape=jax.ShapeDtypeStruct((nb, nh, kl), BF),
        compiler_params=_params("parallel", "arbitrary"),
        name="mla_sample_attention",
    )(pt, qcat, kcat, *([cache_t] * npg))


def _uv_kernel(o_ref, wuv_ref, out_ref):
    nh = o_ref.shape[0]
    for hd in range(nh):
        out_ref[:, hd * MLA_V:(hd + 1) * MLA_V] = _dot(o_ref[hd], wuv_ref[hd]).astype(BF)


def _uv(o_lat, wuv):
    nh, m, kl = o_lat.shape
    return pl.pallas_call(
        _uv_kernel,
        grid=(1,),
        in_specs=[
            pl.BlockSpec(o_lat.shape, lambda i: (0, 0, 0)),
            pl.BlockSpec(wuv.shape, lambda i: (0, 0, 0)),
        ],
        out_specs=pl.BlockSpec((m, nh * MLA_V), lambda i: (0, 0)),
        out_shape=jax.ShapeDtypeStruct((m, nh * MLA_V), BF),
        compiler_params=_params("arbitrary"),
        name="mla_value_up",
    )(o_lat, wuv)


def _swa_prompt_kernel(q_ref, kvp_ref, kvc_ref, bias_ref, mask_ref, sink_ref, o_ref, *, n_kv, scale):
    kvw = n_kv * LANES
    pairs_per_kv = SWA_GROUP // 2
    mask = mask_ref[...] > 0.5
    nkeys = 2 * BLOCK
    for c in range(n_kv):
        def sect(s):
            lo = s * kvw + c * LANES
            return jnp.concatenate([kvp_ref[:, lo:lo + LANES], kvc_ref[:, lo:lo + LANES]], axis=0)
        k2 = jnp.concatenate([sect(0), sect(1)], axis=0)
        v2 = jnp.concatenate([sect(2), sect(3)], axis=0)
        pr0 = c * pairs_per_kv
        qs = jnp.concatenate([q_ref[:, (pr0 + pp) * LANES:(pr0 + pp + 1) * LANES] for pp in range(pairs_per_kv)],
                             axis=0)
        s = _dot_nt(qs, k2) * scale + bias_ref[c]
        s = jnp.where(mask, s, NEG_INF)
        halves = []
        for hf in range(2):
            sh = s[:, hf * nkeys:(hf + 1) * nkeys]
            sink = sink_ref[c, hf]
            m = jnp.maximum(jnp.max(sh, axis=-1, keepdims=True), sink)
            e = jnp.exp(sh - m)
            den = jnp.sum(e, axis=-1, keepdims=True) + jnp.exp(sink - m)
            halves.append((e * (1.0 / den)).astype(BF))
        o = _dot(jnp.concatenate(halves, axis=1), v2).astype(BF)
        for pp in range(pairs_per_kv):
            o_ref[:, (pr0 + pp) * LANES:(pr0 + pp + 1) * LANES] = o[pp * BLOCK:(pp + 1) * BLOCK]


def _swa_prompt(q, kv4, bias_kv, mask2, sink_rows, *, batch, seq, n_kv, scale):
    m, dq = q.shape
    nb = seq // BLOCK
    w4 = kv4.shape[1]
    return pl.pallas_call(
        functools.partial(_swa_prompt_kernel, n_kv=n_kv, scale=scale),
        grid=(batch, nb),
        in_specs=[
            pl.BlockSpec((BLOCK, dq), lambda b, j: (b * nb + j, 0)),
            pl.BlockSpec((BLOCK, w4), lambda b, j: (b * nb + jnp.maximum(j - 1, 0), 0)),
            pl.BlockSpec((BLOCK, w4), lambda b, j: (b * nb + j, 0)),
            _const_spec(bias_kv),
            pl.BlockSpec((None,) + mask2.shape[1:], lambda b, j: (jnp.minimum(j, 1), 0, 0)),
            _const_spec(sink_rows),
        ],
        out_specs=pl.BlockSpec((BLOCK, dq), lambda b, j: (b * nb + j, 0)),
        out_shape=jax.ShapeDtypeStruct((m, dq), BF),
        compiler_params=_params("parallel", "arbitrary"),
        name="swa_prompt_attention",
    )(q, kv4, kv4, bias_kv, mask2, sink_rows)


def _swa_sample_kernel(q_ref, k_ref, v_ref, bias_ref, sink_ref, o_ref, *, n_kv, n_keys, scale):
    for c in range(n_kv):
        q = q_ref[:, c * SWA_GROUP:(c + 1) * SWA_GROUP, :]
        k = k_ref[:, c]
        v = v_ref[:, c]
        s = jnp.einsum("bgd,bkd->bgk", q, k, preferred_element_type=F32) * scale + bias_ref[c][None]
        col = lax.broadcasted_iota(jnp.int32, s.shape, 2)
        s = jnp.where(col < n_keys, s, NEG_INF)
        sink = sink_ref[c][None]
        m = jnp.maximum(jnp.max(s, axis=-1, keepdims=True), sink)
        e = jnp.exp(s - m)
        den = jnp.sum(e, axis=-1, keepdims=True) + jnp.exp(sink - m)
        p = (e * (1.0 / den)).astype(BF)
        o = jnp.einsum("bgk,bkd->bgd", p, v, preferred_element_type=F32)
        o_ref[:, c * SWA_GROUP:(c + 1) * SWA_GROUP, :] = o.astype(BF)


def _swa_sample(q, keys, vals, bias, sinks, *, n_keys, scale):
    nb, nh, dh = q.shape
    n_kv, kp = keys.shape[1], keys.shape[2]
    bb = _pick(nb, (16, 8))
    return pl.pallas_call(
        functools.partial(_swa_sample_kernel, n_kv=n_kv, n_keys=n_keys, scale=scale),
        grid=(nb // bb,),
        in_specs=[
            pl.BlockSpec((bb, nh, dh), lambda i: (i, 0, 0)),
            pl.BlockSpec((bb, n_kv, kp, dh), lambda i: (i, 0, 0, 0)),
            pl.BlockSpec((bb, n_kv, kp, dh), lambda i: (i, 0, 0, 0)),
            pl.BlockSpec(bias.shape, lambda i: (0, 0, 0)),
            pl.BlockSpec(sinks.shape, lambda i: (0, 0, 0)),
        ],
        out_specs=pl.BlockSpec((bb, nh, dh), lambda i: (i, 0, 0)),
        out_shape=jax.ShapeDtypeStruct((nb, nh, dh), BF),
        compiler_params=_params("parallel"),
        name="swa_sample_attention",
    )(q, keys, vals, bias, sinks)


def _final_norm_kernel(x_ref, g_ref, o_ref):
    o_ref[...] = _rms(x_ref[...], g_ref[...])


def _final_norm(x, g):
    m, d = x.shape
    tm = _pick(m, (512, 256, 128))
    return pl.pallas_call(
        _final_norm_kernel,
        grid=(m // tm,),
        in_specs=[pl.BlockSpec((tm, d), lambda i: (i, 0)), pl.BlockSpec((1, d), lambda i: (0, 0))],
        out_specs=pl.BlockSpec((tm, d), lambda i: (i, 0)),
        out_shape=jax.ShapeDtypeStruct((m, d), F32),
        compiler_params=_params("parallel"),
        name="final_norm",
    )(x, g)


def _rope_tables(pos):
    half = MLA_ROPE // 2
    inv = ROPE_THETA ** (-jnp.arange(half, dtype=F32) / half)
    ang = pos.astype(F32)[:, None] * inv
    cos, sin = jnp.cos(ang), jnp.sin(ang)
    pad = jnp.zeros((pos.shape[0], LANES - MLA_ROPE), F32)
    return jnp.concatenate([cos, cos, pad], axis=1), jnp.concatenate([-sin, sin, pad], axis=1)


def _swap_halves(w):
    half = w.shape[-1] // 2
    return jnp.concatenate([w[..., half:], w[..., :half]], axis=-1)


def _pad_lanes(w):
    return jnp.pad(w, [(0, 0)] * (w.ndim - 1) + [(0, LANES - w.shape[-1])])


def _t5_bucket(dist):
    max_exact = NUM_BUCKETS // 2
    n = jnp.maximum(dist, 0)
    nf = jnp.maximum(n, 1).astype(F32)
    large = max_exact + (jnp.log(nf / max_exact) / math.log(MAX_DISTANCE / max_exact)
                         * (NUM_BUCKETS - max_exact)).astype(jnp.int32)
    large = jnp.minimum(large, NUM_BUCKETS - 1)
    return jnp.where(n < max_exact, n, large)


def kernel(x_prompt, x_sample, c_prompt, c_sample, cache_mla, state_swa_k, state_swa_v, page_table, ada_w, ada_b, norm_g, ffn_w_in, ffn_w_out, mla_w_dq, mla_q_norm, mla_w_uq, mla_w_dkv, mla_kv_norm, mla_w_uk, mla_w_uv, mla_w_o, kv_ada_w, kv_ada_b, kv_norm_g, kv_w, swa_w_q, swa_w_o, swa_sinks, rel_bias_table, final_norm_g):
    batch, seq, d = x_prompt.shape
    dec_batch, dec_seq, _ = x_sample.shape
    assert dec_seq == 1
    depth = ada_w.shape[0]
    n_a = mla_w_dq.shape[0]
    n_pages = page_table.shape[1]
    past_len = n_pages * PAGE_SIZE
    ql = mla_w_dq.shape[2]
    kl = mla_kv_norm.shape[1]
    nh = mla_w_uk.shape[2]
    swa_heads = swa_w_q.shape[2] // SWA_HEAD_DIM
    n_kv = swa_heads // SWA_GROUP
    win_s = state_swa_k.shape[1]
    assert win_s == WINDOW and seq % BLOCK == 0
    mla_scale = float((MLA_NOPE + MLA_ROPE) ** -0.5)
    swa_scale = float(SWA_HEAD_DIM ** -0.5)
    mp = batch * seq

    c_all = jnp.concatenate([c_prompt, c_sample], axis=0)
    n_seq = c_all.shape[0]
    c_all = jnp.pad(c_all, ((0, (-n_seq) % 16), (0, 0)))
    mod = _ada(c_all, ada_w, ada_b[:, None, :])
    kvmod = _ada(c_all, kv_ada_w[None], kv_ada_b[None, None, :])[0]

    def mods_p(arr, n):
        return arr[:batch, n * d:(n + 1) * d][:, None, :]

    def mods_s(arr, n):
        return arr[batch:batch + dec_batch, n * d:(n + 1) * d][None]

    cos_p, sin_p = _rope_tables(jnp.arange(seq, dtype=jnp.int32))
    cos_s, sin_s = _rope_tables(jnp.full((1,), past_len, jnp.int32))

    qi = jnp.arange(BLOCK, dtype=jnp.int32)
    km = jnp.arange(2 * BLOCK, dtype=jnp.int32) - BLOCK
    d_p = qi[:, None] - km[None, :]
    win_mask = (d_p >= 0) & (d_p <= WINDOW)
    onehot_p = (_t5_bucket(d_p)[:, :, None] == jnp.arange(NUM_BUCKETS, dtype=jnp.int32)).astype(F32)
    bias_p = jnp.einsum("qkb,bh->hqk", onehot_p, rel_bias_table.astype(F32),
                        precision=lax.Precision.HIGHEST)
    bias_pairs = bias_p.reshape(swa_heads // 2, 2, BLOCK, 2 * BLOCK).transpose(0, 2, 1, 3)
    pairs_per_kv = SWA_GROUP // 2
    bias_kv = bias_pairs.reshape(n_kv, pairs_per_kv * BLOCK, 4 * BLOCK)
    mask_first = win_mask & (km >= 0)[None, :]
    mask2 = jnp.stack([jnp.tile(mask_first, (pairs_per_kv, 2)), jnp.tile(win_mask, (pairs_per_kv, 2))]).astype(F32)
    kp_pad = 2 * WINDOW
    n_keys_s = win_s + 1
    d_s = jnp.concatenate([jnp.arange(win_s, 0, -1, dtype=jnp.int32), jnp.zeros((1,), jnp.int32)])
    bias_s = jnp.transpose(rel_bias_table[_t5_bucket(d_s)], (1, 0)).astype(F32)
    bias_s = jnp.pad(bias_s, ((0, 0), (0, kp_pad - n_keys_s))).reshape(n_kv, SWA_GROUP, kp_pad)

    cache_t = jnp.transpose(cache_mla, (0, 1, 3, 2))

    w_in_bf = ffn_w_in.astype(BF)
    w_out_bf = ffn_w_out.astype(BF)

    x_p = x_prompt.reshape(mp, d)
    x_s = x_sample.reshape(dec_batch, d)
    rows_p, rows_s = [], []
    kv4_p = keys_s = vals_s = keys_s_bf = vals_s_bf = None
    k_p = v_p = None

    for l in range(depth):
        if l == n_a:
            kv_k = kv_w[:, :n_kv * SWA_HEAD_DIM].reshape(d, n_kv, SWA_HEAD_DIM)
            kv_v = kv_w[:, n_kv * SWA_HEAD_DIM:].reshape(d, n_kv, SWA_HEAD_DIM)
            zero = jnp.zeros_like(kv_k)
            lo = lambda w: jnp.concatenate([w, zero], axis=-1).reshape(d, n_kv * LANES)
            hi = lambda w: jnp.concatenate([zero, w], axis=-1).reshape(d, n_kv * LANES)
            nkv = 2 * n_kv * SWA_HEAD_DIM
            w_sh = jnp.concatenate([kv_w, lo(kv_k), hi(kv_k), lo(kv_v), hi(kv_v)], axis=1).astype(BF)
            kvg = kv_norm_g[None]
            kv_f32_p, kv_bf_p = _mod_mm(x_p, kvg, mods_p(kvmod, 0), mods_p(kvmod, 1), w_sh, (F32, BF),
                                        seq_rows=seq)
            k_p = kv_f32_p[:, :nkv // 2].reshape(batch, seq, n_kv, SWA_HEAD_DIM)
            v_p = kv_f32_p[:, nkv // 2:nkv].reshape(batch, seq, n_kv, SWA_HEAD_DIM)
            kv4_p = kv_bf_p[:, nkv:]
            (kv_f32_s,) = _mod_mm(x_s, kvg, mods_s(kvmod, 0), mods_s(kvmod, 1), w_sh[:, :nkv], (F32,),
                                  seq_rows=1)
            k_new = kv_f32_s[:, :nkv // 2].reshape(dec_batch, 1, n_kv, SWA_HEAD_DIM)
            v_new = kv_f32_s[:, nkv // 2:].reshape(dec_batch, 1, n_kv, SWA_HEAD_DIM)
            keys_s = jnp.concatenate([state_swa_k, k_new], axis=1)
            vals_s = jnp.concatenate([state_swa_v, v_new], axis=1)
            padk = lambda t: jnp.pad(jnp.transpose(t, (0, 2, 1, 3)).astype(BF),
                                     ((0, 0), (0, 0), (0, kp_pad - n_keys_s), (0, 0)))
            keys_s_bf, vals_s_bf = padk(keys_s), padk(vals_s)

        g = norm_g[l]
        mp_l = [mods_p(mod[l], n) for n in range(3 * N_SUB)]
        ms_l = [mods_s(mod[l], n) for n in range(3 * N_SUB)]

        x_p = _ffn(x_p, g[0][None], mp_l[0], mp_l[1], mp_l[2], w_in_bf, w_out_bf, l, 0, seq_rows=seq)
        x_s = _ffn(x_s, g[0][None], ms_l[0], ms_l[1], ms_l[2], w_in_bf, w_out_bf, l, 0, seq_rows=1)

        if l < n_a:
            a = l
            dkv = mla_w_dkv[a]
            w1 = jnp.concatenate([mla_w_dq[a], dkv[:, :kl], _pad_lanes(dkv[:, kl:]),
                                  _pad_lanes(_swap_halves(dkv[:, kl:]))], axis=1).astype(BF)
            wuq = mla_w_uq[a].reshape(ql, nh, MLA_NOPE + MLA_ROPE)
            wuq_r = wuq[:, :, MLA_NOPE:]
            w2 = jnp.concatenate([wuq[:, :, :MLA_NOPE].reshape(ql, nh * MLA_NOPE),
                                  _pad_lanes(wuq_r).reshape(ql, nh * LANES),
                                  _pad_lanes(_swap_halves(wuq_r)).reshape(ql, nh * LANES)], axis=1).astype(BF)
            w3 = jnp.transpose(mla_w_uk[a], (1, 2, 0)).astype(BF)
            wuv = jnp.transpose(mla_w_uv[a], (1, 0, 2)).astype(BF)
            w_o = mla_w_o[a].astype(BF)
            qn, kn = mla_q_norm[a][None], mla_kv_norm[a][None]

            row_p, kcat_p, qcat_p = _mla_pre(x_p, g[1][None], mp_l[3], mp_l[4], cos_p, sin_p,
                                             w1, qn, kn, w2, w3, seq_rows=seq)
            oh_p = _mla_flash(qcat_p, kcat_p, wuv, batch=batch, seq=seq, scale=mla_scale)
            x_p = _mm_resid(oh_p, w_o, x_p, mp_l[5], seq_rows=seq)

            row_s, kcat_s, qcat_s = _mla_pre(x_s, g[1][None], ms_l[3], ms_l[4], cos_s, sin_s,
                                             w1, qn, kn, w2, w3, seq_rows=1)
            o_lat_s = _mla_decode(page_table, jnp.transpose(qcat_s, (1, 0, 2)), kcat_s[:, None, :],
                                  cache_t, layer=a, scale=mla_scale)
            oh_s = _uv(jnp.transpose(o_lat_s, (1, 0, 2)), wuv)
            x_s = _mm_resid(oh_s, w_o, x_s, ms_l[5], seq_rows=1)
            rows_p.append(row_p.reshape(batch, seq, kl + MLA_ROPE))
            rows_s.append(row_s.reshape(dec_batch, 1, kl + MLA_ROPE))
        else:
            b = l - n_a
            w_q = swa_w_q[b].astype(BF)
            w_o = swa_w_o[b].astype(BF)
            (q_p,) = _mod_mm(x_p, g[1][None], mp_l[3], mp_l[4], w_q, (BF,), seq_rows=seq)
            sink_rows = jnp.repeat(swa_sinks[b].reshape(n_kv, pairs_per_kv, 2).transpose(0, 2, 1), BLOCK,
                                   axis=2)[..., None]
            o_p = _swa_prompt(q_p, kv4_p, bias_kv, mask2, sink_rows, batch=batch, seq=seq, n_kv=n_kv,
                              scale=swa_scale)
            x_p = _mm_resid(o_p, w_o, x_p, mp_l[5], seq_rows=seq)

            (q_s,) = _mod_mm(x_s, g[1][None], ms_l[3], ms_l[4], w_q, (BF,), seq_rows=1)
            o_s = _swa_sample(q_s.reshape(dec_batch, swa_heads, SWA_HEAD_DIM), keys_s_bf, vals_s_bf, bias_s,
                              swa_sinks[b].reshape(n_kv, SWA_GROUP, 1), n_keys=n_keys_s, scale=swa_scale)
            x_s = _mm_resid(o_s.reshape(dec_batch, swa_heads * SWA_HEAD_DIM), w_o, x_s, ms_l[5], seq_rows=1)

        x_p = _ffn(x_p, g[2][None], mp_l[6], mp_l[7], mp_l[8], w_in_bf, w_out_bf, l, 1, seq_rows=seq)
        x_s = _ffn(x_s, g[2][None], ms_l[6], ms_l[7], ms_l[8], w_in_bf, w_out_bf, l, 1, seq_rows=1)

    y_prompt = _final_norm(x_p, final_norm_g[None]).reshape(batch, seq, d)
    y_sample = _final_norm(x_s, final_norm_g[None]).reshape(dec_batch, 1, d)
    new_mla_prompt = jnp.stack(rows_p, axis=1)
    new_mla_sample = jnp.stack(rows_s, axis=1)
    win_p = min(WINDOW, seq)
    return (y_prompt, y_sample, new_mla_prompt, new_mla_sample,
            k_p[:, seq - win_p:], v_p[:, seq - win_p:],
            keys_s[:, keys_s.shape[1] - win_s:], vals_s[:, vals_s.shape[1] - win_s:])
```

```python
import functools
import math

import jax
import jax.numpy as jnp
import numpy as np
from jax import lax
from jax.experimental import pallas as pl
from jax.experimental.pallas import tpu as pltpu

EPS = 1e-6
NEG_INF = -1e30
ROPE_THETA = 10000.0
PAGE_SIZE = 128
WINDOW = 128
BLOCK = 128
NUM_BUCKETS = 32
MAX_DISTANCE = 128
MLA_NOPE = 128
MLA_ROPE = 64
MLA_V = 128
SWA_HEAD_DIM = 64
SWA_GROUP = 8
N_SUB = 3

LANES = 128
V7X_VMEM_LIMIT_BYTES = 56 * 1024 * 1024

BF = jnp.bfloat16
F32 = jnp.float32


def _params(*sem):
    return pltpu.CompilerParams(dimension_semantics=sem, vmem_limit_bytes=V7X_VMEM_LIMIT_BYTES)


def _pick(n, prefs):
    for p in prefs:
        if p <= n and n % p == 0:
            return p
    return n


def _row_tile(m, seq_rows, prefs):
    return _pick(m if seq_rows == 1 else seq_rows, prefs)


def _const_spec(arr):
    nd = arr.ndim
    return pl.BlockSpec(arr.shape, lambda *_: (0,) * nd, pipeline_mode=pl.Buffered(1))


def _silu(x):
    return x * (1.0 / (1.0 + jnp.exp(-x)))


def _rms(x, g):
    return x * lax.rsqrt(jnp.mean(x * x, axis=-1, keepdims=True) + EPS) * g


def _modulate(x, g, shift, scale):
    return _rms(x, g) * (1.0 + scale) + shift


def _dot(a, b):
    return jnp.dot(a, b, preferred_element_type=F32)


def _dot_nt(a, b):
    return lax.dot_general(a, b, (((1,), (1,)), ((), ())), preferred_element_type=F32)


def _mod_spec(arr, tm, seq_rows):
    _, r, d = arr.shape
    if r == 1:
        return pl.BlockSpec((None, 1, d), lambda i, *_: ((i * tm) // seq_rows, 0, 0))
    return pl.BlockSpec((None, tm, d), lambda i, *_: (0, i, 0))


def _pos_spec(arr, tm):
    p = arr.shape[0]
    if p == 1:
        return pl.BlockSpec((1, LANES), lambda i, *_: (0, 0))
    nblk = p // tm
    return pl.BlockSpec((tm, LANES), lambda i, *_: (i % nblk, 0))


def _ada_kernel(c_ref, w_ref, b_ref, o_ref):
    a = _silu(c_ref[...]).astype(BF)
    o_ref[...] = _dot(a, w_ref[...].astype(BF)) + b_ref[...]


def _ada(c, w, b):
    nl, d, n = w.shape
    m = c.shape[0]
    tn = _pick(n, (1024, 512, 256, 128))
    return pl.pallas_call(
        _ada_kernel,
        grid=(nl, n // tn),
        in_specs=[
            pl.BlockSpec((m, d), lambda l, j: (0, 0)),
            pl.BlockSpec((None, d, tn), lambda l, j: (l, 0, j)),
            pl.BlockSpec((None, 1, tn), lambda l, j: (l, 0, j)),
        ],
        out_specs=pl.BlockSpec((None, m, tn), lambda l, j: (l, 0, j)),
        out_shape=jax.ShapeDtypeStruct((nl, m, n), F32),
        compiler_params=_params("parallel", "parallel"),
        name="ada_params",
    )(c, w, b)


def _ffn_kernel(x_ref, g_ref, sh_ref, sc_ref, wg_ref, wu_ref, wo_ref, xc_ref, gt_ref, o_ref, h_ref, a_ref, *, nf, tf):
    j = pl.program_id(1)

    @pl.when(j == 0)
    def _():
        h_ref[...] = _modulate(x_ref[...], g_ref[...], sh_ref[...], sc_ref[...]).astype(BF)

    @pl.when(j < nf)
    def _():
        h = h_ref[...]
        gate = _dot(h, wg_ref[...])
        up = _dot(h, wu_ref[...])
        a_ref[j] = (_silu(gate) * up).astype(BF)

    @pl.when(j >= nf)
    def _():
        y = _dot(a_ref[0], wo_ref[0:tf, :])
        for c in range(1, nf):
            y = y + _dot(a_ref[c], wo_ref[c * tf:(c + 1) * tf, :])
        o_ref[...] = xc_ref[...] + 0.5 * gt_ref[...] * y


def _ffn(x, g, shift, scale, gate, w_in, w_out, layer, half, *, seq_rows):
    m, d = x.shape
    f = w_out.shape[2]
    tm = _row_tile(m, seq_rows, (1024, 512, 256, 128))
    tf = _pick(f, (512, 256, 128))
    tn = _pick(d, (256, 128))
    nf = f // tf
    ms = _mod_spec(shift, tm, seq_rows)
    col = lambda j: jnp.maximum(j - nf, 0)
    if gate.shape[1] == 1:
        gate_spec = pl.BlockSpec((None, 1, tn), lambda i, j: ((i * tm) // seq_rows, 0, col(j)))
    else:
        gate_spec = pl.BlockSpec((None, tm, tn), lambda i, j: (0, i, col(j)))
    return pl.pallas_call(
        functools.partial(_ffn_kernel, nf=nf, tf=tf),
        grid=(m // tm, nf + d // tn),
        in_specs=[
            pl.BlockSpec((tm, d), lambda i, j: (i, 0), pipeline_mode=pl.Buffered(1)),
            pl.BlockSpec((1, d), lambda i, j: (0, 0)),
            ms, ms,
            pl.BlockSpec((None, None, d, tf), lambda i, j: (layer, half, 0, jnp.minimum(j, nf - 1))),
            pl.BlockSpec((None, None, d, tf), lambda i, j: (layer, half, 0, jnp.minimum(j, nf - 1) + nf)),
            pl.BlockSpec((None, None, f, tn), lambda i, j: (layer, half, 0, col(j))),
            pl.BlockSpec((tm, tn), lambda i, j: (i, col(j))),
            gate_spec,
        ],
        out_specs=pl.BlockSpec((tm, tn), lambda i, j: (i, col(j))),
        out_shape=jax.ShapeDtypeStruct((m, d), F32),
        scratch_shapes=[pltpu.VMEM((tm, d), BF), pltpu.VMEM((nf, tm, tf), BF)],
        compiler_params=_params("parallel", "arbitrary"),
        name="ffn_half",
    )(x, g, shift, scale, w_in, w_in, w_out, x, gate)


def _mod_mm_kernel(x_ref, g_ref, sh_ref, sc_ref, w_ref, *o_refs):
    h = _modulate(x_ref[...], g_ref[...], sh_ref[...], sc_ref[...]).astype(BF)
    y = _dot(h, w_ref[...])
    for o_ref in o_refs:
        o_ref[...] = y.astype(o_ref.dtype)


def _mod_mm(x, g, shift, scale, w, out_dtypes, *, seq_rows):
    m, d = x.shape
    n = w.shape[1]
    tm = _row_tile(m, seq_rows, (512, 256, 128))
    ms = _mod_spec(shift, tm, seq_rows)
    outs = pl.pallas_call(
        _mod_mm_kernel,
        grid=(m // tm,),
        in_specs=[
            pl.BlockSpec((tm, d), lambda i: (i, 0)),
            pl.BlockSpec((1, d), lambda i: (0, 0)),
            ms, ms,
            _const_spec(w),
        ],
        out_specs=[pl.BlockSpec((tm, n), lambda i: (i, 0)) for _ in out_dtypes],
        out_shape=[jax.ShapeDtypeStruct((m, n), dt) for dt in out_dtypes],
        compiler_params=_params("parallel"),
        name="mod_matmul",
    )(x, g, shift, scale, w)
    return outs


def _mm_resid_kernel(a_ref, w_ref, x_ref, gt_ref, o_ref, *, a_transposed):
    if a_transposed:
        y = lax.dot_general(a_ref[...], w_ref[...], (((0,), (0,)), ((), ())), preferred_element_type=F32)
    else:
        y = _dot(a_ref[...], w_ref[...])
    o_ref[...] = x_ref[...] + gt_ref[...] * y


def _mm_resid(a, w, x, gate, *, seq_rows, a_transposed=False):
    m, n = x.shape
    k = w.shape[0]
    tm = _row_tile(m, seq_rows, (512, 256, 128))
    return pl.pallas_call(
        functools.partial(_mm_resid_kernel, a_transposed=a_transposed),
        grid=(m // tm,),
        in_specs=[
            pl.BlockSpec((k, tm), lambda i: (0, i)) if a_transposed else pl.BlockSpec((tm, k), lambda i: (i, 0)),
            _const_spec(w),
            pl.BlockSpec((tm, n), lambda i: (i, 0)),
            _mod_spec(gate, tm, seq_rows),
        ],
        out_specs=pl.BlockSpec((tm, n), lambda i: (i, 0)),
        out_shape=jax.ShapeDtypeStruct((m, n), F32),
        compiler_params=_params("parallel"),
        name="matmul_residual",
    )(a, w, x, gate)


def _mla_pre_kernel(x_ref, g_ref, sh_ref, sc_ref, cos_ref, sin_ref, cost_ref, sint_ref, w1_ref, qn_ref, kn_ref,
                    w2t_ref, w3_ref, row_ref, kcat_ref, latt_ref, qt_ref, *, ql, kl, nh):
    tm = x_ref.shape[0]
    cos_t = cos_ref[...]
    sin_t = sin_ref[...]
    h = _modulate(x_ref[...], g_ref[...], sh_ref[...], sc_ref[...]).astype(BF)
    p1 = _dot(h, w1_ref[...])
    cq = _rms(p1[:, :ql], qn_ref[...])
    lat = _rms(p1[:, ql:ql + kl], kn_ref[...])
    a = ql + kl
    kr = p1[:, a:a + LANES] * cos_t + p1[:, a + LANES:a + 2 * LANES] * sin_t
    row_ref[:, :kl] = lat
    row_ref[:, kl:] = kr[:, :MLA_ROPE]
    kcat_ref[:, :kl] = lat.astype(BF)
    kcat_ref[:, kl:] = kr.astype(BF)
    latt_ref[...] = lat.T.astype(BF)
    q2t = _dot(w2t_ref[...], cq.T.astype(BF))
    cos_c = cost_ref[...]
    sin_c = sint_ref[...]
    hn = nh * LANES
    for hd in range(nh):
        lo = hd * LANES
        qn_t = q2t[lo:lo + LANES].astype(BF)
        qt_ref[:kl, hd * tm:(hd + 1) * tm] = _dot(w3_ref[hd], qn_t).astype(BF)
        qr_t = q2t[hn + lo:hn + lo + LANES] * cos_c + q2t[2 * hn + lo:2 * hn + lo + LANES] * sin_c
        qt_ref[kl:, hd * tm:(hd + 1) * tm] = qr_t.astype(BF)


def _mla_pre(x, g, shift, scale, cos_t, sin_t, w1, qnorm, knorm, w2t, w3, *, seq_rows):
    m, d = x.shape
    ql = qnorm.shape[1]
    kl = knorm.shape[1]
    nh = w3.shape[0]
    c = kl + LANES
    tm = _row_tile(m, seq_rows, (256, 128))
    ms = _mod_spec(shift, tm, seq_rows)
    ps = _pos_spec(cos_t, tm)
    cos_c, sin_c = cos_t.T, sin_t.T
    npos = cos_t.shape[0]
    if npos == 1:
        pst = pl.BlockSpec((LANES, 1), lambda i: (0, 0))
    else:
        pst = pl.BlockSpec((LANES, tm), lambda i: (0, i % (npos // tm)))
    return pl.pallas_call(
        functools.partial(_mla_pre_kernel, ql=ql, kl=kl, nh=nh),
        grid=(m // tm,),
        in_specs=[
            pl.BlockSpec((tm, d), lambda i: (i, 0)),
            pl.BlockSpec((1, d), lambda i: (0, 0)),
            ms, ms, ps, ps, pst, pst,
            _const_spec(w1), _const_spec(qnorm), _const_spec(knorm), _const_spec(w2t), _const_spec(w3),
        ],
        out_specs=[
            pl.BlockSpec((tm, kl + MLA_ROPE), lambda i: (i, 0)),
            pl.BlockSpec((tm, c), lambda i: (i, 0)),
            pl.BlockSpec((kl, tm), lambda i: (0, i)),
            pl.BlockSpec((None, c, nh * tm), lambda i: (i, 0, 0)),
        ],
        out_shape=[
            jax.ShapeDtypeStruct((m, kl + MLA_ROPE), F32),
            jax.ShapeDtypeStruct((m, c), BF),
            jax.ShapeDtypeStruct((kl, m), BF),
            jax.ShapeDtypeStruct((m // tm, c, nh * tm), BF),
        ],
        compiler_params=_params("parallel"),
        name="mla_project",
    )(x, g, shift, scale, cos_t, sin_t, cos_c, sin_c, w1, qnorm, knorm, w2t, w3)


def _mla_flash_kernel(it_ref, jt_ref, qt_ref, k_ref, vt_ref, wuvt_ref, o_ref, m_sc, l_sc, acc_sc, *,
                      tq, tk, nh, exp2_scale):
    t = pl.program_id(1)
    i = it_ref[t]
    j = jt_ref[t]
    cols = nh * tq
    j_last = ((i + 1) * tq - 1) // tk

    @pl.when(j == 0)
    def _():
        m_sc[...] = jnp.full(m_sc.shape, NEG_INF, F32)
        l_sc[...] = jnp.zeros(l_sc.shape, F32)
        acc_sc[...] = jnp.zeros(acc_sc.shape, F32)

    def step(masked):
        s = _dot(k_ref[...], qt_ref[...])
        if masked:
            kpos = j * tk + lax.broadcasted_iota(jnp.int32, (tk, cols), 0)
            qpos = i * tq + (lax.broadcasted_iota(jnp.int32, (tk, cols), 1) & (tq - 1))
            s = jnp.where(kpos <= qpos, s, NEG_INF)
        m_old = m_sc[...]
        m_new = jnp.maximum(m_old, jnp.max(s, axis=0, keepdims=True))
        alpha = jnp.exp2((m_old - m_new) * exp2_scale)
        p = jnp.exp2((s - m_new) * exp2_scale)
        l_sc[...] = alpha * l_sc[...] + jnp.sum(p, axis=0, keepdims=True)
        acc_sc[...] = alpha * acc_sc[...] + _dot(vt_ref[...], p.astype(BF))
        m_sc[...] = m_new

    needs_mask = (j + 1) * tk - 1 > i * tq

    @pl.when(needs_mask)
    def _():
        step(True)

    @pl.when(jnp.logical_not(needs_mask))
    def _():
        step(False)

    @pl.when(j == j_last)
    def _():
        o_t = (acc_sc[...] * (1.0 / l_sc[...])).astype(BF)
        for hd in range(nh):
            o_ref[hd * MLA_V:(hd + 1) * MLA_V, :] = _dot(wuvt_ref[hd], o_t[:, hd * tq:(hd + 1) * tq]).astype(BF)


def _mla_flash(qt, kcat, latt, wuvt, *, batch, seq, scale):
    nblk, c, cols = qt.shape
    nh, _, kl = wuvt.shape
    tq = cols // nh
    m = nblk * tq
    tk = _pick(seq, (512, 256, 128))
    nq = seq // tq
    nk = seq // tk
    sched = [(i, j) for i in range(nq) for j in range(((i + 1) * tq - 1) // tk + 1)]
    it = jnp.asarray(np.array([s[0] for s in sched], np.int32))
    jt = jnp.asarray(np.array([s[1] for s in sched], np.int32))
    grid_spec = pltpu.PrefetchScalarGridSpec(
        num_scalar_prefetch=2,
        grid=(batch, len(sched)),
        in_specs=[
            pl.BlockSpec((None, c, cols), lambda b, t, it_ref, jt_ref: (b * nq + it_ref[t], 0, 0)),
            pl.BlockSpec((tk, c), lambda b, t, it_ref, jt_ref: (b * nk + jt_ref[t], 0)),
            pl.BlockSpec((kl, tk), lambda b, t, it_ref, jt_ref: (0, b * nk + jt_ref[t])),
            pl.BlockSpec(wuvt.shape, lambda b, t, it_ref, jt_ref: (0, 0, 0), pipeline_mode=pl.Buffered(1)),
        ],
        out_specs=pl.BlockSpec((nh * MLA_V, tq), lambda b, t, it_ref, jt_ref: (0, b * nq + it_ref[t])),
        scratch_shapes=[
            pltpu.VMEM((1, cols), F32),
            pltpu.VMEM((1, cols), F32),
            pltpu.VMEM((kl, cols), F32),
        ],
    )
    return pl.pallas_call(
        functools.partial(_mla_flash_kernel, tq=tq, tk=tk, nh=nh, exp2_scale=scale * math.log2(math.e)),
        grid_spec=grid_spec,
        out_shape=jax.ShapeDtypeStruct((nh * MLA_V, m), BF),
        compiler_params=_params("parallel", "arbitrary"),
        name="mla_prompt_attention",
    )(it, jt, qt, kcat, latt, wuvt)


def _mla_decode_kernel(pt_ref, q_ref, new_ref, *rest, npg, kl, scale):
    page_refs = rest[:npg]
    o_ref = rest[npg]
    kbuf, m_sc, l_sc, acc_sc = rest[npg + 1:]
    c = pl.program_id(1)

    @pl.when(c == 0)
    def _():
        m_sc[...] = jnp.full(m_sc.shape, NEG_INF, F32)
        l_sc[...] = jnp.zeros(l_sc.shape, F32)
        acc_sc[...] = jnp.zeros(acc_sc.shape, F32)

    for k, ref in enumerate(page_refs):
        kbuf[:, k * PAGE_SIZE:(k + 1) * PAGE_SIZE] = ref[...].astype(BF)

    q = q_ref[...]
    keys_t = kbuf[...]
    s = (_dot(q[:, :kl], keys_t[:kl]) + _dot(q[:, kl:kl + MLA_ROPE], keys_t[kl:])) * scale
    m_old = m_sc[...]
    m_new = jnp.maximum(m_old, jnp.max(s, axis=-1, keepdims=True))
    alpha = jnp.exp(m_old - m_new)
    p = jnp.exp(s - m_new)
    l_sc[...] = alpha * l_sc[...] + jnp.sum(p, axis=-1, keepdims=True)
    acc_sc[...] = alpha * acc_sc[...] + _dot_nt(p.astype(BF), keys_t[:kl])
    m_sc[...] = m_new

    @pl.when(c == pl.num_programs(1) - 1)
    def _():
        new = new_ref[...].astype(F32)
        s_new = jnp.sum(q.astype(F32) * new, axis=-1, keepdims=True) * scale
        m_old = m_sc[...]
        m_new = jnp.maximum(m_old, s_new)
        alpha = jnp.exp(m_old - m_new)
        p_new = jnp.exp(s_new - m_new)
        l_new = alpha * l_sc[...] + p_new
        acc = alpha * acc_sc[...] + p_new.astype(BF).astype(F32) * new[:, :kl]
        o_ref[...] = (acc * (1.0 / l_new)).astype(BF)


def _mla_decode(page_table, qcat, kcat, cache_t, *, layer, scale):
    nb, nh, c = qcat.shape
    lat = cache_t.shape[2]
    kl = lat - MLA_ROPE
    n_pages = page_table.shape[1]
    npg = _pick(n_pages, (32, 16, 8, 4, 2, 1))
    pt = page_table.reshape(-1)

    def page_spec(k):
        return pl.BlockSpec((None, None, lat, PAGE_SIZE),
                            lambda b, ch, pt_ref: (pt_ref[b * n_pages + ch * npg + k], layer, 0, 0))

    grid_spec = pltpu.PrefetchScalarGridSpec(
        num_scalar_prefetch=1,
        grid=(nb, n_pages // npg),
        in_specs=[
            pl.BlockSpec((None, nh, c), lambda b, ch, pt_ref: (b, 0, 0)),
            pl.BlockSpec((None, 1, c), lambda b, ch, pt_ref: (b, 0, 0)),
        ] + [page_spec(k) for k in range(npg)],
        out_specs=pl.BlockSpec((None, nh, kl), lambda b, ch, pt_ref: (b, 0, 0)),
        scratch_shapes=[
            pltpu.VMEM((lat, npg * PAGE_SIZE), BF),
            pltpu.VMEM((nh, 1), F32),
            pltpu.VMEM((nh, 1), F32),
            pltpu.VMEM((nh, kl), F32),
        ],
    )
    return pl.pallas_call(
        functools.partial(_mla_decode_kernel, npg=npg, kl=kl, scale=scale),
        grid_spec=grid_spec,
        out_shape=jax.ShapeDtypeStruct((nb, nh, kl), BF),
        compiler_params=_params("parallel", "arbitrary"),
        name="mla_sample_attention",
    )(pt, qcat, kcat, *([cache_t] * npg))


def _uv_kernel(o_ref, wuv_ref, out_ref):
    nh = o_ref.shape[0]
    for hd in range(nh):
        out_ref[:, hd * MLA_V:(hd + 1) * MLA_V] = _dot(o_ref[hd], wuv_ref[hd]).astype(BF)


def _uv(o_lat, wuv):
    nh, m, kl = o_lat.shape
    return pl.pallas_call(
        _uv_kernel,
        grid=(1,),
        in_specs=[
            pl.BlockSpec(o_lat.shape, lambda i: (0, 0, 0)),
            pl.BlockSpec(wuv.shape, lambda i: (0, 0, 0)),
        ],
        out_specs=pl.BlockSpec((m, nh * MLA_V), lambda i: (0, 0)),
        out_shape=jax.ShapeDtypeStruct((m, nh * MLA_V), BF),
        compiler_params=_params("arbitrary"),
        name="mla_value_up",
    )(o_lat, wuv)


def _swa_prompt_kernel(q_ref, kvp_ref, kvc_ref, bias_ref, mask_ref, sink_ref, o_ref, *, n_kv, scale):
    kvw = n_kv * LANES
    pairs_per_kv = SWA_GROUP // 2
    mask = mask_ref[...] > 0.5
    nkeys = 2 * BLOCK
    for c in range(n_kv):
        def sect(s):
            lo = s * kvw + c * LANES
            return jnp.concatenate([kvp_ref[:, lo:lo + LANES], kvc_ref[:, lo:lo + LANES]], axis=0)
        k2 = jnp.concatenate([sect(0), sect(1)], axis=0)
        v2 = jnp.concatenate([sect(2), sect(3)], axis=0)
        pr0 = c * pairs_per_kv
        qs = jnp.concatenate([q_ref[:, (pr0 + pp) * LANES:(pr0 + pp + 1) * LANES] for pp in range(pairs_per_kv)],
                             axis=0)
        s = _dot_nt(qs, k2) * scale + bias_ref[c]
        s = jnp.where(mask, s, NEG_INF)
        halves = []
        for hf in range(2):
            sh = s[:, hf * nkeys:(hf + 1) * nkeys]
            sink = sink_ref[c, hf]
            m = jnp.maximum(jnp.max(sh, axis=-1, keepdims=True), sink)
            e = jnp.exp(sh - m)
            den = jnp.sum(e, axis=-1, keepdims=True) + jnp.exp(sink - m)
            halves.append((e * (1.0 / den)).astype(BF))
        o = _dot(jnp.concatenate(halves, axis=1), v2).astype(BF)
        for pp in range(pairs_per_kv):
            o_ref[:, (pr0 + pp) * LANES:(pr0 + pp + 1) * LANES] = o[pp * BLOCK:(pp + 1) * BLOCK]


def _swa_prompt(q, kv4, bias_kv, mask2, sink_rows, *, batch, seq, n_kv, scale):
    m, dq = q.shape
    nb = seq // BLOCK
    w4 = kv4.shape[1]
    return pl.pallas_call(
        functools.partial(_swa_prompt_kernel, n_kv=n_kv, scale=scale),
        grid=(batch, nb),
        in_specs=[
            pl.BlockSpec((BLOCK, dq), lambda b, j: (b * nb + j, 0)),
            pl.BlockSpec((BLOCK, w4), lambda b, j: (b * nb + jnp.maximum(j - 1, 0), 0)),
            pl.BlockSpec((BLOCK, w4), lambda b, j: (b * nb + j, 0)),
            _const_spec(bias_kv),
            pl.BlockSpec((None,) + mask2.shape[1:], lambda b, j: (jnp.minimum(j, 1), 0, 0)),
            _const_spec(sink_rows),
        ],
        out_specs=pl.BlockSpec((BLOCK, dq), lambda b, j: (b * nb + j, 0)),
        out_shape=jax.ShapeDtypeStruct((m, dq), BF),
        compiler_params=_params("parallel", "arbitrary"),
        name="swa_prompt_attention",
    )(q, kv4, kv4, bias_kv, mask2, sink_rows)


def _swa_sample_kernel(q_ref, k_ref, v_ref, bias_ref, sink_ref, o_ref, *, n_kv, n_keys, scale):
    for c in range(n_kv):
        q = q_ref[:, c * SWA_GROUP:(c + 1) * SWA_GROUP, :]
        k = k_ref[:, c]
        v = v_ref[:, c]
        s = jnp.einsum("bgd,bkd->bgk", q, k, preferred_element_type=F32) * scale + bias_ref[c][None]
        col = lax.broadcasted_iota(jnp.int32, s.shape, 2)
        s = jnp.where(col < n_keys, s, NEG_INF)
        sink = sink_ref[c][None]
        m = jnp.maximum(jnp.max(s, axis=-1, keepdims=True), sink)
        e = jnp.exp(s - m)
        den = jnp.sum(e, axis=-1, keepdims=True) + jnp.exp(sink - m)
        p = (e * (1.0 / den)).astype(BF)
        o = jnp.einsum("bgk,bkd->bgd", p, v, preferred_element_type=F32)
        o_ref[:, c * SWA_GROUP:(c + 1) * SWA_GROUP, :] = o.astype(BF)


def _swa_sample(q, keys, vals, bias, sinks, *, n_keys, scale):
    nb, nh, dh = q.shape
    n_kv, kp = keys.shape[1], keys.shape[2]
    bb = _pick(nb, (16, 8))
    return pl.pallas_call(
        functools.partial(_swa_sample_kernel, n_kv=n_kv, n_keys=n_keys, scale=scale),
        grid=(nb // bb,),
        in_specs=[
            pl.BlockSpec((bb, nh, dh), lambda i: (i, 0, 0)),
            pl.BlockSpec((bb, n_kv, kp, dh), lambda i: (i, 0, 0, 0)),
            pl.BlockSpec((bb, n_kv, kp, dh), lambda i: (i, 0, 0, 0)),
            pl.BlockSpec(bias.shape, lambda i: (0, 0, 0)),
            pl.BlockSpec(sinks.shape, lambda i: (0, 0, 0)),
        ],
        out_specs=pl.BlockSpec((bb, nh, dh), lambda i: (i, 0, 0)),
        out_shape=jax.ShapeDtypeStruct((nb, nh, dh), BF),
        compiler_params=_params("parallel"),
        name="swa_sample_attention",
    )(q, keys, vals, bias, sinks)


def _final_norm_kernel(x_ref, g_ref, o_ref):
    o_ref[...] = _rms(x_ref[...], g_ref[...])


def _final_norm(x, g):
    m, d = x.shape
    tm = _pick(m, (512, 256, 128))
    return pl.pallas_call(
        _final_norm_kernel,
        grid=(m // tm,),
        in_specs=[pl.BlockSpec((tm, d), lambda i: (i, 0)), pl.BlockSpec((1, d), lambda i: (0, 0))],
        out_specs=pl.BlockSpec((tm, d), lambda i: (i, 0)),
        out_shape=jax.ShapeDtypeStruct((m, d), F32),
        compiler_params=_params("parallel"),
        name="final_norm",
    )(x, g)


def _rope_tables(pos):
    half = MLA_ROPE // 2
    inv = ROPE_THETA ** (-jnp.arange(half, dtype=F32) / half)
    ang = pos.astype(F32)[:, None] * inv
    cos, sin = jnp.cos(ang), jnp.sin(ang)
    pad = jnp.zeros((pos.shape[0], LANES - MLA_ROPE), F32)
    return jnp.concatenate([cos, cos, pad], axis=1), jnp.concatenate([-sin, sin, pad], axis=1)


def _swap_halves(w):
    half = w.shape[-1] // 2
    return jnp.concatenate([w[..., half:], w[..., :half]], axis=-1)


def _pad_lanes(w):
    return jnp.pad(w, [(0, 0)] * (w.ndim - 1) + [(0, LANES - w.shape[-1])])


def _t5_bucket(dist):
    max_exact = NUM_BUCKETS // 2
    n = jnp.maximum(dist, 0)
    nf = jnp.maximum(n, 1).astype(F32)
    large = max_exact + (jnp.log(nf / max_exact) / math.log(MAX_DISTANCE / max_exact)
                         * (NUM_BUCKETS - max_exact)).astype(jnp.int32)
    large = jnp.minimum(large, NUM_BUCKETS - 1)
    return jnp.where(n < max_exact, n, large)


def kernel(x_prompt, x_sample, c_prompt, c_sample, cache_mla, state_swa_k, state_swa_v, page_table, ada_w, ada_b, norm_g, ffn_w_in, ffn_w_out, mla_w_dq, mla_q_norm, mla_w_uq, mla_w_dkv, mla_kv_norm, mla_w_uk, mla_w_uv, mla_w_o, kv_ada_w, kv_ada_b, kv_norm_g, kv_w, swa_w_q, swa_w_o, swa_sinks, rel_bias_table, final_norm_g):
    batch, seq, d = x_prompt.shape
    dec_batch, dec_seq, _ = x_sample.shape
    assert dec_seq == 1
    depth = ada_w.shape[0]
    n_a = mla_w_dq.shape[0]
    n_pages = page_table.shape[1]
    past_len = n_pages * PAGE_SIZE
    ql = mla_w_dq.shape[2]
    kl = mla_kv_norm.shape[1]
    nh = mla_w_uk.shape[2]
    swa_heads = swa_w_q.shape[2] // SWA_HEAD_DIM
    n_kv = swa_heads // SWA_GROUP
    win_s = state_swa_k.shape[1]
    assert win_s == WINDOW and seq % BLOCK == 0
    mla_scale = float((MLA_NOPE + MLA_ROPE) ** -0.5)
    swa_scale = float(SWA_HEAD_DIM ** -0.5)
    mp = batch * seq

    c_all = jnp.concatenate([c_prompt, c_sample], axis=0)
    n_seq = c_all.shape[0]
    c_all = jnp.pad(c_all, ((0, (-n_seq) % 16), (0, 0)))
    mod = _ada(c_all, ada_w, ada_b[:, None, :])
    kvmod = _ada(c_all, kv_ada_w[None], kv_ada_b[None, None, :])[0]

    def mods_p(arr, n):
        return arr[:batch, n * d:(n + 1) * d][:, None, :]

    def mods_s(arr, n):
        return arr[batch:batch + dec_batch, n * d:(n + 1) * d][None]

    cos_p, sin_p = _rope_tables(jnp.arange(seq, dtype=jnp.int32))
    cos_s, sin_s = _rope_tables(jnp.full((1,), past_len, jnp.int32))

    qi = jnp.arange(BLOCK, dtype=jnp.int32)
    km = jnp.arange(2 * BLOCK, dtype=jnp.int32) - BLOCK
    d_p = qi[:, None] - km[None, :]
    win_mask = (d_p >= 0) & (d_p <= WINDOW)
    onehot_p = (_t5_bucket(d_p)[:, :, None] == jnp.arange(NUM_BUCKETS, dtype=jnp.int32)).astype(F32)
    bias_p = jnp.einsum("qkb,bh->hqk", onehot_p, rel_bias_table.astype(F32),
                        precision=lax.Precision.HIGHEST)
    bias_pairs = bias_p.reshape(swa_heads // 2, 2, BLOCK, 2 * BLOCK).transpose(0, 2, 1, 3)
    pairs_per_kv = SWA_GROUP // 2
    bias_kv = bias_pairs.reshape(n_kv, pairs_per_kv * BLOCK, 4 * BLOCK)
    mask_first = win_mask & (km >= 0)[None, :]
    mask2 = jnp.stack([jnp.tile(mask_first, (pairs_per_kv, 2)), jnp.tile(win_mask, (pairs_per_kv, 2))]).astype(F32)
    kp_pad = 2 * WINDOW
    n_keys_s = win_s + 1
    d_s = jnp.concatenate([jnp.arange(win_s, 0, -1, dtype=jnp.int32), jnp.zeros((1,), jnp.int32)])
    bias_s = jnp.transpose(rel_bias_table[_t5_bucket(d_s)], (1, 0)).astype(F32)
    bias_s = jnp.pad(bias_s, ((0, 0), (0, kp_pad - n_keys_s))).reshape(n_kv, SWA_GROUP, kp_pad)

    cache_t = jnp.transpose(cache_mla, (0, 1, 3, 2))

    w_in_bf = ffn_w_in.astype(BF)
    w_out_bf = ffn_w_out.astype(BF)

    x_p = x_prompt.reshape(mp, d)
    x_s = x_sample.reshape(dec_batch, d)
    rows_p, rows_s = [], []
    kv4_p = keys_s = vals_s = keys_s_bf = vals_s_bf = None
    k_p = v_p = None

    for l in range(depth):
        if l == n_a:
            kv_k = kv_w[:, :n_kv * SWA_HEAD_DIM].reshape(d, n_kv, SWA_HEAD_DIM)
            kv_v = kv_w[:, n_kv * SWA_HEAD_DIM:].reshape(d, n_kv, SWA_HEAD_DIM)
            zero = jnp.zeros_like(kv_k)
            lo = lambda w: jnp.concatenate([w, zero], axis=-1).reshape(d, n_kv * LANES)
            hi = lambda w: jnp.concatenate([zero, w], axis=-1).reshape(d, n_kv * LANES)
            nkv = 2 * n_kv * SWA_HEAD_DIM
            w_sh = jnp.concatenate([kv_w, lo(kv_k), hi(kv_k), lo(kv_v), hi(kv_v)], axis=1).astype(BF)
            kvg = kv_norm_g[None]
            kv_f32_p, kv_bf_p = _mod_mm(x_p, kvg, mods_p(kvmod, 0), mods_p(kvmod, 1), w_sh, (F32, BF),
                                        seq_rows=seq)
            k_p = kv_f32_p[:, :nkv // 2].reshape(batch, seq, n_kv, SWA_HEAD_DIM)
            v_p = kv_f32_p[:, nkv // 2:nkv].reshape(batch, seq, n_kv, SWA_HEAD_DIM)
            kv4_p = kv_bf_p[:, nkv:]
            (kv_f32_s,) = _mod_mm(x_s, kvg, mods_s(kvmod, 0), mods_s(kvmod, 1), w_sh[:, :nkv], (F32,),
                                  seq_rows=1)
            k_new = kv_f32_s[:, :nkv // 2].reshape(dec_batch, 1, n_kv, SWA_HEAD_DIM)
            v_new = kv_f32_s[:, nkv // 2:].reshape(dec_batch, 1, n_kv, SWA_HEAD_DIM)
            keys_s = jnp.concatenate([state_swa_k, k_new], axis=1)
            vals_s = jnp.concatenate([state_swa_v, v_new], axis=1)
            padk = lambda t: jnp.pad(jnp.transpose(t, (0, 2, 1, 3)).astype(BF),
                                     ((0, 0), (0, 0), (0, kp_pad - n_keys_s), (0, 0)))
            keys_s_bf, vals_s_bf = padk(keys_s), padk(vals_s)

        g = norm_g[l]
        mp_l = [mods_p(mod[l], n) for n in range(3 * N_SUB)]
        ms_l = [mods_s(mod[l], n) for n in range(3 * N_SUB)]

        x_p = _ffn(x_p, g[0][None], mp_l[0], mp_l[1], mp_l[2], w_in_bf, w_out_bf, l, 0, seq_rows=seq)
        x_s = _ffn(x_s, g[0][None], ms_l[0], ms_l[1], ms_l[2], w_in_bf, w_out_bf, l, 0, seq_rows=1)

        if l < n_a:
            a = l
            dkv = mla_w_dkv[a]
            w1 = jnp.concatenate([mla_w_dq[a], dkv[:, :kl], _pad_lanes(dkv[:, kl:]),
                                  _pad_lanes(_swap_halves(dkv[:, kl:]))], axis=1).astype(BF)
            wuq = mla_w_uq[a].reshape(ql, nh, MLA_NOPE + MLA_ROPE)
            wuq_r = wuq[:, :, MLA_NOPE:]
            w2t = jnp.concatenate([wuq[:, :, :MLA_NOPE].reshape(ql, nh * MLA_NOPE),
                                   _pad_lanes(wuq_r).reshape(ql, nh * LANES),
                                   _pad_lanes(_swap_halves(wuq_r)).reshape(ql, nh * LANES)], axis=1).T.astype(BF)
            w3 = jnp.transpose(mla_w_uk[a], (1, 0, 2)).astype(BF)
            wuv = jnp.transpose(mla_w_uv[a], (1, 0, 2)).astype(BF)
            wuvt = jnp.transpose(mla_w_uv[a], (1, 2, 0)).astype(BF)
            w_o = mla_w_o[a].astype(BF)
            qn, kn = mla_q_norm[a][None], mla_kv_norm[a][None]

            row_p, kcat_p, latt_p, qt_p = _mla_pre(x_p, g[1][None], mp_l[3], mp_l[4], cos_p, sin_p,
                                                   w1, qn, kn, w2t, w3, seq_rows=seq)
            oh_t_p = _mla_flash(qt_p, kcat_p, latt_p, wuvt, batch=batch, seq=seq, scale=mla_scale)
            x_p = _mm_resid(oh_t_p, w_o, x_p, mp_l[5], seq_rows=seq, a_transposed=True)

            row_s, kcat_s, _, qt_s = _mla_pre(x_s, g[1][None], ms_l[3], ms_l[4], cos_s, sin_s,
                                              w1, qn, kn, w2t, w3, seq_rows=1)
            q_s = jnp.transpose(qt_s.reshape(qt_s.shape[0], kl + LANES, nh, -1), (0, 3, 2, 1))
            q_s = q_s.reshape(dec_batch, nh, kl + LANES)
            o_lat_s = _mla_decode(page_table, q_s, kcat_s[:, None, :],
                                  cache_t, layer=a, scale=mla_scale)
            oh_s = _uv(jnp.transpose(o_lat_s, (1, 0, 2)), wuv)
            x_s = _mm_resid(oh_s, w_o, x_s, ms_l[5], seq_rows=1)
            rows_p.append(row_p.reshape(batch, seq, kl + MLA_ROPE))
            rows_s.append(row_s.reshape(dec_batch, 1, kl + MLA_ROPE))
        else:
            b = l - n_a
            w_q = swa_w_q[b].astype(BF)
            w_o = swa_w_o[b].astype(BF)
            (q_p,) = _mod_mm(x_p, g[1][None], mp_l[3], mp_l[4], w_q, (BF,), seq_rows=seq)
            sink_rows = jnp.repeat(swa_sinks[b].reshape(n_kv, pairs_per_kv, 2).transpose(0, 2, 1), BLOCK,
                                   axis=2)[..., None]
            o_p = _swa_prompt(q_p, kv4_p, bias_kv, mask2, sink_rows, batch=batch, seq=seq, n_kv=n_kv,
                              scale=swa_scale)
            x_p = _mm_resid(o_p, w_o, x_p, mp_l[5], seq_rows=seq)

            (q_s,) = _mod_mm(x_s, g[1][None], ms_l[3], ms_l[4], w_q, (BF,), seq_rows=1)
            o_s = _swa_sample(q_s.reshape(dec_batch, swa_heads, SWA_HEAD_DIM), keys_s_bf, vals_s_bf, bias_s,
                              swa_sinks[b].reshape(n_kv, SWA_GROUP, 1), n_keys=n_keys_s, scale=swa_scale)
            x_s = _mm_resid(o_s.reshape(dec_batch, swa_heads * SWA_HEAD_DIM), w_o, x_s, ms_l[5], seq_rows=1)

        x_p = _ffn(x_p, g[2][None], mp_l[6], mp_l[7], mp_l[8], w_in_bf, w_out_bf, l, 1, seq_rows=seq)
        x_s = _ffn(x_s, g[2][None], ms_l[6], ms_l[7], ms_l[8], w_in_bf, w_out_bf, l, 1, seq_rows=1)

    y_prompt = _final_norm(x_p, final_norm_g[None]).reshape(batch, seq, d)
    y_sample = _final_norm(x_s, final_norm_g[None]).reshape(dec_batch, 1, d)
    new_mla_prompt = jnp.stack(rows_p, axis=1)
    new_mla_sample = jnp.stack(rows_s, axis=1)
    win_p = min(WINDOW, seq)
    return (y_prompt, y_sample, new_mla_prompt, new_mla_sample,
            k_p[:, seq - win_p:], v_p[:, seq - win_p:],
            keys_s[:, keys_s.shape[1] - win_s:], vals_s[:, vals_s.shape[1] - win_s:])
```

```python
import functools
import math

import jax
import jax.numpy as jnp
import numpy as np
from jax import lax
from jax.experimental import pallas as pl
from jax.experimental.pallas import tpu as pltpu

EPS = 1e-6
NEG_INF = -1e30
ROPE_THETA = 10000.0
PAGE_SIZE = 128
WINDOW = 128
BLOCK = 128
NUM_BUCKETS = 32
MAX_DISTANCE = 128
MLA_NOPE = 128
MLA_ROPE = 64
MLA_V = 128
SWA_HEAD_DIM = 64
SWA_GROUP = 8
N_SUB = 3

LANES = 128
V7X_VMEM_LIMIT_BYTES = 56 * 1024 * 1024

BF = jnp.bfloat16
F32 = jnp.float32


def _params(*sem):
    return pltpu.CompilerParams(dimension_semantics=sem, vmem_limit_bytes=V7X_VMEM_LIMIT_BYTES)


def _pick(n, prefs):
    for p in prefs:
        if p <= n and n % p == 0:
            return p
    return n


def _row_tile(m, seq_rows, prefs):
    return _pick(m if seq_rows == 1 else seq_rows, prefs)


def _const_spec(arr):
    nd = arr.ndim
    return pl.BlockSpec(arr.shape, lambda *_: (0,) * nd, pipeline_mode=pl.Buffered(1))


def _silu(x):
    return x * (1.0 / (1.0 + jnp.exp(-x)))


def _rms(x, g):
    return x * lax.rsqrt(jnp.mean(x * x, axis=-1, keepdims=True) + EPS) * g


def _modulate(x, g, shift, scale):
    return _rms(x, g) * (1.0 + scale) + shift


def _dot(a, b):
    return jnp.dot(a, b, preferred_element_type=F32)


def _dot_nt(a, b):
    return lax.dot_general(a, b, (((1,), (1,)), ((), ())), preferred_element_type=F32)


def _mod_spec(arr, tm, seq_rows):
    _, r, d = arr.shape
    if r == 1:
        return pl.BlockSpec((None, 1, d), lambda i, *_: ((i * tm) // seq_rows, 0, 0))
    return pl.BlockSpec((None, tm, d), lambda i, *_: (0, i, 0))


def _pos_spec(arr, tm):
    p = arr.shape[0]
    if p == 1:
        return pl.BlockSpec((1, LANES), lambda i, *_: (0, 0))
    nblk = p // tm
    return pl.BlockSpec((tm, LANES), lambda i, *_: (i % nblk, 0))


def _ada_kernel(c_ref, w_ref, b_ref, o_ref):
    a = _silu(c_ref[...]).astype(BF)
    o_ref[...] = _dot(a, w_ref[...].astype(BF)) + b_ref[...]


def _ada(c, w, b):
    nl, d, n = w.shape
    m = c.shape[0]
    tn = _pick(n, (1024, 512, 256, 128))
    return pl.pallas_call(
        _ada_kernel,
        grid=(nl, n // tn),
        in_specs=[
            pl.BlockSpec((m, d), lambda l, j: (0, 0)),
            pl.BlockSpec((None, d, tn), lambda l, j: (l, 0, j)),
            pl.BlockSpec((None, 1, tn), lambda l, j: (l, 0, j)),
        ],
        out_specs=pl.BlockSpec((None, m, tn), lambda l, j: (l, 0, j)),
        out_shape=jax.ShapeDtypeStruct((nl, m, n), F32),
        compiler_params=_params("parallel", "parallel"),
        name="ada_params",
    )(c, w, b)


def _ffn_kernel(x_ref, g_ref, sh_ref, sc_ref, wg_ref, wu_ref, wo_ref, xc_ref, gt_ref, o_ref, h_ref, a_ref, *, nf, tf):
    j = pl.program_id(1)

    @pl.when(j == 0)
    def _():
        h_ref[...] = _modulate(x_ref[...], g_ref[...], sh_ref[...], sc_ref[...]).astype(BF)

    @pl.when(j < nf)
    def _():
        h = h_ref[...]
        gate = _dot(h, wg_ref[...])
        up = _dot(h, wu_ref[...])
        a_ref[j] = (_silu(gate) * up).astype(BF)

    @pl.when(j >= nf)
    def _():
        y = _dot(a_ref[0], wo_ref[0:tf, :])
        for c in range(1, nf):
            y = y + _dot(a_ref[c], wo_ref[c * tf:(c + 1) * tf, :])
        o_ref[...] = xc_ref[...] + 0.5 * gt_ref[...] * y


def _ffn(x, g, shift, scale, gate, w_in, w_out, layer, half, *, seq_rows):
    m, d = x.shape
    f = w_out.shape[2]
    tm = _row_tile(m, seq_rows, (1024, 512, 256, 128))
    if tm <= 128:
        tf = _pick(f, (1408, 512, 256, 128))
        tn = _pick(d, (1024, 512, 256, 128))
    else:
        tf = _pick(f, (512, 256, 128))
        tn = _pick(d, (256, 128))
    nf = f // tf
    ms = _mod_spec(shift, tm, seq_rows)
    col = lambda j: jnp.maximum(j - nf, 0)
    if gate.shape[1] == 1:
        gate_spec = pl.BlockSpec((None, 1, tn), lambda i, j: ((i * tm) // seq_rows, 0, col(j)))
    else:
        gate_spec = pl.BlockSpec((None, tm, tn), lambda i, j: (0, i, col(j)))
    return pl.pallas_call(
        functools.partial(_ffn_kernel, nf=nf, tf=tf),
        grid=(m // tm, nf + d // tn),
        in_specs=[
            pl.BlockSpec((tm, d), lambda i, j: (i, 0), pipeline_mode=pl.Buffered(1)),
            pl.BlockSpec((1, d), lambda i, j: (0, 0)),
            ms, ms,
            pl.BlockSpec((None, None, d, tf), lambda i, j: (layer, half, 0, jnp.minimum(j, nf - 1))),
            pl.BlockSpec((None, None, d, tf), lambda i, j: (layer, half, 0, jnp.minimum(j, nf - 1) + nf)),
            pl.BlockSpec((None, None, f, tn), lambda i, j: (layer, half, 0, col(j))),
            pl.BlockSpec((tm, tn), lambda i, j: (i, col(j))),
            gate_spec,
        ],
        out_specs=pl.BlockSpec((tm, tn), lambda i, j: (i, col(j))),
        out_shape=jax.ShapeDtypeStruct((m, d), F32),
        scratch_shapes=[pltpu.VMEM((tm, d), BF), pltpu.VMEM((nf, tm, tf), BF)],
        compiler_params=_params("parallel", "arbitrary"),
        name="ffn_half",
    )(x, g, shift, scale, w_in, w_in, w_out, x, gate)


def _mod_mm_kernel(x_ref, g_ref, sh_ref, sc_ref, w_ref, *o_refs, out_transposed):
    h = _modulate(x_ref[...], g_ref[...], sh_ref[...], sc_ref[...]).astype(BF)
    y = _dot_nt(w_ref[...], h) if out_transposed else _dot(h, w_ref[...])
    for o_ref in o_refs:
        o_ref[...] = y.astype(o_ref.dtype)


def _mod_mm(x, g, shift, scale, w, out_dtypes, *, seq_rows, out_transposed=False):
    m, d = x.shape
    n = w.shape[0] if out_transposed else w.shape[1]
    tm = _row_tile(m, seq_rows, (512, 256, 128))
    ms = _mod_spec(shift, tm, seq_rows)
    if out_transposed:
        out_spec, out_dims = pl.BlockSpec((n, tm), lambda i: (0, i)), (n, m)
    else:
        out_spec, out_dims = pl.BlockSpec((tm, n), lambda i: (i, 0)), (m, n)
    outs = pl.pallas_call(
        functools.partial(_mod_mm_kernel, out_transposed=out_transposed),
        grid=(m // tm,),
        in_specs=[
            pl.BlockSpec((tm, d), lambda i: (i, 0)),
            pl.BlockSpec((1, d), lambda i: (0, 0)),
            ms, ms,
            _const_spec(w),
        ],
        out_specs=[out_spec for _ in out_dtypes],
        out_shape=[jax.ShapeDtypeStruct(out_dims, dt) for dt in out_dtypes],
        compiler_params=_params("parallel"),
        name="mod_matmul",
    )(x, g, shift, scale, w)
    return outs


def _mm_resid_kernel(a_ref, w_ref, x_ref, gt_ref, o_ref, *, a_transposed):
    if a_transposed:
        y = lax.dot_general(a_ref[...], w_ref[...], (((0,), (0,)), ((), ())), preferred_element_type=F32)
    else:
        y = _dot(a_ref[...], w_ref[...])
    o_ref[...] = x_ref[...] + gt_ref[...] * y


def _mm_resid(a, w, x, gate, *, seq_rows, a_transposed=False):
    m, n = x.shape
    k = w.shape[0]
    tm = _row_tile(m, seq_rows, (512, 256, 128))
    return pl.pallas_call(
        functools.partial(_mm_resid_kernel, a_transposed=a_transposed),
        grid=(m // tm,),
        in_specs=[
            pl.BlockSpec((k, tm), lambda i: (0, i)) if a_transposed else pl.BlockSpec((tm, k), lambda i: (i, 0)),
            _const_spec(w),
            pl.BlockSpec((tm, n), lambda i: (i, 0)),
            _mod_spec(gate, tm, seq_rows),
        ],
        out_specs=pl.BlockSpec((tm, n), lambda i: (i, 0)),
        out_shape=jax.ShapeDtypeStruct((m, n), F32),
        compiler_params=_params("parallel"),
        name="matmul_residual",
    )(a, w, x, gate)


def _mla_pre_kernel(x_ref, g_ref, sh_ref, sc_ref, cos_ref, sin_ref, cost_ref, sint_ref, w1_ref, qn_ref, kn_ref,
                    w2t_ref, w3_ref, row_ref, kcat_ref, latt_ref, qt_ref, *, ql, kl, nh):
    tm = x_ref.shape[0]
    cos_t = cos_ref[...]
    sin_t = sin_ref[...]
    h = _modulate(x_ref[...], g_ref[...], sh_ref[...], sc_ref[...]).astype(BF)
    p1 = _dot(h, w1_ref[...])
    cq = _rms(p1[:, :ql], qn_ref[...])
    lat = _rms(p1[:, ql:ql + kl], kn_ref[...])
    a = ql + kl
    kr = p1[:, a:a + LANES] * cos_t + p1[:, a + LANES:a + 2 * LANES] * sin_t
    row_ref[:, :kl] = lat
    row_ref[:, kl:] = kr[:, :MLA_ROPE]
    kcat_ref[:, :kl] = lat.astype(BF)
    kcat_ref[:, kl:] = kr.astype(BF)
    latt_ref[...] = lat.T.astype(BF)
    q2t = _dot(w2t_ref[...], cq.T.astype(BF))
    cos_c = cost_ref[...]
    sin_c = sint_ref[...]
    hn = nh * LANES
    for hd in range(nh):
        lo = hd * LANES
        qn_t = q2t[lo:lo + LANES].astype(BF)
        qt_ref[:kl, hd * tm:(hd + 1) * tm] = _dot(w3_ref[hd], qn_t).astype(BF)
        qr_t = q2t[hn + lo:hn + lo + LANES] * cos_c + q2t[2 * hn + lo:2 * hn + lo + LANES] * sin_c
        qt_ref[kl:, hd * tm:(hd + 1) * tm] = qr_t.astype(BF)


def _mla_pre(x, g, shift, scale, cos_t, sin_t, w1, qnorm, knorm, w2t, w3, *, seq_rows):
    m, d = x.shape
    ql = qnorm.shape[1]
    kl = knorm.shape[1]
    nh = w3.shape[0]
    c = kl + LANES
    tm = _row_tile(m, seq_rows, (256, 128))
    ms = _mod_spec(shift, tm, seq_rows)
    ps = _pos_spec(cos_t, tm)
    cos_c, sin_c = cos_t.T, sin_t.T
    npos = cos_t.shape[0]
    if npos == 1:
        pst = pl.BlockSpec((LANES, 1), lambda i: (0, 0))
    else:
        pst = pl.BlockSpec((LANES, tm), lambda i: (0, i % (npos // tm)))
    return pl.pallas_call(
        functools.partial(_mla_pre_kernel, ql=ql, kl=kl, nh=nh),
        grid=(m // tm,),
        in_specs=[
            pl.BlockSpec((tm, d), lambda i: (i, 0)),
            pl.BlockSpec((1, d), lambda i: (0, 0)),
            ms, ms, ps, ps, pst, pst,
            _const_spec(w1), _const_spec(qnorm), _const_spec(knorm), _const_spec(w2t), _const_spec(w3),
        ],
        out_specs=[
            pl.BlockSpec((tm, kl + MLA_ROPE), lambda i: (i, 0)),
            pl.BlockSpec((tm, c), lambda i: (i, 0)),
            pl.BlockSpec((kl, tm), lambda i: (0, i)),
            pl.BlockSpec((None, c, nh * tm), lambda i: (i, 0, 0)),
        ],
        out_shape=[
            jax.ShapeDtypeStruct((m, kl + MLA_ROPE), F32),
            jax.ShapeDtypeStruct((m, c), BF),
            jax.ShapeDtypeStruct((kl, m), BF),
            jax.ShapeDtypeStruct((m // tm, c, nh * tm), BF),
        ],
        compiler_params=_params("parallel"),
        name="mla_project",
    )(x, g, shift, scale, cos_t, sin_t, cos_c, sin_c, w1, qnorm, knorm, w2t, w3)


def _mla_flash_kernel(it_ref, jt_ref, qt_ref, k_ref, vt_ref, wuvt_ref, o_ref, m_sc, l_sc, acc_sc, *,
                      tq, tk, nh, exp2_scale):
    t = pl.program_id(1)
    i = it_ref[t]
    j = jt_ref[t]
    cols = nh * tq
    j_last = ((i + 1) * tq - 1) // tk

    @pl.when(j == 0)
    def _():
        m_sc[...] = jnp.full(m_sc.shape, NEG_INF, F32)
        l_sc[...] = jnp.zeros(l_sc.shape, F32)
        acc_sc[...] = jnp.zeros(acc_sc.shape, F32)

    def step(masked):
        s = _dot(k_ref[...], qt_ref[...])
        if masked:
            kpos = j * tk + lax.broadcasted_iota(jnp.int32, (tk, cols), 0)
            qpos = i * tq + (lax.broadcasted_iota(jnp.int32, (tk, cols), 1) & (tq - 1))
            s = jnp.where(kpos <= qpos, s, NEG_INF)
        m_old = m_sc[...]
        m_new = jnp.maximum(m_old, jnp.max(s, axis=0, keepdims=True))
        alpha = jnp.exp2((m_old - m_new) * exp2_scale)
        p = jnp.exp2((s - m_new) * exp2_scale)
        l_sc[...] = alpha * l_sc[...] + jnp.sum(p, axis=0, keepdims=True)
        acc_sc[...] = alpha * acc_sc[...] + _dot(vt_ref[...], p.astype(BF))
        m_sc[...] = m_new

    needs_mask = (j + 1) * tk - 1 > i * tq

    @pl.when(needs_mask)
    def _():
        step(True)

    @pl.when(jnp.logical_not(needs_mask))
    def _():
        step(False)

    @pl.when(j == j_last)
    def _():
        o_t = (acc_sc[...] * (1.0 / l_sc[...])).astype(BF)
        for hd in range(nh):
            o_ref[hd * MLA_V:(hd + 1) * MLA_V, :] = _dot(wuvt_ref[hd], o_t[:, hd * tq:(hd + 1) * tq]).astype(BF)


def _mla_flash(qt, kcat, latt, wuvt, *, batch, seq, scale):
    nblk, c, cols = qt.shape
    nh, _, kl = wuvt.shape
    tq = cols // nh
    m = nblk * tq
    tk = _pick(seq, (512, 256, 128))
    nq = seq // tq
    nk = seq // tk
    sched = [(i, j) for i in range(nq) for j in range(((i + 1) * tq - 1) // tk + 1)]
    it = jnp.asarray(np.array([s[0] for s in sched], np.int32))
    jt = jnp.asarray(np.array([s[1] for s in sched], np.int32))
    grid_spec = pltpu.PrefetchScalarGridSpec(
        num_scalar_prefetch=2,
        grid=(batch, len(sched)),
        in_specs=[
            pl.BlockSpec((None, c, cols), lambda b, t, it_ref, jt_ref: (b * nq + it_ref[t], 0, 0)),
            pl.BlockSpec((tk, c), lambda b, t, it_ref, jt_ref: (b * nk + jt_ref[t], 0)),
            pl.BlockSpec((kl, tk), lambda b, t, it_ref, jt_ref: (0, b * nk + jt_ref[t])),
            pl.BlockSpec(wuvt.shape, lambda b, t, it_ref, jt_ref: (0, 0, 0), pipeline_mode=pl.Buffered(1)),
        ],
        out_specs=pl.BlockSpec((nh * MLA_V, tq), lambda b, t, it_ref, jt_ref: (0, b * nq + it_ref[t])),
        scratch_shapes=[
            pltpu.VMEM((1, cols), F32),
            pltpu.VMEM((1, cols), F32),
            pltpu.VMEM((kl, cols), F32),
        ],
    )
    return pl.pallas_call(
        functools.partial(_mla_flash_kernel, tq=tq, tk=tk, nh=nh, exp2_scale=scale * math.log2(math.e)),
        grid_spec=grid_spec,
        out_shape=jax.ShapeDtypeStruct((nh * MLA_V, m), BF),
        compiler_params=_params("parallel", "arbitrary"),
        name="mla_prompt_attention",
    )(it, jt, qt, kcat, latt, wuvt)


def _mla_decode_kernel(pt_ref, q_ref, new_ref, *rest, npg, kl, scale):
    page_refs = rest[:npg]
    o_ref = rest[npg]
    kbuf, m_sc, l_sc, acc_sc = rest[npg + 1:]
    c = pl.program_id(1)

    @pl.when(c == 0)
    def _():
        m_sc[...] = jnp.full(m_sc.shape, NEG_INF, F32)
        l_sc[...] = jnp.zeros(l_sc.shape, F32)
        acc_sc[...] = jnp.zeros(acc_sc.shape, F32)

    for k, ref in enumerate(page_refs):
        kbuf[:, k * PAGE_SIZE:(k + 1) * PAGE_SIZE] = ref[...].astype(BF)

    q = q_ref[...]
    keys_t = kbuf[...]
    s = (_dot(q[:, :kl], keys_t[:kl]) + _dot(q[:, kl:kl + MLA_ROPE], keys_t[kl:])) * scale
    m_old = m_sc[...]
    m_new = jnp.maximum(m_old, jnp.max(s, axis=-1, keepdims=True))
    alpha = jnp.exp(m_old - m_new)
    p = jnp.exp(s - m_new)
    l_sc[...] = alpha * l_sc[...] + jnp.sum(p, axis=-1, keepdims=True)
    acc_sc[...] = alpha * acc_sc[...] + _dot_nt(p.astype(BF), keys_t[:kl])
    m_sc[...] = m_new

    @pl.when(c == pl.num_programs(1) - 1)
    def _():
        new = new_ref[...].astype(F32)
        s_new = jnp.sum(q.astype(F32) * new, axis=-1, keepdims=True) * scale
        m_old = m_sc[...]
        m_new = jnp.maximum(m_old, s_new)
        alpha = jnp.exp(m_old - m_new)
        p_new = jnp.exp(s_new - m_new)
        l_new = alpha * l_sc[...] + p_new
        acc = alpha * acc_sc[...] + p_new.astype(BF).astype(F32) * new[:, :kl]
        o_ref[...] = (acc * (1.0 / l_new)).astype(BF)


def _mla_decode(page_table, qcat, kcat, cache_t, *, layer, scale):
    nb, nh, c = qcat.shape
    lat = cache_t.shape[2]
    kl = lat - MLA_ROPE
    n_pages = page_table.shape[1]
    npg = _pick(n_pages, (64, 32, 16, 8, 4, 2, 1))
    pt = page_table.reshape(-1)

    def page_spec(k):
        return pl.BlockSpec((None, None, lat, PAGE_SIZE),
                            lambda b, ch, pt_ref: (pt_ref[b * n_pages + ch * npg + k], layer, 0, 0))

    grid_spec = pltpu.PrefetchScalarGridSpec(
        num_scalar_prefetch=1,
        grid=(nb, n_pages // npg),
        in_specs=[
            pl.BlockSpec((None, nh, c), lambda b, ch, pt_ref: (b, 0, 0)),
            pl.BlockSpec((None, 1, c), lambda b, ch, pt_ref: (b, 0, 0)),
        ] + [page_spec(k) for k in range(npg)],
        out_specs=pl.BlockSpec((None, nh, kl), lambda b, ch, pt_ref: (b, 0, 0)),
        scratch_shapes=[
            pltpu.VMEM((lat, npg * PAGE_SIZE), BF),
            pltpu.VMEM((nh, 1), F32),
            pltpu.VMEM((nh, 1), F32),
            pltpu.VMEM((nh, kl), F32),
        ],
    )
    return pl.pallas_call(
        functools.partial(_mla_decode_kernel, npg=npg, kl=kl, scale=scale),
        grid_spec=grid_spec,
        out_shape=jax.ShapeDtypeStruct((nb, nh, kl), BF),
        compiler_params=_params("parallel", "arbitrary"),
        name="mla_sample_attention",
    )(pt, qcat, kcat, *([cache_t] * npg))


def _uv_kernel(o_ref, wuv_ref, out_ref):
    nh = o_ref.shape[0]
    for hd in range(nh):
        out_ref[:, hd * MLA_V:(hd + 1) * MLA_V] = _dot(o_ref[hd], wuv_ref[hd]).astype(BF)


def _uv(o_lat, wuv):
    nh, m, kl = o_lat.shape
    return pl.pallas_call(
        _uv_kernel,
        grid=(1,),
        in_specs=[
            pl.BlockSpec(o_lat.shape, lambda i: (0, 0, 0)),
            pl.BlockSpec(wuv.shape, lambda i: (0, 0, 0)),
        ],
        out_specs=pl.BlockSpec((m, nh * MLA_V), lambda i: (0, 0)),
        out_shape=jax.ShapeDtypeStruct((m, nh * MLA_V), BF),
        compiler_params=_params("arbitrary"),
        name="mla_value_up",
    )(o_lat, wuv)


def _swa_prompt_kernel(qt_ref, kvp_ref, kvc_ref, bias_ref, mask_ref, sink_ref, o_ref, *, n_kv, scale):
    kvw = n_kv * LANES
    pairs_per_kv = SWA_GROUP // 2
    mask = mask_ref[...] > 0.5
    nkeys = 2 * BLOCK
    for c in range(n_kv):
        def sect(s):
            lo = s * kvw + c * LANES
            return jnp.concatenate([kvp_ref[:, lo:lo + LANES], kvc_ref[:, lo:lo + LANES]], axis=0)
        k2 = jnp.concatenate([sect(0), sect(1)], axis=0)
        v2 = jnp.concatenate([sect(2), sect(3)], axis=0)
        pr0 = c * pairs_per_kv
        qc = jnp.concatenate([qt_ref[(pr0 + pp) * LANES:(pr0 + pp + 1) * LANES, :] for pp in range(pairs_per_kv)],
                             axis=1)
        s = _dot(k2, qc) * scale + bias_ref[c]
        s = jnp.where(mask, s, NEG_INF)
        halves = []
        for hf in range(2):
            sh = s[hf * nkeys:(hf + 1) * nkeys]
            sink = sink_ref[c, hf]
            m = jnp.maximum(jnp.max(sh, axis=0, keepdims=True), sink)
            e = jnp.exp(sh - m)
            den = jnp.sum(e, axis=0, keepdims=True) + jnp.exp(sink - m)
            halves.append((e * (1.0 / den)).astype(BF))
        p = jnp.concatenate(halves, axis=0)
        o_t = lax.dot_general(v2, p, (((0,), (0,)), ((), ())), preferred_element_type=F32).astype(BF)
        for pp in range(pairs_per_kv):
            o_ref[(pr0 + pp) * LANES:(pr0 + pp + 1) * LANES, :] = o_t[:, pp * BLOCK:(pp + 1) * BLOCK]


def _swa_prompt(qt, kv4, bias_t, mask_t, sink_cols, *, batch, seq, n_kv, scale):
    dq, m = qt.shape
    nb = seq // BLOCK
    w4 = kv4.shape[1]
    return pl.pallas_call(
        functools.partial(_swa_prompt_kernel, n_kv=n_kv, scale=scale),
        grid=(batch, nb),
        in_specs=[
            pl.BlockSpec((dq, BLOCK), lambda b, j: (0, b * nb + j)),
            pl.BlockSpec((BLOCK, w4), lambda b, j: (b * nb + jnp.maximum(j - 1, 0), 0)),
            pl.BlockSpec((BLOCK, w4), lambda b, j: (b * nb + j, 0)),
            _const_spec(bias_t),
            pl.BlockSpec((None,) + mask_t.shape[1:], lambda b, j: (jnp.minimum(j, 1), 0, 0)),
            _const_spec(sink_cols),
        ],
        out_specs=pl.BlockSpec((dq, BLOCK), lambda b, j: (0, b * nb + j)),
        out_shape=jax.ShapeDtypeStruct((dq, m), BF),
        compiler_params=_params("parallel", "arbitrary"),
        name="swa_prompt_attention",
    )(qt, kv4, kv4, bias_t, mask_t, sink_cols)


def _swa_sample_kernel(q_ref, k_ref, v_ref, bias_ref, sink_ref, o_ref, *, n_kv, n_keys, scale):
    for c in range(n_kv):
        q = q_ref[:, c * SWA_GROUP:(c + 1) * SWA_GROUP, :]
        k = k_ref[:, c]
        v = v_ref[:, c]
        s = jnp.einsum("bgd,bkd->bgk", q, k, preferred_element_type=F32) * scale + bias_ref[c][None]
        col = lax.broadcasted_iota(jnp.int32, s.shape, 2)
        s = jnp.where(col < n_keys, s, NEG_INF)
        sink = sink_ref[c][None]
        m = jnp.maximum(jnp.max(s, axis=-1, keepdims=True), sink)
        e = jnp.exp(s - m)
        den = jnp.sum(e, axis=-1, keepdims=True) + jnp.exp(sink - m)
        p = (e * (1.0 / den)).astype(BF)
        o = jnp.einsum("bgk,bkd->bgd", p, v, preferred_element_type=F32)
        o_ref[:, c * SWA_GROUP:(c + 1) * SWA_GROUP, :] = o.astype(BF)


def _swa_sample(q, keys, vals, bias, sinks, *, n_keys, scale):
    nb, nh, dh = q.shape
    n_kv, kp = keys.shape[1], keys.shape[2]
    bb = _pick(nb, (16, 8))
    return pl.pallas_call(
        functools.partial(_swa_sample_kernel, n_kv=n_kv, n_keys=n_keys, scale=scale),
        grid=(nb // bb,),
        in_specs=[
            pl.BlockSpec((bb, nh, dh), lambda i: (i, 0, 0)),
            pl.BlockSpec((bb, n_kv, kp, dh), lambda i: (i, 0, 0, 0)),
            pl.BlockSpec((bb, n_kv, kp, dh), lambda i: (i, 0, 0, 0)),
            pl.BlockSpec(bias.shape, lambda i: (0, 0, 0)),
            pl.BlockSpec(sinks.shape, lambda i: (0, 0, 0)),
        ],
        out_specs=pl.BlockSpec((bb, nh, dh), lambda i: (i, 0, 0)),
        out_shape=jax.ShapeDtypeStruct((nb, nh, dh), BF),
        compiler_params=_params("parallel"),
        name="swa_sample_attention",
    )(q, keys, vals, bias, sinks)


def _final_norm_kernel(x_ref, g_ref, o_ref):
    o_ref[...] = _rms(x_ref[...], g_ref[...])


def _final_norm(x, g):
    m, d = x.shape
    tm = _pick(m, (512, 256, 128))
    return pl.pallas_call(
        _final_norm_kernel,
        grid=(m // tm,),
        in_specs=[pl.BlockSpec((tm, d), lambda i: (i, 0)), pl.BlockSpec((1, d), lambda i: (0, 0))],
        out_specs=pl.BlockSpec((tm, d), lambda i: (i, 0)),
        out_shape=jax.ShapeDtypeStruct((m, d), F32),
        compiler_params=_params("parallel"),
        name="final_norm",
    )(x, g)


def _rope_tables(pos):
    half = MLA_ROPE // 2
    inv = ROPE_THETA ** (-jnp.arange(half, dtype=F32) / half)
    ang = pos.astype(F32)[:, None] * inv
    cos, sin = jnp.cos(ang), jnp.sin(ang)
    pad = jnp.zeros((pos.shape[0], LANES - MLA_ROPE), F32)
    return jnp.concatenate([cos, cos, pad], axis=1), jnp.concatenate([-sin, sin, pad], axis=1)


def _swap_halves(w):
    half = w.shape[-1] // 2
    return jnp.concatenate([w[..., half:], w[..., :half]], axis=-1)


def _pad_lanes(w):
    return jnp.pad(w, [(0, 0)] * (w.ndim - 1) + [(0, LANES - w.shape[-1])])


def _t5_bucket(dist):
    max_exact = NUM_BUCKETS // 2
    n = jnp.maximum(dist, 0)
    nf = jnp.maximum(n, 1).astype(F32)
    large = max_exact + (jnp.log(nf / max_exact) / math.log(MAX_DISTANCE / max_exact)
                         * (NUM_BUCKETS - max_exact)).astype(jnp.int32)
    large = jnp.minimum(large, NUM_BUCKETS - 1)
    return jnp.where(n < max_exact, n, large)


def kernel(x_prompt, x_sample, c_prompt, c_sample, cache_mla, state_swa_k, state_swa_v, page_table, ada_w, ada_b, norm_g, ffn_w_in, ffn_w_out, mla_w_dq, mla_q_norm, mla_w_uq, mla_w_dkv, mla_kv_norm, mla_w_uk, mla_w_uv, mla_w_o, kv_ada_w, kv_ada_b, kv_norm_g, kv_w, swa_w_q, swa_w_o, swa_sinks, rel_bias_table, final_norm_g):
    batch, seq, d = x_prompt.shape
    dec_batch, dec_seq, _ = x_sample.shape
    assert dec_seq == 1
    depth = ada_w.shape[0]
    n_a = mla_w_dq.shape[0]
    n_pages = page_table.shape[1]
    past_len = n_pages * PAGE_SIZE
    ql = mla_w_dq.shape[2]
    kl = mla_kv_norm.shape[1]
    nh = mla_w_uk.shape[2]
    swa_heads = swa_w_q.shape[2] // SWA_HEAD_DIM
    n_kv = swa_heads // SWA_GROUP
    win_s = state_swa_k.shape[1]
    assert win_s == WINDOW and seq % BLOCK == 0
    mla_scale = float((MLA_NOPE + MLA_ROPE) ** -0.5)
    swa_scale = float(SWA_HEAD_DIM ** -0.5)
    mp = batch * seq

    c_all = jnp.concatenate([c_prompt, c_sample], axis=0)
    n_seq = c_all.shape[0]
    c_all = jnp.pad(c_all, ((0, (-n_seq) % 16), (0, 0)))
    mod = _ada(c_all, ada_w, ada_b[:, None, :])
    kvmod = _ada(c_all, kv_ada_w[None], kv_ada_b[None, None, :])[0]

    def mods_p(arr, n):
        return arr[:batch, n * d:(n + 1) * d][:, None, :]

    def mods_s(arr, n):
        return arr[batch:batch + dec_batch, n * d:(n + 1) * d][None]

    cos_p, sin_p = _rope_tables(jnp.arange(seq, dtype=jnp.int32))
    cos_s, sin_s = _rope_tables(jnp.full((1,), past_len, jnp.int32))

    qi = jnp.arange(BLOCK, dtype=jnp.int32)
    km = jnp.arange(2 * BLOCK, dtype=jnp.int32) - BLOCK
    d_p = qi[:, None] - km[None, :]
    win_mask = (d_p >= 0) & (d_p <= WINDOW)
    onehot_p = (_t5_bucket(d_p)[:, :, None] == jnp.arange(NUM_BUCKETS, dtype=jnp.int32)).astype(F32)
    bias_p = jnp.einsum("qkb,bh->hqk", onehot_p, rel_bias_table.astype(F32),
                        precision=lax.Precision.HIGHEST)
    bias_pairs = bias_p.reshape(swa_heads // 2, 2, BLOCK, 2 * BLOCK).transpose(0, 2, 1, 3)
    pairs_per_kv = SWA_GROUP // 2
    bias_t = jnp.transpose(bias_pairs.reshape(n_kv, pairs_per_kv * BLOCK, 4 * BLOCK), (0, 2, 1))
    mask_first = win_mask & (km >= 0)[None, :]
    mask_t = jnp.stack([jnp.tile(mask_first, (pairs_per_kv, 2)).T,
                        jnp.tile(win_mask, (pairs_per_kv, 2)).T]).astype(F32)
    kp_pad = 2 * WINDOW
    n_keys_s = win_s + 1
    d_s = jnp.concatenate([jnp.arange(win_s, 0, -1, dtype=jnp.int32), jnp.zeros((1,), jnp.int32)])
    bias_s = jnp.transpose(rel_bias_table[_t5_bucket(d_s)], (1, 0)).astype(F32)
    bias_s = jnp.pad(bias_s, ((0, 0), (0, kp_pad - n_keys_s))).reshape(n_kv, SWA_GROUP, kp_pad)

    cache_t = jnp.transpose(cache_mla, (0, 1, 3, 2))

    w_in_bf = ffn_w_in.astype(BF)
    w_out_bf = ffn_w_out.astype(BF)

    x_p = x_prompt.reshape(mp, d)
    x_s = x_sample.reshape(dec_batch, d)
    rows_p, rows_s = [], []
    kv4_p = keys_s = vals_s = keys_s_bf = vals_s_bf = None
    k_p = v_p = None

    for l in range(depth):
        if l == n_a:
            kv_k = kv_w[:, :n_kv * SWA_HEAD_DIM].reshape(d, n_kv, SWA_HEAD_DIM)
            kv_v = kv_w[:, n_kv * SWA_HEAD_DIM:].reshape(d, n_kv, SWA_HEAD_DIM)
            zero = jnp.zeros_like(kv_k)
            lo = lambda w: jnp.concatenate([w, zero], axis=-1).reshape(d, n_kv * LANES)
            hi = lambda w: jnp.concatenate([zero, w], axis=-1).reshape(d, n_kv * LANES)
            nkv = 2 * n_kv * SWA_HEAD_DIM
            w_sh = jnp.concatenate([kv_w, lo(kv_k), hi(kv_k), lo(kv_v), hi(kv_v)], axis=1).astype(BF)
            kvg = kv_norm_g[None]
            kv_f32_p, kv_bf_p = _mod_mm(x_p, kvg, mods_p(kvmod, 0), mods_p(kvmod, 1), w_sh, (F32, BF),
                                        seq_rows=seq)
            k_p = kv_f32_p[:, :nkv // 2].reshape(batch, seq, n_kv, SWA_HEAD_DIM)
            v_p = kv_f32_p[:, nkv // 2:nkv].reshape(batch, seq, n_kv, SWA_HEAD_DIM)
            kv4_p = kv_bf_p[:, nkv:]
            (kv_f32_s,) = _mod_mm(x_s, kvg, mods_s(kvmod, 0), mods_s(kvmod, 1), w_sh[:, :nkv], (F32,),
                                  seq_rows=1)
            k_new = kv_f32_s[:, :nkv // 2].reshape(dec_batch, 1, n_kv, SWA_HEAD_DIM)
            v_new = kv_f32_s[:, nkv // 2:].reshape(dec_batch, 1, n_kv, SWA_HEAD_DIM)
            keys_s = jnp.concatenate([state_swa_k, k_new], axis=1)
            vals_s = jnp.concatenate([state_swa_v, v_new], axis=1)
            padk = lambda t: jnp.pad(jnp.transpose(t, (0, 2, 1, 3)).astype(BF),
                                     ((0, 0), (0, 0), (0, kp_pad - n_keys_s), (0, 0)))
            keys_s_bf, vals_s_bf = padk(keys_s), padk(vals_s)

        g = norm_g[l]
        mp_l = [mods_p(mod[l], n) for n in range(3 * N_SUB)]
        ms_l = [mods_s(mod[l], n) for n in range(3 * N_SUB)]

        x_p = _ffn(x_p, g[0][None], mp_l[0], mp_l[1], mp_l[2], w_in_bf, w_out_bf, l, 0, seq_rows=seq)
        x_s = _ffn(x_s, g[0][None], ms_l[0], ms_l[1], ms_l[2], w_in_bf, w_out_bf, l, 0, seq_rows=1)

        if l < n_a:
            a = l
            dkv = mla_w_dkv[a]
            w1 = jnp.concatenate([mla_w_dq[a], dkv[:, :kl], _pad_lanes(dkv[:, kl:]),
                                  _pad_lanes(_swap_halves(dkv[:, kl:]))], axis=1).astype(BF)
            wuq = mla_w_uq[a].reshape(ql, nh, MLA_NOPE + MLA_ROPE)
            wuq_r = wuq[:, :, MLA_NOPE:]
            w2t = jnp.concatenate([wuq[:, :, :MLA_NOPE].reshape(ql, nh * MLA_NOPE),
                                   _pad_lanes(wuq_r).reshape(ql, nh * LANES),
                                   _pad_lanes(_swap_halves(wuq_r)).reshape(ql, nh * LANES)], axis=1).T.astype(BF)
            w3 = jnp.transpose(mla_w_uk[a], (1, 0, 2)).astype(BF)
            wuv = jnp.transpose(mla_w_uv[a], (1, 0, 2)).astype(BF)
            wuvt = jnp.transpose(mla_w_uv[a], (1, 2, 0)).astype(BF)
            w_o = mla_w_o[a].astype(BF)
            qn, kn = mla_q_norm[a][None], mla_kv_norm[a][None]

            row_p, kcat_p, latt_p, qt_p = _mla_pre(x_p, g[1][None], mp_l[3], mp_l[4], cos_p, sin_p,
                                                   w1, qn, kn, w2t, w3, seq_rows=seq)
            oh_t_p = _mla_flash(qt_p, kcat_p, latt_p, wuvt, batch=batch, seq=seq, scale=mla_scale)
            x_p = _mm_resid(oh_t_p, w_o, x_p, mp_l[5], seq_rows=seq, a_transposed=True)

            row_s, kcat_s, _, qt_s = _mla_pre(x_s, g[1][None], ms_l[3], ms_l[4], cos_s, sin_s,
                                              w1, qn, kn, w2t, w3, seq_rows=1)
            q_s = jnp.transpose(qt_s.reshape(qt_s.shape[0], kl + LANES, nh, -1), (0, 3, 2, 1))
            q_s = q_s.reshape(dec_batch, nh, kl + LANES)
            o_lat_s = _mla_decode(page_table, q_s, kcat_s[:, None, :],
                                  cache_t, layer=a, scale=mla_scale)
            oh_s = _uv(jnp.transpose(o_lat_s, (1, 0, 2)), wuv)
            x_s = _mm_resid(oh_s, w_o, x_s, ms_l[5], seq_rows=1)
            rows_p.append(row_p.reshape(batch, seq, kl + MLA_ROPE))
            rows_s.append(row_s.reshape(dec_batch, 1, kl + MLA_ROPE))
        else:
            b = l - n_a
            w_q = swa_w_q[b].astype(BF)
            w_o = swa_w_o[b].astype(BF)
            (qt_p,) = _mod_mm(x_p, g[1][None], mp_l[3], mp_l[4], swa_w_q[b].T.astype(BF), (BF,), seq_rows=seq,
                              out_transposed=True)
            sink_cols = jnp.repeat(swa_sinks[b].reshape(n_kv, pairs_per_kv, 2).transpose(0, 2, 1), BLOCK,
                                   axis=2)[:, :, None, :]
            ot_p = _swa_prompt(qt_p, kv4_p, bias_t, mask_t, sink_cols, batch=batch, seq=seq, n_kv=n_kv,
                               scale=swa_scale)
            x_p = _mm_resid(ot_p, w_o, x_p, mp_l[5], seq_rows=seq, a_transposed=True)

            (q_s,) = _mod_mm(x_s, g[1][None], ms_l[3], ms_l[4], w_q, (BF,), seq_rows=1)
            o_s = _swa_sample(q_s.reshape(dec_batch, swa_heads, SWA_HEAD_DIM), keys_s_bf, vals_s_bf, bias_s,
                              swa_sinks[b].reshape(n_kv, SWA_GROUP, 1), n_keys=n_keys_s, scale=swa_scale)
            x_s = _mm_resid(o_s.reshape(dec_batch, swa_heads * SWA_HEAD_DIM), w_o, x_s, ms_l[5], seq_rows=1)

        x_p = _ffn(x_p, g[2][None], mp_l[6], mp_l[7], mp_l[8], w_in_bf, w_out_bf, l, 1, seq_rows=seq)
        x_s = _ffn(x_s, g[2][None], ms_l[6], ms_l[7], ms_l[8], w_in_bf, w_out_bf, l, 1, seq_rows=1)

    y_prompt = _final_norm(x_p, final_norm_g[None]).reshape(batch, seq, d)
    y_sample = _final_norm(x_s, final_norm_g[None]).reshape(dec_batch, 1, d)
    new_mla_prompt = jnp.stack(rows_p, axis=1)
    new_mla_sample = jnp.stack(rows_s, axis=1)
    win_p = min(WINDOW, seq)
    return (y_prompt, y_sample, new_mla_prompt, new_mla_sample,
            k_p[:, seq - win_p:], v_p[:, seq - win_p:],
            keys_s[:, keys_s.shape[1] - win_s:], vals_s[:, vals_s.shape[1] - win_s:])
```

```python
import functools
import math
from typing import NamedTuple

import jax
import jax.numpy as jnp
import numpy as np
from jax import lax
from jax.experimental import pallas as pl
from jax.experimental.pallas import tpu as pltpu

EPS = 1e-6
NEG_INF = -1e30
ROPE_THETA = 10000.0
PAGE_SIZE = 128
WINDOW = 128
BLOCK = 128
NUM_BUCKETS = 32
MAX_DISTANCE = 128
MLA_NOPE = 128
MLA_ROPE = 64
MLA_V = 128
SWA_HEAD_DIM = 64
SWA_GROUP = 8
N_SUB = 3

LANES = 128
V7X_VMEM_LIMIT_BYTES = 56 * 1024 * 1024

BF = jnp.bfloat16
F32 = jnp.float32


def _params(*sem):
    return pltpu.CompilerParams(dimension_semantics=sem, vmem_limit_bytes=V7X_VMEM_LIMIT_BYTES)


def _pick(n, prefs):
    for p in prefs:
        if p <= n and n % p == 0:
            return p
    return n


def _row_tile(m, seq_rows, prefs):
    return _pick(m if seq_rows == 1 else seq_rows, prefs)


def _const_spec(arr):
    nd = arr.ndim
    return pl.BlockSpec(arr.shape, lambda *_: (0,) * nd, pipeline_mode=pl.Buffered(1))


def _silu(x):
    return x * (1.0 / (1.0 + jnp.exp(-x)))


def _rms(x, g):
    return x * lax.rsqrt(jnp.mean(x * x, axis=-1, keepdims=True) + EPS) * g


def _modulate(x, g, shift, scale):
    return _rms(x, g) * (1.0 + scale) + shift


def _dot(a, b):
    return jnp.dot(a, b, preferred_element_type=F32)


def _dot_nt(a, b):
    return lax.dot_general(a, b, (((1,), (1,)), ((), ())), preferred_element_type=F32)


class _Mod(NamedTuple):
    arr: jax.Array
    layer: int
    n: int


def _mod_spec(mod, tm, seq_rows, d, tn=None, col=None):
    tn = d if tn is None else tn
    base = mod.n * (d // tn)
    cidx = (lambda rest: base + col(rest[0])) if col is not None else (lambda rest: base)
    if mod.arr.ndim == 4:
        return pl.BlockSpec((None, None, 1, tn), lambda i, *rest: (mod.layer, (i * tm) // seq_rows, 0, cidx(rest)))
    return pl.BlockSpec((None, tm, tn), lambda i, *rest: (mod.layer, i, cidx(rest)))


def _pos_spec(arr, tm):
    p = arr.shape[0]
    if p == 1:
        return pl.BlockSpec((1, LANES), lambda i, *_: (0, 0))
    nblk = p // tm
    return pl.BlockSpec((tm, LANES), lambda i, *_: (i % nblk, 0))


def _ada_kernel(c_ref, w_ref, b_ref, o_ref):
    a = _silu(c_ref[...]).astype(BF)
    o_ref[...] = _dot(a, w_ref[...].astype(BF)) + b_ref[...]


def _ada(c, w, b):
    nl, d, n = w.shape
    m = c.shape[0]
    tn = _pick(n, (1024, 512, 256, 128))
    return pl.pallas_call(
        _ada_kernel,
        grid=(nl, n // tn),
        in_specs=[
            pl.BlockSpec((m, d), lambda l, j: (0, 0)),
            pl.BlockSpec((None, d, tn), lambda l, j: (l, 0, j)),
            pl.BlockSpec((None, 1, tn), lambda l, j: (l, 0, j)),
        ],
        out_specs=pl.BlockSpec((None, m, tn), lambda l, j: (l, 0, j)),
        out_shape=jax.ShapeDtypeStruct((nl, m, n), F32),
        compiler_params=_params("parallel", "parallel"),
        name="ada_params",
    )(c, w, b)


def _ffn_kernel(x_ref, g_ref, sh_ref, sc_ref, wg_ref, wu_ref, wo_ref, xc_ref, gt_ref, o_ref, *rest, nf, tf,
                emit_bf16):
    if emit_bf16:
        wg_out, wu_out, wo_out, h_ref, a_ref = rest
    else:
        h_ref, a_ref = rest
    j = pl.program_id(1)

    @pl.when(j == 0)
    def _():
        h_ref[...] = _modulate(x_ref[...], g_ref[...], sh_ref[...], sc_ref[...]).astype(BF)

    @pl.when(j < nf)
    def _():
        h = h_ref[...]
        wg = wg_ref[...].astype(BF)
        wu = wu_ref[...].astype(BF)
        if emit_bf16:
            wg_out[...] = wg
            wu_out[...] = wu
        gate = _dot(h, wg)
        up = _dot(h, wu)
        a_ref[j] = (_silu(gate) * up).astype(BF)

    @pl.when(j >= nf)
    def _():
        if emit_bf16:
            wo_out[...] = wo_ref[...].astype(BF)
            wo = wo_out
        else:
            wo = wo_ref
        y = _dot(a_ref[0], wo[0:tf, :])
        for c in range(1, nf):
            y = y + _dot(a_ref[c], wo[c * tf:(c + 1) * tf, :])
        o_ref[...] = xc_ref[...] + 0.5 * gt_ref[...] * y


def _ffn(x, g, shift, scale, gate, weights, *, seq_rows):
    emit_bf16 = len(weights) == 4
    m, d = x.shape
    tm = _row_tile(m, seq_rows, (1024, 512, 256, 128))
    if emit_bf16:
        w_in, w_out, layer, half = weights
        f = w_out.shape[2]
        assert m == tm, "the bf16 weight copies are written by a single row tile"
    else:
        wg, wu, wo = weights
        f = wo.shape[0]
    tf = _pick(f, (512, 256, 128))
    tn = _pick(d, (256, 128))
    nf = f // tf
    sh_spec = _mod_spec(shift, tm, seq_rows, d)
    sc_spec = _mod_spec(scale, tm, seq_rows, d)
    col = lambda j: jnp.maximum(j - nf, 0)
    fcol = lambda j: jnp.minimum(j, nf - 1)
    gate_spec = _mod_spec(gate, tm, seq_rows, d, tn=tn, col=col)
    if emit_bf16:
        w_specs = [
            pl.BlockSpec((None, None, d, tf), lambda i, j: (layer, half, 0, fcol(j))),
            pl.BlockSpec((None, None, d, tf), lambda i, j: (layer, half, 0, fcol(j) + nf)),
            pl.BlockSpec((None, None, f, tn), lambda i, j: (layer, half, 0, col(j))),
        ]
        w_args = (w_in, w_in, w_out)
    else:
        w_specs = [
            pl.BlockSpec((d, tf), lambda i, j: (0, fcol(j))),
            pl.BlockSpec((d, tf), lambda i, j: (0, fcol(j))),
            pl.BlockSpec((f, tn), lambda i, j: (0, col(j))),
        ]
        w_args = (wg, wu, wo)
    out_specs = [pl.BlockSpec((tm, tn), lambda i, j: (i, col(j)))]
    out_shape = [jax.ShapeDtypeStruct((m, d), F32)]
    if emit_bf16:
        out_specs += [
            pl.BlockSpec((d, tf), lambda i, j: (0, fcol(j))),
            pl.BlockSpec((d, tf), lambda i, j: (0, fcol(j))),
            pl.BlockSpec((f, tn), lambda i, j: (0, col(j))),
        ]
        out_shape += [jax.ShapeDtypeStruct((d, f), BF), jax.ShapeDtypeStruct((d, f), BF),
                      jax.ShapeDtypeStruct((f, d), BF)]
    outs = pl.pallas_call(
        functools.partial(_ffn_kernel, nf=nf, tf=tf, emit_bf16=emit_bf16),
        grid=(m // tm, nf + d // tn),
        in_specs=[
            pl.BlockSpec((tm, d), lambda i, j: (i, 0), pipeline_mode=pl.Buffered(1)),
            pl.BlockSpec((1, d), lambda i, j: (0, 0)),
            sh_spec, sc_spec,
            *w_specs,
            pl.BlockSpec((tm, tn), lambda i, j: (i, col(j))),
            gate_spec,
        ],
        out_specs=out_specs,
        out_shape=out_shape,
        scratch_shapes=[pltpu.VMEM((tm, d), BF), pltpu.VMEM((nf, tm, tf), BF)],
        compiler_params=_params("parallel", "arbitrary"),
        name="ffn_half",
    )(x, g, shift.arr, scale.arr, *w_args, x, gate.arr)
    return (outs[0], tuple(outs[1:])) if emit_bf16 else outs[0]


def _mod_mm_kernel(x_ref, g_ref, sh_ref, sc_ref, w_ref, *o_refs, out_transposed):
    h = _modulate(x_ref[...], g_ref[...], sh_ref[...], sc_ref[...]).astype(BF)
    y = _dot_nt(w_ref[...], h) if out_transposed else _dot(h, w_ref[...])
    for o_ref in o_refs:
        o_ref[...] = y.astype(o_ref.dtype)


def _mod_mm(x, g, shift, scale, w, out_dtypes, *, seq_rows, out_transposed=False):
    m, d = x.shape
    n = w.shape[0] if out_transposed else w.shape[1]
    tm = _row_tile(m, seq_rows, (512, 256, 128))
    sh_spec = _mod_spec(shift, tm, seq_rows, d)
    sc_spec = _mod_spec(scale, tm, seq_rows, d)
    if out_transposed:
        out_spec, out_dims = pl.BlockSpec((n, tm), lambda i: (0, i)), (n, m)
    else:
        out_spec, out_dims = pl.BlockSpec((tm, n), lambda i: (i, 0)), (m, n)
    outs = pl.pallas_call(
        functools.partial(_mod_mm_kernel, out_transposed=out_transposed),
        grid=(m // tm,),
        in_specs=[
            pl.BlockSpec((tm, d), lambda i: (i, 0)),
            pl.BlockSpec((1, d), lambda i: (0, 0)),
            sh_spec, sc_spec,
            _const_spec(w),
        ],
        out_specs=[out_spec for _ in out_dtypes],
        out_shape=[jax.ShapeDtypeStruct(out_dims, dt) for dt in out_dtypes],
        compiler_params=_params("parallel"),
        name="mod_matmul",
    )(x, g, shift.arr, scale.arr, w)
    return outs


def _mm_resid_kernel(a_ref, w_ref, x_ref, gt_ref, o_ref, *, a_transposed):
    if a_transposed:
        y = lax.dot_general(a_ref[...], w_ref[...], (((0,), (0,)), ((), ())), preferred_element_type=F32)
    else:
        y = _dot(a_ref[...], w_ref[...])
    o_ref[...] = x_ref[...] + gt_ref[...] * y


def _mm_resid(a, w, x, gate, *, seq_rows, a_transposed=False):
    m, n = x.shape
    k = w.shape[0]
    tm = _row_tile(m, seq_rows, (512, 256, 128))
    return pl.pallas_call(
        functools.partial(_mm_resid_kernel, a_transposed=a_transposed),
        grid=(m // tm,),
        in_specs=[
            pl.BlockSpec((k, tm), lambda i: (0, i)) if a_transposed else pl.BlockSpec((tm, k), lambda i: (i, 0)),
            _const_spec(w),
            pl.BlockSpec((tm, n), lambda i: (i, 0)),
            _mod_spec(gate, tm, seq_rows, n),
        ],
        out_specs=pl.BlockSpec((tm, n), lambda i: (i, 0)),
        out_shape=jax.ShapeDtypeStruct((m, n), F32),
        compiler_params=_params("parallel"),
        name="matmul_residual",
    )(a, w, x, gate.arr)


def _mla_pre_kernel(x_ref, g_ref, sh_ref, sc_ref, cos_ref, sin_ref, cost_ref, sint_ref, w1_ref, qn_ref, kn_ref,
                    w2t_ref, w3_ref, row_ref, kcat_ref, latt_ref, qt_ref, *, ql, kl, nh):
    tm = x_ref.shape[0]
    cos_t = cos_ref[...]
    sin_t = sin_ref[...]
    h = _modulate(x_ref[...], g_ref[...], sh_ref[...], sc_ref[...]).astype(BF)
    p1 = _dot(h, w1_ref[...])
    cq = _rms(p1[:, :ql], qn_ref[...])
    lat = _rms(p1[:, ql:ql + kl], kn_ref[...])
    a = ql + kl
    kr = p1[:, a:a + LANES] * cos_t + p1[:, a + LANES:a + 2 * LANES] * sin_t
    row_ref[:, :kl] = lat
    row_ref[:, kl:] = kr[:, :MLA_ROPE]
    kcat_ref[:, :kl] = lat.astype(BF)
    kcat_ref[:, kl:] = kr.astype(BF)
    latt_ref[...] = lat.T.astype(BF)
    q2t = _dot(w2t_ref[...], cq.T.astype(BF))
    cos_c = cost_ref[...]
    sin_c = sint_ref[...]
    hn = nh * LANES
    for hd in range(nh):
        lo = hd * LANES
        qn_t = q2t[lo:lo + LANES].astype(BF)
        qt_ref[:kl, hd * tm:(hd + 1) * tm] = _dot(w3_ref[hd], qn_t).astype(BF)
        qr_t = q2t[hn + lo:hn + lo + LANES] * cos_c + q2t[2 * hn + lo:2 * hn + lo + LANES] * sin_c
        qt_ref[kl:, hd * tm:(hd + 1) * tm] = qr_t.astype(BF)


def _mla_pre(x, g, shift, scale, cos_t, sin_t, w1, qnorm, knorm, w2t, w3, *, seq_rows):
    m, d = x.shape
    ql = qnorm.shape[1]
    kl = knorm.shape[1]
    nh = w3.shape[0]
    c = kl + LANES
    tm = _row_tile(m, seq_rows, (256, 128))
    sh_spec = _mod_spec(shift, tm, seq_rows, d)
    sc_spec = _mod_spec(scale, tm, seq_rows, d)
    ps = _pos_spec(cos_t, tm)
    cos_c, sin_c = cos_t.T, sin_t.T
    npos = cos_t.shape[0]
    if npos == 1:
        pst = pl.BlockSpec((LANES, 1), lambda i: (0, 0))
    else:
        pst = pl.BlockSpec((LANES, tm), lambda i: (0, i % (npos // tm)))
    return pl.pallas_call(
        functools.partial(_mla_pre_kernel, ql=ql, kl=kl, nh=nh),
        grid=(m // tm,),
        in_specs=[
            pl.BlockSpec((tm, d), lambda i: (i, 0)),
            pl.BlockSpec((1, d), lambda i: (0, 0)),
            sh_spec, sc_spec, ps, ps, pst, pst,
            _const_spec(w1), _const_spec(qnorm), _const_spec(knorm), _const_spec(w2t), _const_spec(w3),
        ],
        out_specs=[
            pl.BlockSpec((tm, kl + MLA_ROPE), lambda i: (i, 0)),
            pl.BlockSpec((tm, c), lambda i: (i, 0)),
            pl.BlockSpec((kl, tm), lambda i: (0, i)),
            pl.BlockSpec((None, c, nh * tm), lambda i: (i, 0, 0)),
        ],
        out_shape=[
            jax.ShapeDtypeStruct((m, kl + MLA_ROPE), F32),
            jax.ShapeDtypeStruct((m, c), BF),
            jax.ShapeDtypeStruct((kl, m), BF),
            jax.ShapeDtypeStruct((m // tm, c, nh * tm), BF),
        ],
        compiler_params=_params("parallel"),
        name="mla_project",
    )(x, g, shift.arr, scale.arr, cos_t, sin_t, cos_c, sin_c, w1, qnorm, knorm, w2t, w3)


def _mla_flash_kernel(it_ref, jt_ref, qt_ref, k_ref, vt_ref, wuvt_ref, o_ref, m_sc, l_sc, acc_sc, *,
                      tq, tk, nh, exp2_scale):
    t = pl.program_id(1)
    i = it_ref[t]
    j = jt_ref[t]
    cols = nh * tq
    j_last = ((i + 1) * tq - 1) // tk

    @pl.when(j == 0)
    def _():
        m_sc[...] = jnp.full(m_sc.shape, NEG_INF, F32)
        l_sc[...] = jnp.zeros(l_sc.shape, F32)
        acc_sc[...] = jnp.zeros(acc_sc.shape, F32)

    def step(masked, nkeys):
        s = _dot(k_ref[:nkeys], qt_ref[...])
        if masked:
            kpos = j * tk + lax.broadcasted_iota(jnp.int32, (nkeys, cols), 0)
            qpos = i * tq + (lax.broadcasted_iota(jnp.int32, (nkeys, cols), 1) & (tq - 1))
            s = jnp.where(kpos <= qpos, s, NEG_INF)
        m_old = m_sc[...]
        m_new = jnp.maximum(m_old, jnp.max(s, axis=0, keepdims=True))
        alpha = jnp.exp2((m_old - m_new) * exp2_scale)
        p = jnp.exp2((s - m_new) * exp2_scale)
        l_sc[...] = alpha * l_sc[...] + jnp.sum(p, axis=0, keepdims=True)
        acc_sc[...] = alpha * acc_sc[...] + _dot(vt_ref[:, :nkeys], p.astype(BF))
        m_sc[...] = m_new

    needs_mask = (j + 1) * tk - 1 > i * tq
    half_block = (i + 1) * tq <= j * tk + tk // 2

    @pl.when(needs_mask & half_block)
    def _():
        step(True, tk // 2)

    @pl.when(needs_mask & jnp.logical_not(half_block))
    def _():
        step(True, tk)

    @pl.when(jnp.logical_not(needs_mask))
    def _():
        step(False, tk)

    @pl.when(j == j_last)
    def _():
        o_t = (acc_sc[...] * (1.0 / l_sc[...])).astype(BF)
        for hd in range(nh):
            o_ref[hd * MLA_V:(hd + 1) * MLA_V, :] = _dot(wuvt_ref[hd], o_t[:, hd * tq:(hd + 1) * tq]).astype(BF)


def _mla_flash(qt, kcat, latt, wuvt, *, batch, seq, scale):
    nblk, c, cols = qt.shape
    nh, _, kl = wuvt.shape
    tq = cols // nh
    m = nblk * tq
    tk = _pick(seq, (512, 256, 128))
    nq = seq // tq
    nk = seq // tk
    sched = [(i, j) for i in range(nq) for j in range(((i + 1) * tq - 1) // tk + 1)]
    it = jnp.asarray(np.array([s[0] for s in sched], np.int32))
    jt = jnp.asarray(np.array([s[1] for s in sched], np.int32))
    grid_spec = pltpu.PrefetchScalarGridSpec(
        num_scalar_prefetch=2,
        grid=(batch, len(sched)),
        in_specs=[
            pl.BlockSpec((None, c, cols), lambda b, t, it_ref, jt_ref: (b * nq + it_ref[t], 0, 0)),
            pl.BlockSpec((tk, c), lambda b, t, it_ref, jt_ref: (b * nk + jt_ref[t], 0)),
            pl.BlockSpec((kl, tk), lambda b, t, it_ref, jt_ref: (0, b * nk + jt_ref[t])),
            pl.BlockSpec(wuvt.shape, lambda b, t, it_ref, jt_ref: (0, 0, 0), pipeline_mode=pl.Buffered(1)),
        ],
        out_specs=pl.BlockSpec((nh * MLA_V, tq), lambda b, t, it_ref, jt_ref: (0, b * nq + it_ref[t])),
        scratch_shapes=[
            pltpu.VMEM((1, cols), F32),
            pltpu.VMEM((1, cols), F32),
            pltpu.VMEM((kl, cols), F32),
        ],
    )
    return pl.pallas_call(
        functools.partial(_mla_flash_kernel, tq=tq, tk=tk, nh=nh, exp2_scale=scale * math.log2(math.e)),
        grid_spec=grid_spec,
        out_shape=jax.ShapeDtypeStruct((nh * MLA_V, m), BF),
        compiler_params=_params("parallel", "arbitrary"),
        name="mla_prompt_attention",
    )(it, jt, qt, kcat, latt, wuvt)


def _mla_decode_kernel(pt_ref, q_ref, new_ref, *rest, npg, kl, scale):
    page_refs = rest[:npg]
    o_ref = rest[npg]
    kbuf, m_sc, l_sc, acc_sc = rest[npg + 1:]
    c = pl.program_id(1)

    @pl.when(c == 0)
    def _():
        m_sc[...] = jnp.full(m_sc.shape, NEG_INF, F32)
        l_sc[...] = jnp.zeros(l_sc.shape, F32)
        acc_sc[...] = jnp.zeros(acc_sc.shape, F32)

    for k, ref in enumerate(page_refs):
        kbuf[:, k * PAGE_SIZE:(k + 1) * PAGE_SIZE] = ref[...].astype(BF)

    q = q_ref[...]
    keys_t = kbuf[...]
    s = (_dot(q[:, :kl], keys_t[:kl]) + _dot(q[:, kl:kl + MLA_ROPE], keys_t[kl:])) * scale
    m_old = m_sc[...]
    m_new = jnp.maximum(m_old, jnp.max(s, axis=-1, keepdims=True))
    alpha = jnp.exp(m_old - m_new)
    p = jnp.exp(s - m_new)
    l_sc[...] = alpha * l_sc[...] + jnp.sum(p, axis=-1, keepdims=True)
    acc_sc[...] = alpha * acc_sc[...] + _dot_nt(p.astype(BF), keys_t[:kl])
    m_sc[...] = m_new

    @pl.when(c == pl.num_programs(1) - 1)
    def _():
        new = new_ref[...].astype(F32)
        s_new = jnp.sum(q.astype(F32) * new, axis=-1, keepdims=True) * scale
        m_old = m_sc[...]
        m_new = jnp.maximum(m_old, s_new)
        alpha = jnp.exp(m_old - m_new)
        p_new = jnp.exp(s_new - m_new)
        l_new = alpha * l_sc[...] + p_new
        acc = alpha * acc_sc[...] + p_new.astype(BF).astype(F32) * new[:, :kl]
        o_ref[...] = (acc * (1.0 / l_new)).astype(BF)


def _mla_decode(page_table, qcat, kcat, cache_t, *, layer, scale):
    nb, nh, c = qcat.shape
    lat = cache_t.shape[2]
    kl = lat - MLA_ROPE
    n_pages = page_table.shape[1]
    npg = _pick(n_pages, (64, 32, 16, 8, 4, 2, 1))
    pt = page_table.reshape(-1)

    def page_spec(k):
        return pl.BlockSpec((None, None, lat, PAGE_SIZE),
                            lambda b, ch, pt_ref: (pt_ref[b * n_pages + ch * npg + k], layer, 0, 0))

    grid_spec = pltpu.PrefetchScalarGridSpec(
        num_scalar_prefetch=1,
        grid=(nb, n_pages // npg),
        in_specs=[
            pl.BlockSpec((None, nh, c), lambda b, ch, pt_ref: (b, 0, 0)),
            pl.BlockSpec((None, 1, c), lambda b, ch, pt_ref: (b, 0, 0)),
        ] + [page_spec(k) for k in range(npg)],
        out_specs=pl.BlockSpec((None, nh, kl), lambda b, ch, pt_ref: (b, 0, 0)),
        scratch_shapes=[
            pltpu.VMEM((lat, npg * PAGE_SIZE), BF),
            pltpu.VMEM((nh, 1), F32),
            pltpu.VMEM((nh, 1), F32),
            pltpu.VMEM((nh, kl), F32),
        ],
    )
    return pl.pallas_call(
        functools.partial(_mla_decode_kernel, npg=npg, kl=kl, scale=scale),
        grid_spec=grid_spec,
        out_shape=jax.ShapeDtypeStruct((nb, nh, kl), BF),
        compiler_params=_params("parallel", "arbitrary"),
        name="mla_sample_attention",
    )(pt, qcat, kcat, *([cache_t] * npg))


def _uv_kernel(o_ref, wuv_ref, out_ref):
    nh = o_ref.shape[0]
    for hd in range(nh):
        out_ref[:, hd * MLA_V:(hd + 1) * MLA_V] = _dot(o_ref[hd], wuv_ref[hd]).astype(BF)


def _uv(o_lat, wuv):
    nh, m, kl = o_lat.shape
    return pl.pallas_call(
        _uv_kernel,
        grid=(1,),
        in_specs=[
            pl.BlockSpec(o_lat.shape, lambda i: (0, 0, 0)),
            pl.BlockSpec(wuv.shape, lambda i: (0, 0, 0)),
        ],
        out_specs=pl.BlockSpec((m, nh * MLA_V), lambda i: (0, 0)),
        out_shape=jax.ShapeDtypeStruct((m, nh * MLA_V), BF),
        compiler_params=_params("arbitrary"),
        name="mla_value_up",
    )(o_lat, wuv)


def _swa_prompt_kernel(qt_ref, kvp_ref, kvc_ref, bias_ref, mask_ref, sink_ref, o_ref, *, n_kv, scale):
    kvw = n_kv * LANES
    pairs_per_kv = SWA_GROUP // 2
    mask = mask_ref[...] > 0.5
    nkeys = 2 * BLOCK
    for c in range(n_kv):
        def sect(s):
            lo = s * kvw + c * LANES
            return jnp.concatenate([kvp_ref[:, lo:lo + LANES], kvc_ref[:, lo:lo + LANES]], axis=0)
        k2 = jnp.concatenate([sect(0), sect(1)], axis=0)
        v2 = jnp.concatenate([sect(2), sect(3)], axis=0)
        pr0 = c * pairs_per_kv
        qc = jnp.concatenate([qt_ref[(pr0 + pp) * LANES:(pr0 + pp + 1) * LANES, :] for pp in range(pairs_per_kv)],
                             axis=1)
        s = _dot(k2, qc) * scale + bias_ref[c]
        s = jnp.where(mask, s, NEG_INF)
        halves = []
        for hf in range(2):
            sh = s[hf * nkeys:(hf + 1) * nkeys]
            sink = sink_ref[c, hf]
            m = jnp.maximum(jnp.max(sh, axis=0, keepdims=True), sink)
            e = jnp.exp(sh - m)
            den = jnp.sum(e, axis=0, keepdims=True) + jnp.exp(sink - m)
            halves.append((e * (1.0 / den)).astype(BF))
        p = jnp.concatenate(halves, axis=0)
        o_t = lax.dot_general(v2, p, (((0,), (0,)), ((), ())), preferred_element_type=F32).astype(BF)
        for pp in range(pairs_per_kv):
            o_ref[(pr0 + pp) * LANES:(pr0 + pp + 1) * LANES, :] = o_t[:, pp * BLOCK:(pp + 1) * BLOCK]


def _swa_prompt(qt, kv4, bias_t, mask_t, sink_cols, *, batch, seq, n_kv, scale):
    dq, m = qt.shape
    nb = seq // BLOCK
    w4 = kv4.shape[1]
    return pl.pallas_call(
        functools.partial(_swa_prompt_kernel, n_kv=n_kv, scale=scale),
        grid=(batch, nb),
        in_specs=[
            pl.BlockSpec((dq, BLOCK), lambda b, j: (0, b * nb + j)),
            pl.BlockSpec((BLOCK, w4), lambda b, j: (b * nb + jnp.maximum(j - 1, 0), 0)),
            pl.BlockSpec((BLOCK, w4), lambda b, j: (b * nb + j, 0)),
            _const_spec(bias_t),
            pl.BlockSpec((None,) + mask_t.shape[1:], lambda b, j: (jnp.minimum(j, 1), 0, 0)),
            _const_spec(sink_cols),
        ],
        out_specs=pl.BlockSpec((dq, BLOCK), lambda b, j: (0, b * nb + j)),
        out_shape=jax.ShapeDtypeStruct((dq, m), BF),
        compiler_params=_params("parallel", "arbitrary"),
        name="swa_prompt_attention",
    )(qt, kv4, kv4, bias_t, mask_t, sink_cols)


def _swa_sample_kernel(q_ref, k_ref, v_ref, bias_ref, sink_ref, o_ref, *, n_kv, n_keys, scale):
    for c in range(n_kv):
        q = q_ref[:, c * SWA_GROUP:(c + 1) * SWA_GROUP, :]
        k = k_ref[:, c]
        v = v_ref[:, c]
        s = jnp.einsum("bgd,bkd->bgk", q, k, preferred_element_type=F32) * scale + bias_ref[c][None]
        col = lax.broadcasted_iota(jnp.int32, s.shape, 2)
        s = jnp.where(col < n_keys, s, NEG_INF)
        sink = sink_ref[c][None]
        m = jnp.maximum(jnp.max(s, axis=-1, keepdims=True), sink)
        e = jnp.exp(s - m)
        den = jnp.sum(e, axis=-1, keepdims=True) + jnp.exp(sink - m)
        p = (e * (1.0 / den)).astype(BF)
        o = jnp.einsum("bgk,bkd->bgd", p, v, preferred_element_type=F32)
        o_ref[:, c * SWA_GROUP:(c + 1) * SWA_GROUP, :] = o.astype(BF)


def _swa_sample(q, keys, vals, bias, sinks, *, n_keys, scale):
    nb, nh, dh = q.shape
    n_kv, kp = keys.shape[1], keys.shape[2]
    bb = _pick(nb, (16, 8))
    return pl.pallas_call(
        functools.partial(_swa_sample_kernel, n_kv=n_kv, n_keys=n_keys, scale=scale),
        grid=(nb // bb,),
        in_specs=[
            pl.BlockSpec((bb, nh, dh), lambda i: (i, 0, 0)),
            pl.BlockSpec((bb, n_kv, kp, dh), lambda i: (i, 0, 0, 0)),
            pl.BlockSpec((bb, n_kv, kp, dh), lambda i: (i, 0, 0, 0)),
            pl.BlockSpec(bias.shape, lambda i: (0, 0, 0)),
            pl.BlockSpec(sinks.shape, lambda i: (0, 0, 0)),
        ],
        out_specs=pl.BlockSpec((bb, nh, dh), lambda i: (i, 0, 0)),
        out_shape=jax.ShapeDtypeStruct((nb, nh, dh), BF),
        compiler_params=_params("parallel"),
        name="swa_sample_attention",
    )(q, keys, vals, bias, sinks)


def _final_norm_kernel(x_ref, g_ref, o_ref):
    o_ref[...] = _rms(x_ref[...], g_ref[...])


def _final_norm(x, g):
    m, d = x.shape
    tm = _pick(m, (512, 256, 128))
    return pl.pallas_call(
        _final_norm_kernel,
        grid=(m // tm,),
        in_specs=[pl.BlockSpec((tm, d), lambda i: (i, 0)), pl.BlockSpec((1, d), lambda i: (0, 0))],
        out_specs=pl.BlockSpec((tm, d), lambda i: (i, 0)),
        out_shape=jax.ShapeDtypeStruct((m, d), F32),
        compiler_params=_params("parallel"),
        name="final_norm",
    )(x, g)


def _rope_tables(pos):
    half = MLA_ROPE // 2
    inv = ROPE_THETA ** (-jnp.arange(half, dtype=F32) / half)
    ang = pos.astype(F32)[:, None] * inv
    cos, sin = jnp.cos(ang), jnp.sin(ang)
    pad = jnp.zeros((pos.shape[0], LANES - MLA_ROPE), F32)
    return jnp.concatenate([cos, cos, pad], axis=1), jnp.concatenate([-sin, sin, pad], axis=1)


def _swap_halves(w):
    half = w.shape[-1] // 2
    return jnp.concatenate([w[..., half:], w[..., :half]], axis=-1)


def _pad_lanes(w):
    return jnp.pad(w, [(0, 0)] * (w.ndim - 1) + [(0, LANES - w.shape[-1])])


def _t5_bucket(dist):
    max_exact = NUM_BUCKETS // 2
    n = jnp.maximum(dist, 0)
    nf = jnp.maximum(n, 1).astype(F32)
    large = max_exact + (jnp.log(nf / max_exact) / math.log(MAX_DISTANCE / max_exact)
                         * (NUM_BUCKETS - max_exact)).astype(jnp.int32)
    large = jnp.minimum(large, NUM_BUCKETS - 1)
    return jnp.where(n < max_exact, n, large)


def kernel(x_prompt, x_sample, c_prompt, c_sample, cache_mla, state_swa_k, state_swa_v, page_table, ada_w, ada_b, norm_g, ffn_w_in, ffn_w_out, mla_w_dq, mla_q_norm, mla_w_uq, mla_w_dkv, mla_kv_norm, mla_w_uk, mla_w_uv, mla_w_o, kv_ada_w, kv_ada_b, kv_norm_g, kv_w, swa_w_q, swa_w_o, swa_sinks, rel_bias_table, final_norm_g):
    batch, seq, d = x_prompt.shape
    dec_batch, dec_seq, _ = x_sample.shape
    assert dec_seq == 1
    depth = ada_w.shape[0]
    n_a = mla_w_dq.shape[0]
    n_pages = page_table.shape[1]
    past_len = n_pages * PAGE_SIZE
    ql = mla_w_dq.shape[2]
    kl = mla_kv_norm.shape[1]
    nh = mla_w_uk.shape[2]
    swa_heads = swa_w_q.shape[2] // SWA_HEAD_DIM
    n_kv = swa_heads // SWA_GROUP
    win_s = state_swa_k.shape[1]
    assert win_s == WINDOW and seq % BLOCK == 0
    mla_scale = float((MLA_NOPE + MLA_ROPE) ** -0.5)
    swa_scale = float(SWA_HEAD_DIM ** -0.5)
    mp = batch * seq

    c_all = jnp.concatenate([c_prompt, c_sample], axis=0)
    n_seq = c_all.shape[0]
    c_all = jnp.pad(c_all, ((0, (-n_seq) % 16), (0, 0)))
    mod = _ada(c_all, ada_w, ada_b[:, None, :])
    kvmod = _ada(c_all, kv_ada_w[None], kv_ada_b[None, None, :])[0]

    mod_p4 = mod[:, :batch, None, :]
    mod_s3 = mod[:, batch:batch + dec_batch]
    kvmod_p4 = kvmod[None, :batch, None, :]
    kvmod_s3 = kvmod[None, batch:batch + dec_batch]

    cos_p, sin_p = _rope_tables(jnp.arange(seq, dtype=jnp.int32))
    cos_s, sin_s = _rope_tables(jnp.full((1,), past_len, jnp.int32))

    qi = jnp.arange(BLOCK, dtype=jnp.int32)
    km = jnp.arange(2 * BLOCK, dtype=jnp.int32) - BLOCK
    d_p = qi[:, None] - km[None, :]
    win_mask = (d_p >= 0) & (d_p <= WINDOW)
    onehot_p = (_t5_bucket(d_p)[:, :, None] == jnp.arange(NUM_BUCKETS, dtype=jnp.int32)).astype(F32)
    bias_p = jnp.einsum("qkb,bh->hqk", onehot_p, rel_bias_table.astype(F32),
                        precision=lax.Precision.HIGHEST)
    bias_pairs = bias_p.reshape(swa_heads // 2, 2, BLOCK, 2 * BLOCK).transpose(0, 2, 1, 3)
    pairs_per_kv = SWA_GROUP // 2
    bias_t = jnp.transpose(bias_pairs.reshape(n_kv, pairs_per_kv * BLOCK, 4 * BLOCK), (0, 2, 1))
    mask_first = win_mask & (km >= 0)[None, :]
    mask_t = jnp.stack([jnp.tile(mask_first, (pairs_per_kv, 2)).T,
                        jnp.tile(win_mask, (pairs_per_kv, 2)).T]).astype(F32)
    kp_pad = 2 * WINDOW
    n_keys_s = win_s + 1
    d_s = jnp.concatenate([jnp.arange(win_s, 0, -1, dtype=jnp.int32), jnp.zeros((1,), jnp.int32)])
    bias_s = jnp.transpose(rel_bias_table[_t5_bucket(d_s)], (1, 0)).astype(F32)
    bias_s = jnp.pad(bias_s, ((0, 0), (0, kp_pad - n_keys_s))).reshape(n_kv, SWA_GROUP, kp_pad)

    cache_t = jnp.transpose(cache_mla, (0, 1, 3, 2))

    x_p = x_prompt.reshape(mp, d)
    x_s = x_sample.reshape(dec_batch, d)
    rows_p, rows_s = [], []
    kv4_p = keys_s = vals_s = keys_s_bf = vals_s_bf = None
    k_p = v_p = None

    for l in range(depth):
        if l == n_a:
            kv_k = kv_w[:, :n_kv * SWA_HEAD_DIM].reshape(d, n_kv, SWA_HEAD_DIM)
            kv_v = kv_w[:, n_kv * SWA_HEAD_DIM:].reshape(d, n_kv, SWA_HEAD_DIM)
            zero = jnp.zeros_like(kv_k)
            lo = lambda w: jnp.concatenate([w, zero], axis=-1).reshape(d, n_kv * LANES)
            hi = lambda w: jnp.concatenate([zero, w], axis=-1).reshape(d, n_kv * LANES)
            nkv = 2 * n_kv * SWA_HEAD_DIM
            w_sh = jnp.concatenate([kv_w, lo(kv_k), hi(kv_k), lo(kv_v), hi(kv_v)], axis=1).astype(BF)
            kvg = kv_norm_g[None]
            kv_f32_p, kv_bf_p = _mod_mm(x_p, kvg, _Mod(kvmod_p4, 0, 0), _Mod(kvmod_p4, 0, 1), w_sh, (F32, BF),
                                        seq_rows=seq)
            k_p = kv_f32_p[:, :nkv // 2].reshape(batch, seq, n_kv, SWA_HEAD_DIM)
            v_p = kv_f32_p[:, nkv // 2:nkv].reshape(batch, seq, n_kv, SWA_HEAD_DIM)
            kv4_p = kv_bf_p[:, nkv:]
            (kv_f32_s,) = _mod_mm(x_s, kvg, _Mod(kvmod_s3, 0, 0), _Mod(kvmod_s3, 0, 1), w_sh[:, :nkv], (F32,),
                                  seq_rows=1)
            k_new = kv_f32_s[:, :nkv // 2].reshape(dec_batch, 1, n_kv, SWA_HEAD_DIM)
            v_new = kv_f32_s[:, nkv // 2:].reshape(dec_batch, 1, n_kv, SWA_HEAD_DIM)
            keys_s = jnp.concatenate([state_swa_k, k_new], axis=1)
            vals_s = jnp.concatenate([state_swa_v, v_new], axis=1)
            padk = lambda t: jnp.pad(jnp.transpose(t, (0, 2, 1, 3)).astype(BF),
                                     ((0, 0), (0, 0), (0, kp_pad - n_keys_s), (0, 0)))
            keys_s_bf, vals_s_bf = padk(keys_s), padk(vals_s)

        g = norm_g[l]
        mp_l = [_Mod(mod_p4, l, n) for n in range(3 * N_SUB)]
        ms_l = [_Mod(mod_s3, l, n) for n in range(3 * N_SUB)]

        x_s, w_bf = _ffn(x_s, g[0][None], ms_l[0], ms_l[1], ms_l[2], (ffn_w_in, ffn_w_out, l, 0), seq_rows=1)
        x_p = _ffn(x_p, g[0][None], mp_l[0], mp_l[1], mp_l[2], w_bf, seq_rows=seq)

        if l < n_a:
            a = l
            dkv = mla_w_dkv[a]
            w1 = jnp.concatenate([mla_w_dq[a], dkv[:, :kl], _pad_lanes(dkv[:, kl:]),
                                  _pad_lanes(_swap_halves(dkv[:, kl:]))], axis=1).astype(BF)
            wuq = mla_w_uq[a].reshape(ql, nh, MLA_NOPE + MLA_ROPE)
            wuq_r = wuq[:, :, MLA_NOPE:]
            w2t = jnp.concatenate([wuq[:, :, :MLA_NOPE].reshape(ql, nh * MLA_NOPE),
                                   _pad_lanes(wuq_r).reshape(ql, nh * LANES),
                                   _pad_lanes(_swap_halves(wuq_r)).reshape(ql, nh * LANES)], axis=1).T.astype(BF)
            w3 = jnp.transpose(mla_w_uk[a], (1, 0, 2)).astype(BF)
            wuv = jnp.transpose(mla_w_uv[a], (1, 0, 2)).astype(BF)
            wuvt = jnp.transpose(mla_w_uv[a], (1, 2, 0)).astype(BF)
            w_o = mla_w_o[a].astype(BF)
            qn, kn = mla_q_norm[a][None], mla_kv_norm[a][None]

            row_p, kcat_p, latt_p, qt_p = _mla_pre(x_p, g[1][None], mp_l[3], mp_l[4], cos_p, sin_p,
                                                   w1, qn, kn, w2t, w3, seq_rows=seq)
            oh_t_p = _mla_flash(qt_p, kcat_p, latt_p, wuvt, batch=batch, seq=seq, scale=mla_scale)
            x_p = _mm_resid(oh_t_p, w_o, x_p, mp_l[5], seq_rows=seq, a_transposed=True)

            row_s, kcat_s, _, qt_s = _mla_pre(x_s, g[1][None], ms_l[3], ms_l[4], cos_s, sin_s,
                                              w1, qn, kn, w2t, w3, seq_rows=1)
            q_s = jnp.transpose(qt_s.reshape(qt_s.shape[0], kl + LANES, nh, -1), (0, 3, 2, 1))
            q_s = q_s.reshape(dec_batch, nh, kl + LANES)
            o_lat_s = _mla_decode(page_table, q_s, kcat_s[:, None, :],
                                  cache_t, layer=a, scale=mla_scale)
            oh_s = _uv(jnp.transpose(o_lat_s, (1, 0, 2)), wuv)
            x_s = _mm_resid(oh_s, w_o, x_s, ms_l[5], seq_rows=1)
            rows_p.append(row_p.reshape(batch, seq, kl + MLA_ROPE))
            rows_s.append(row_s.reshape(dec_batch, 1, kl + MLA_ROPE))
        else:
            b = l - n_a
            w_q = swa_w_q[b].astype(BF)
            w_o = swa_w_o[b].astype(BF)
            (qt_p,) = _mod_mm(x_p, g[1][None], mp_l[3], mp_l[4], swa_w_q[b].T.astype(BF), (BF,), seq_rows=seq,
                              out_transposed=True)
            sink_cols = jnp.repeat(swa_sinks[b].reshape(n_kv, pairs_per_kv, 2).transpose(0, 2, 1), BLOCK,
                                   axis=2)[:, :, None, :]
            ot_p = _swa_prompt(qt_p, kv4_p, bias_t, mask_t, sink_cols, batch=batch, seq=seq, n_kv=n_kv,
                               scale=swa_scale)
            x_p = _mm_resid(ot_p, w_o, x_p, mp_l[5], seq_rows=seq, a_transposed=True)

            (q_s,) = _mod_mm(x_s, g[1][None], ms_l[3], ms_l[4], w_q, (BF,), seq_rows=1)
            o_s = _swa_sample(q_s.reshape(dec_batch, swa_heads, SWA_HEAD_DIM), keys_s_bf, vals_s_bf, bias_s,
                              swa_sinks[b].reshape(n_kv, SWA_GROUP, 1), n_keys=n_keys_s, scale=swa_scale)
            x_s = _mm_resid(o_s.reshape(dec_batch, swa_heads * SWA_HEAD_DIM), w_o, x_s, ms_l[5], seq_rows=1)

        x_s, w_bf = _ffn(x_s, g[2][None], ms_l[6], ms_l[7], ms_l[8], (ffn_w_in, ffn_w_out, l, 1), seq_rows=1)
        x_p = _ffn(x_p, g[2][None], mp_l[6], mp_l[7], mp_l[8], w_bf, seq_rows=seq)

    y_prompt = _final_norm(x_p, final_norm_g[None]).reshape(batch, seq, d)
    y_sample = _final_norm(x_s, final_norm_g[None]).reshape(dec_batch, 1, d)
    new_mla_prompt = jnp.stack(rows_p, axis=1)
    new_mla_sample = jnp.stack(rows_s, axis=1)
    win_p = min(WINDOW, seq)
    return (y_prompt, y_sample, new_mla_prompt, new_mla_sample,
            k_p[:, seq - win_p:], v_p[:, seq - win_p:],
            keys_s[:, keys_s.shape[1] - win_s:], vals_s[:, vals_s.shape[1] - win_s:])
```

```python
import functools
import math
from typing import NamedTuple

import jax
import jax.numpy as jnp
import numpy as np
from jax import lax
from jax.experimental import pallas as pl
from jax.experimental.pallas import tpu as pltpu

EPS = 1e-6
NEG_INF = -1e30
ROPE_THETA = 10000.0
PAGE_SIZE = 128
WINDOW = 128
BLOCK = 128
NUM_BUCKETS = 32
MAX_DISTANCE = 128
MLA_NOPE = 128
MLA_ROPE = 64
MLA_V = 128
SWA_HEAD_DIM = 64
SWA_GROUP = 8
N_SUB = 3

LANES = 128
V7X_VMEM_LIMIT_BYTES = 56 * 1024 * 1024

BF = jnp.bfloat16
F32 = jnp.float32


def _params(*sem):
    return pltpu.CompilerParams(dimension_semantics=sem, vmem_limit_bytes=V7X_VMEM_LIMIT_BYTES)


def _pick(n, prefs):
    for p in prefs:
        if p <= n and n % p == 0:
            return p
    return n


def _row_tile(m, seq_rows, prefs):
    return _pick(m if seq_rows == 1 else seq_rows, prefs)


def _const_spec(arr):
    nd = arr.ndim
    return pl.BlockSpec(arr.shape, lambda *_: (0,) * nd, pipeline_mode=pl.Buffered(1))


def _silu(x):
    return x * (1.0 / (1.0 + jnp.exp(-x)))


def _rms(x, g):
    return x * lax.rsqrt(jnp.mean(x * x, axis=-1, keepdims=True) + EPS) * g


def _modulate(x, g, shift, scale):
    return _rms(x, g) * (1.0 + scale) + shift


def _dot(a, b):
    return jnp.dot(a, b, preferred_element_type=F32)


def _dot_nt(a, b):
    return lax.dot_general(a, b, (((1,), (1,)), ((), ())), preferred_element_type=F32)


class _Mod(NamedTuple):
    arr: jax.Array
    layer: int
    n: int


def _mod_spec(mod, tm, seq_rows, d, tn=None, col=None):
    tn = d if tn is None else tn
    base = mod.n * (d // tn)
    cidx = (lambda rest: base + col(rest[0])) if col is not None else (lambda rest: base)
    if mod.arr.ndim == 4:
        return pl.BlockSpec((None, None, 1, tn), lambda i, *rest: (mod.layer, (i * tm) // seq_rows, 0, cidx(rest)))
    return pl.BlockSpec((None, tm, tn), lambda i, *rest: (mod.layer, i, cidx(rest)))


def _pos_spec(arr, tm):
    p = arr.shape[0]
    if p == 1:
        return pl.BlockSpec((1, LANES), lambda i, *_: (0, 0))
    nblk = p // tm
    return pl.BlockSpec((tm, LANES), lambda i, *_: (i % nblk, 0))


def _ada_kernel(c_ref, w_ref, b_ref, o_ref):
    a = _silu(c_ref[...]).astype(BF)
    o_ref[...] = _dot(a, w_ref[...].astype(BF)) + b_ref[...]


def _ada(c, w, b):
    nl, d, n = w.shape
    m = c.shape[0]
    tn = _pick(n, (1024, 512, 256, 128))
    return pl.pallas_call(
        _ada_kernel,
        grid=(nl, n // tn),
        in_specs=[
            pl.BlockSpec((m, d), lambda l, j: (0, 0)),
            pl.BlockSpec((None, d, tn), lambda l, j: (l, 0, j)),
            pl.BlockSpec((None, 1, tn), lambda l, j: (l, 0, j)),
        ],
        out_specs=pl.BlockSpec((None, m, tn), lambda l, j: (l, 0, j)),
        out_shape=jax.ShapeDtypeStruct((nl, m, n), F32),
        compiler_params=_params("parallel", "parallel"),
        name="ada_params",
    )(c, w, b)


def _ffn_kernel(x_ref, g_ref, sh_ref, sc_ref, wg_ref, wu_ref, wo_ref, xc_ref, gt_ref, o_ref, *rest, nf, tf,
                emit_bf16):
    if emit_bf16:
        wg_out, wu_out, wo_out, h_ref, a_ref = rest
    else:
        h_ref, a_ref = rest
    j = pl.program_id(1)

    @pl.when(j == 0)
    def _():
        h_ref[...] = _modulate(x_ref[...], g_ref[...], sh_ref[...], sc_ref[...]).astype(BF)

    @pl.when(j < nf)
    def _():
        h = h_ref[...]
        wg = wg_ref[...].astype(BF)
        wu = wu_ref[...].astype(BF)
        if emit_bf16:
            wg_out[...] = wg
            wu_out[...] = wu
        gate = _dot(h, wg)
        up = _dot(h, wu)
        a_ref[j] = (_silu(gate) * up).astype(BF)

    @pl.when(j >= nf)
    def _():
        if emit_bf16:
            wo_out[...] = wo_ref[...].astype(BF)
            wo = wo_out
        else:
            wo = wo_ref
        y = _dot(a_ref[0], wo[0:tf, :])
        for c in range(1, nf):
            y = y + _dot(a_ref[c], wo[c * tf:(c + 1) * tf, :])
        o_ref[...] = xc_ref[...] + 0.5 * gt_ref[...] * y


def _ffn(x, g, shift, scale, gate, weights, *, seq_rows):
    emit_bf16 = len(weights) == 4
    m, d = x.shape
    tm = _row_tile(m, seq_rows, (1024, 512, 256, 128))
    if emit_bf16:
        w_in, w_out, layer, half = weights
        f = w_out.shape[2]
        assert m == tm, "the bf16 weight copies are written by a single row tile"
    else:
        wg, wu, wo = weights
        f = wo.shape[0]
    tf = _pick(f, (512, 256, 128))
    tn = _pick(d, (256, 128))
    nf = f // tf
    sh_spec = _mod_spec(shift, tm, seq_rows, d)
    sc_spec = _mod_spec(scale, tm, seq_rows, d)
    col = lambda j: jnp.maximum(j - nf, 0)
    fcol = lambda j: jnp.minimum(j, nf - 1)
    gate_spec = _mod_spec(gate, tm, seq_rows, d, tn=tn, col=col)
    if emit_bf16:
        w_specs = [
            pl.BlockSpec((None, None, d, tf), lambda i, j: (layer, half, 0, fcol(j))),
            pl.BlockSpec((None, None, d, tf), lambda i, j: (layer, half, 0, fcol(j) + nf)),
            pl.BlockSpec((None, None, f, tn), lambda i, j: (layer, half, 0, col(j))),
        ]
        w_args = (w_in, w_in, w_out)
    else:
        w_specs = [
            pl.BlockSpec((d, tf), lambda i, j: (0, fcol(j))),
            pl.BlockSpec((d, tf), lambda i, j: (0, fcol(j))),
            pl.BlockSpec((f, tn), lambda i, j: (0, col(j))),
        ]
        w_args = (wg, wu, wo)
    out_specs = [pl.BlockSpec((tm, tn), lambda i, j: (i, col(j)))]
    out_shape = [jax.ShapeDtypeStruct((m, d), F32)]
    if emit_bf16:
        out_specs += [
            pl.BlockSpec((d, tf), lambda i, j: (0, fcol(j))),
            pl.BlockSpec((d, tf), lambda i, j: (0, fcol(j))),
            pl.BlockSpec((f, tn), lambda i, j: (0, col(j))),
        ]
        out_shape += [jax.ShapeDtypeStruct((d, f), BF), jax.ShapeDtypeStruct((d, f), BF),
                      jax.ShapeDtypeStruct((f, d), BF)]
    outs = pl.pallas_call(
        functools.partial(_ffn_kernel, nf=nf, tf=tf, emit_bf16=emit_bf16),
        grid=(m // tm, nf + d // tn),
        in_specs=[
            pl.BlockSpec((tm, d), lambda i, j: (i, 0), pipeline_mode=pl.Buffered(1)),
            pl.BlockSpec((1, d), lambda i, j: (0, 0)),
            sh_spec, sc_spec,
            *w_specs,
            pl.BlockSpec((tm, tn), lambda i, j: (i, col(j))),
            gate_spec,
        ],
        out_specs=out_specs,
        out_shape=out_shape,
        scratch_shapes=[pltpu.VMEM((tm, d), BF), pltpu.VMEM((nf, tm, tf), BF)],
        compiler_params=_params("parallel", "arbitrary"),
        name="ffn_half",
    )(x, g, shift.arr, scale.arr, *w_args, x, gate.arr)
    return (outs[0], tuple(outs[1:])) if emit_bf16 else outs[0]


def _mod_mm_kernel(x_ref, g_ref, sh_ref, sc_ref, w_ref, *o_refs, out_transposed, ranges):
    h = _modulate(x_ref[...], g_ref[...], sh_ref[...], sc_ref[...]).astype(BF)
    y = _dot_nt(w_ref[...], h) if out_transposed else _dot(h, w_ref[...])
    for o_ref, (lo, hi) in zip(o_refs, ranges):
        o_ref[...] = (y[lo:hi] if out_transposed else y[:, lo:hi]).astype(o_ref.dtype)


def _mod_mm(x, g, shift, scale, w, outs, *, seq_rows, out_transposed=False):
    m, d = x.shape
    tm = _row_tile(m, seq_rows, (512, 256, 128))
    sh_spec = _mod_spec(shift, tm, seq_rows, d)
    sc_spec = _mod_spec(scale, tm, seq_rows, d)
    if out_transposed:
        out_specs = [pl.BlockSpec((hi - lo, tm), lambda i: (0, i)) for _, lo, hi in outs]
        out_shape = [jax.ShapeDtypeStruct((hi - lo, m), dt) for dt, lo, hi in outs]
    else:
        out_specs = [pl.BlockSpec((tm, hi - lo), lambda i: (i, 0)) for _, lo, hi in outs]
        out_shape = [jax.ShapeDtypeStruct((m, hi - lo), dt) for dt, lo, hi in outs]
    return pl.pallas_call(
        functools.partial(_mod_mm_kernel, out_transposed=out_transposed, ranges=[(lo, hi) for _, lo, hi in outs]),
        grid=(m // tm,),
        in_specs=[
            pl.BlockSpec((tm, d), lambda i: (i, 0)),
            pl.BlockSpec((1, d), lambda i: (0, 0)),
            sh_spec, sc_spec,
            _const_spec(w),
        ],
        out_specs=out_specs,
        out_shape=out_shape,
        compiler_params=_params("parallel"),
        name="mod_matmul",
    )(x, g, shift.arr, scale.arr, w)


def _mm_resid_kernel(a_ref, w_ref, x_ref, gt_ref, o_ref, *, a_transposed):
    if a_transposed:
        y = lax.dot_general(a_ref[...], w_ref[...], (((0,), (0,)), ((), ())), preferred_element_type=F32)
    else:
        y = _dot(a_ref[...], w_ref[...])
    o_ref[...] = x_ref[...] + gt_ref[...] * y


def _mm_resid(a, w, x, gate, *, seq_rows, a_transposed=False):
    m, n = x.shape
    k = w.shape[0]
    tm = _row_tile(m, seq_rows, (512, 256, 128))
    return pl.pallas_call(
        functools.partial(_mm_resid_kernel, a_transposed=a_transposed),
        grid=(m // tm,),
        in_specs=[
            pl.BlockSpec((k, tm), lambda i: (0, i)) if a_transposed else pl.BlockSpec((tm, k), lambda i: (i, 0)),
            _const_spec(w),
            pl.BlockSpec((tm, n), lambda i: (i, 0)),
            _mod_spec(gate, tm, seq_rows, n),
        ],
        out_specs=pl.BlockSpec((tm, n), lambda i: (i, 0)),
        out_shape=jax.ShapeDtypeStruct((m, n), F32),
        compiler_params=_params("parallel"),
        name="matmul_residual",
    )(a, w, x, gate.arr)


def _mla_pre_kernel(x_ref, g_ref, sh_ref, sc_ref, cos_ref, sin_ref, cost_ref, sint_ref, w1_ref, qn_ref, kn_ref,
                    w2t_ref, w3_ref, row_ref, kcat_ref, latt_ref, qt_ref, *, ql, kl, nh):
    tm = x_ref.shape[0]
    cos_t = cos_ref[...]
    sin_t = sin_ref[...]
    h = _modulate(x_ref[...], g_ref[...], sh_ref[...], sc_ref[...]).astype(BF)
    p1 = _dot(h, w1_ref[...])
    cq = _rms(p1[:, :ql], qn_ref[...])
    lat = _rms(p1[:, ql:ql + kl], kn_ref[...])
    a = ql + kl
    kr = p1[:, a:a + LANES] * cos_t + p1[:, a + LANES:a + 2 * LANES] * sin_t
    row_ref[:, :kl] = lat
    row_ref[:, kl:] = kr[:, :MLA_ROPE]
    kcat_ref[:, :kl] = lat.astype(BF)
    kcat_ref[:, kl:] = kr.astype(BF)
    latt_ref[...] = lat.T.astype(BF)
    q2t = _dot(w2t_ref[...], cq.T.astype(BF))
    cos_c = cost_ref[...]
    sin_c = sint_ref[...]
    hn = nh * LANES
    for hd in range(nh):
        lo = hd * LANES
        qn_t = q2t[lo:lo + LANES].astype(BF)
        qt_ref[:kl, hd * tm:(hd + 1) * tm] = _dot(w3_ref[hd], qn_t).astype(BF)
        qr_t = q2t[hn + lo:hn + lo + LANES] * cos_c + q2t[2 * hn + lo:2 * hn + lo + LANES] * sin_c
        qt_ref[kl:, hd * tm:(hd + 1) * tm] = qr_t.astype(BF)


def _mla_pre(x, g, shift, scale, cos_t, sin_t, w1, qnorm, knorm, w2t, w3, *, seq_rows):
    m, d = x.shape
    ql = qnorm.shape[1]
    kl = knorm.shape[1]
    nh = w3.shape[0]
    c = kl + LANES
    tm = _row_tile(m, seq_rows, (256, 128))
    sh_spec = _mod_spec(shift, tm, seq_rows, d)
    sc_spec = _mod_spec(scale, tm, seq_rows, d)
    ps = _pos_spec(cos_t, tm)
    cos_c, sin_c = cos_t.T, sin_t.T
    npos = cos_t.shape[0]
    if npos == 1:
        pst = pl.BlockSpec((LANES, 1), lambda i: (0, 0))
    else:
        pst = pl.BlockSpec((LANES, tm), lambda i: (0, i % (npos // tm)))
    return pl.pallas_call(
        functools.partial(_mla_pre_kernel, ql=ql, kl=kl, nh=nh),
        grid=(m // tm,),
        in_specs=[
            pl.BlockSpec((tm, d), lambda i: (i, 0)),
            pl.BlockSpec((1, d), lambda i: (0, 0)),
            sh_spec, sc_spec, ps, ps, pst, pst,
            _const_spec(w1), _const_spec(qnorm), _const_spec(knorm), _const_spec(w2t), _const_spec(w3),
        ],
        out_specs=[
            pl.BlockSpec((tm, kl + MLA_ROPE), lambda i: (i, 0)),
            pl.BlockSpec((tm, c), lambda i: (i, 0)),
            pl.BlockSpec((kl, tm), lambda i: (0, i)),
            pl.BlockSpec((None, c, nh * tm), lambda i: (i, 0, 0)),
        ],
        out_shape=[
            jax.ShapeDtypeStruct((m, kl + MLA_ROPE), F32),
            jax.ShapeDtypeStruct((m, c), BF),
            jax.ShapeDtypeStruct((kl, m), BF),
            jax.ShapeDtypeStruct((m // tm, c, nh * tm), BF),
        ],
        compiler_params=_params("parallel"),
        name="mla_project",
    )(x, g, shift.arr, scale.arr, cos_t, sin_t, cos_c, sin_c, w1, qnorm, knorm, w2t, w3)


def _mla_flash_kernel(it_ref, jt_ref, qt_ref, k_ref, vt_ref, wuvt_ref, o_ref, m_sc, l_sc, acc_sc, *,
                      tq, tk, nh, exp2_scale):
    t = pl.program_id(1)
    i = it_ref[t]
    j = jt_ref[t]
    cols = nh * tq
    j_last = ((i + 1) * tq - 1) // tk

    @pl.when(j == 0)
    def _():
        m_sc[...] = jnp.full(m_sc.shape, NEG_INF, F32)
        l_sc[...] = jnp.zeros(l_sc.shape, F32)
        acc_sc[...] = jnp.zeros(acc_sc.shape, F32)

    def step(masked, nkeys):
        s = _dot(k_ref[:nkeys], qt_ref[...])
        if masked:
            kpos = j * tk + lax.broadcasted_iota(jnp.int32, (nkeys, cols), 0)
            qpos = i * tq + (lax.broadcasted_iota(jnp.int32, (nkeys, cols), 1) & (tq - 1))
            s = jnp.where(kpos <= qpos, s, NEG_INF)
        m_old = m_sc[...]
        m_new = jnp.maximum(m_old, jnp.max(s, axis=0, keepdims=True))
        alpha = jnp.exp2((m_old - m_new) * exp2_scale)
        p = jnp.exp2((s - m_new) * exp2_scale)
        l_sc[...] = alpha * l_sc[...] + jnp.sum(p, axis=0, keepdims=True)
        acc_sc[...] = alpha * acc_sc[...] + _dot(vt_ref[:, :nkeys], p.astype(BF))
        m_sc[...] = m_new

    needs_mask = (j + 1) * tk - 1 > i * tq
    half_block = (i + 1) * tq <= j * tk + tk // 2

    @pl.when(needs_mask & half_block)
    def _():
        step(True, tk // 2)

    @pl.when(needs_mask & jnp.logical_not(half_block))
    def _():
        step(True, tk)

    @pl.when(jnp.logical_not(needs_mask))
    def _():
        step(False, tk)

    @pl.when(j == j_last)
    def _():
        o_t = (acc_sc[...] * (1.0 / l_sc[...])).astype(BF)
        for hd in range(nh):
            o_ref[hd * MLA_V:(hd + 1) * MLA_V, :] = _dot(wuvt_ref[hd], o_t[:, hd * tq:(hd + 1) * tq]).astype(BF)


def _mla_flash(qt, kcat, latt, wuvt, *, batch, seq, scale):
    nblk, c, cols = qt.shape
    nh, _, kl = wuvt.shape
    tq = cols // nh
    m = nblk * tq
    tk = _pick(seq, (512, 256, 128))
    nq = seq // tq
    nk = seq // tk
    sched = [(i, j) for i in range(nq) for j in range(((i + 1) * tq - 1) // tk + 1)]
    it = jnp.asarray(np.array([s[0] for s in sched], np.int32))
    jt = jnp.asarray(np.array([s[1] for s in sched], np.int32))
    grid_spec = pltpu.PrefetchScalarGridSpec(
        num_scalar_prefetch=2,
        grid=(batch, len(sched)),
        in_specs=[
            pl.BlockSpec((None, c, cols), lambda b, t, it_ref, jt_ref: (b * nq + it_ref[t], 0, 0)),
            pl.BlockSpec((tk, c), lambda b, t, it_ref, jt_ref: (b * nk + jt_ref[t], 0)),
            pl.BlockSpec((kl, tk), lambda b, t, it_ref, jt_ref: (0, b * nk + jt_ref[t])),
            pl.BlockSpec(wuvt.shape, lambda b, t, it_ref, jt_ref: (0, 0, 0), pipeline_mode=pl.Buffered(1)),
        ],
        out_specs=pl.BlockSpec((nh * MLA_V, tq), lambda b, t, it_ref, jt_ref: (0, b * nq + it_ref[t])),
        scratch_shapes=[
            pltpu.VMEM((1, cols), F32),
            pltpu.VMEM((1, cols), F32),
            pltpu.VMEM((kl, cols), F32),
        ],
    )
    return pl.pallas_call(
        functools.partial(_mla_flash_kernel, tq=tq, tk=tk, nh=nh, exp2_scale=scale * math.log2(math.e)),
        grid_spec=grid_spec,
        out_shape=jax.ShapeDtypeStruct((nh * MLA_V, m), BF),
        compiler_params=_params("parallel", "arbitrary"),
        name="mla_prompt_attention",
    )(it, jt, qt, kcat, latt, wuvt)


def _mla_decode_kernel(pt_ref, q_ref, new_ref, *rest, npg, kl, scale):
    page_refs = rest[:npg]
    o_ref = rest[npg]
    kbuf, m_sc, l_sc, acc_sc = rest[npg + 1:]
    c = pl.program_id(1)

    @pl.when(c == 0)
    def _():
        m_sc[...] = jnp.full(m_sc.shape, NEG_INF, F32)
        l_sc[...] = jnp.zeros(l_sc.shape, F32)
        acc_sc[...] = jnp.zeros(acc_sc.shape, F32)

    for k, ref in enumerate(page_refs):
        kbuf[:, k * PAGE_SIZE:(k + 1) * PAGE_SIZE] = ref[...].astype(BF)

    q = q_ref[...]
    keys_t = kbuf[...]
    s = (_dot(q[:, :kl], keys_t[:kl]) + _dot(q[:, kl:kl + MLA_ROPE], keys_t[kl:])) * scale
    m_old = m_sc[...]
    m_new = jnp.maximum(m_old, jnp.max(s, axis=-1, keepdims=True))
    alpha = jnp.exp(m_old - m_new)
    p = jnp.exp(s - m_new)
    l_sc[...] = alpha * l_sc[...] + jnp.sum(p, axis=-1, keepdims=True)
    acc_sc[...] = alpha * acc_sc[...] + _dot_nt(p.astype(BF), keys_t[:kl])
    m_sc[...] = m_new

    @pl.when(c == pl.num_programs(1) - 1)
    def _():
        new = new_ref[...].astype(F32)
        s_new = jnp.sum(q.astype(F32) * new, axis=-1, keepdims=True) * scale
        m_old = m_sc[...]
        m_new = jnp.maximum(m_old, s_new)
        alpha = jnp.exp(m_old - m_new)
        p_new = jnp.exp(s_new - m_new)
        l_new = alpha * l_sc[...] + p_new
        acc = alpha * acc_sc[...] + p_new.astype(BF).astype(F32) * new[:, :kl]
        o_ref[...] = (acc * (1.0 / l_new)).astype(BF)


def _mla_decode(page_table, qcat, kcat, cache_t, *, layer, scale):
    nb, nh, c = qcat.shape
    lat = cache_t.shape[2]
    kl = lat - MLA_ROPE
    n_pages = page_table.shape[1]
    npg = _pick(n_pages, (64, 32, 16, 8, 4, 2, 1))
    pt = page_table.reshape(-1)

    def page_spec(k):
        return pl.BlockSpec((None, None, lat, PAGE_SIZE),
                            lambda b, ch, pt_ref: (pt_ref[b * n_pages + ch * npg + k], layer, 0, 0))

    grid_spec = pltpu.PrefetchScalarGridSpec(
        num_scalar_prefetch=1,
        grid=(nb, n_pages // npg),
        in_specs=[
            pl.BlockSpec((None, nh, c), lambda b, ch, pt_ref: (b, 0, 0)),
            pl.BlockSpec((None, 1, c), lambda b, ch, pt_ref: (b, 0, 0)),
        ] + [page_spec(k) for k in range(npg)],
        out_specs=pl.BlockSpec((None, nh, kl), lambda b, ch, pt_ref: (b, 0, 0)),
        scratch_shapes=[
            pltpu.VMEM((lat, npg * PAGE_SIZE), BF),
            pltpu.VMEM((nh, 1), F32),
            pltpu.VMEM((nh, 1), F32),
            pltpu.VMEM((nh, kl), F32),
        ],
    )
    return pl.pallas_call(
        functools.partial(_mla_decode_kernel, npg=npg, kl=kl, scale=scale),
        grid_spec=grid_spec,
        out_shape=jax.ShapeDtypeStruct((nb, nh, kl), BF),
        compiler_params=_params("parallel", "arbitrary"),
        name="mla_sample_attention",
    )(pt, qcat, kcat, *([cache_t] * npg))


def _uv_kernel(o_ref, wuv_ref, out_ref):
    nh = o_ref.shape[0]
    for hd in range(nh):
        out_ref[:, hd * MLA_V:(hd + 1) * MLA_V] = _dot(o_ref[hd], wuv_ref[hd]).astype(BF)


def _uv(o_lat, wuv):
    nh, m, kl = o_lat.shape
    return pl.pallas_call(
        _uv_kernel,
        grid=(1,),
        in_specs=[
            pl.BlockSpec(o_lat.shape, lambda i: (0, 0, 0)),
            pl.BlockSpec(wuv.shape, lambda i: (0, 0, 0)),
        ],
        out_specs=pl.BlockSpec((m, nh * MLA_V), lambda i: (0, 0)),
        out_shape=jax.ShapeDtypeStruct((m, nh * MLA_V), BF),
        compiler_params=_params("arbitrary"),
        name="mla_value_up",
    )(o_lat, wuv)


def _swa_prompt_kernel(qt_ref, kvp_ref, kvc_ref, bias_ref, mask_ref, sink_ref, o_ref, *, n_kv, scale):
    kvw = n_kv * LANES
    pairs_per_kv = SWA_GROUP // 2
    mask = mask_ref[...] > 0.5
    nkeys = 2 * BLOCK
    for c in range(n_kv):
        def sect(s):
            lo = s * kvw + c * LANES
            return jnp.concatenate([kvp_ref[:, lo:lo + LANES], kvc_ref[:, lo:lo + LANES]], axis=0)
        k2 = jnp.concatenate([sect(0), sect(1)], axis=0)
        v2 = jnp.concatenate([sect(2), sect(3)], axis=0)
        pr0 = c * pairs_per_kv
        qc = jnp.concatenate([qt_ref[(pr0 + pp) * LANES:(pr0 + pp + 1) * LANES, :] for pp in range(pairs_per_kv)],
                             axis=1)
        s = _dot(k2, qc) * scale + bias_ref[c]
        s = jnp.where(mask, s, NEG_INF)
        halves = []
        for hf in range(2):
            sh = s[hf * nkeys:(hf + 1) * nkeys]
            sink = sink_ref[c, hf]
            m = jnp.maximum(jnp.max(sh, axis=0, keepdims=True), sink)
            e = jnp.exp(sh - m)
            den = jnp.sum(e, axis=0, keepdims=True) + jnp.exp(sink - m)
            halves.append((e * (1.0 / den)).astype(BF))
        p = jnp.concatenate(halves, axis=0)
        o_t = lax.dot_general(v2, p, (((0,), (0,)), ((), ())), preferred_element_type=F32).astype(BF)
        for pp in range(pairs_per_kv):
            o_ref[(pr0 + pp) * LANES:(pr0 + pp + 1) * LANES, :] = o_t[:, pp * BLOCK:(pp + 1) * BLOCK]


def _swa_prompt(qt, kv4, bias_t, mask_t, sink_cols, *, batch, seq, n_kv, scale):
    dq, m = qt.shape
    nb = seq // BLOCK
    w4 = kv4.shape[1]
    return pl.pallas_call(
        functools.partial(_swa_prompt_kernel, n_kv=n_kv, scale=scale),
        grid=(batch, nb),
        in_specs=[
            pl.BlockSpec((dq, BLOCK), lambda b, j: (0, b * nb + j)),
            pl.BlockSpec((BLOCK, w4), lambda b, j: (b * nb + jnp.maximum(j - 1, 0), 0)),
            pl.BlockSpec((BLOCK, w4), lambda b, j: (b * nb + j, 0)),
            _const_spec(bias_t),
            pl.BlockSpec((None,) + mask_t.shape[1:], lambda b, j: (jnp.minimum(j, 1), 0, 0)),
            _const_spec(sink_cols),
        ],
        out_specs=pl.BlockSpec((dq, BLOCK), lambda b, j: (0, b * nb + j)),
        out_shape=jax.ShapeDtypeStruct((dq, m), BF),
        compiler_params=_params("parallel", "arbitrary"),
        name="swa_prompt_attention",
    )(qt, kv4, kv4, bias_t, mask_t, sink_cols)


def _swa_sample_kernel(q_ref, kt_ref, vt_ref, kn_ref, vn_ref, bias_ref, bias_new_ref, sink_ref, o_ref, *, n_kv, scale):
    for c in range(n_kv):
        q = q_ref[:, c * SWA_GROUP:(c + 1) * SWA_GROUP, :]
        kt = kt_ref[:, c].astype(BF)
        vt = vt_ref[:, c].astype(BF)
        kn = kn_ref[:, c].astype(BF).astype(F32)[:, None, :]
        vn = vn_ref[:, c].astype(BF).astype(F32)[:, None, :]
        s = jnp.einsum("bgd,bdk->bgk", q, kt, preferred_element_type=F32) * scale + bias_ref[c][None]
        s_new = jnp.sum(q.astype(F32) * kn, axis=-1, keepdims=True) * scale + bias_new_ref[c][None]
        sink = sink_ref[c][None]
        m = jnp.maximum(jnp.maximum(jnp.max(s, axis=-1, keepdims=True), s_new), sink)
        e = jnp.exp(s - m)
        e_new = jnp.exp(s_new - m)
        inv = 1.0 / (jnp.sum(e, axis=-1, keepdims=True) + e_new + jnp.exp(sink - m))
        p = (e * inv).astype(BF)
        p_new = (e_new * inv).astype(BF).astype(F32)
        o = jnp.einsum("bgk,bdk->bgd", p, vt, preferred_element_type=F32) + p_new * vn
        o_ref[:, c * SWA_GROUP:(c + 1) * SWA_GROUP, :] = o.astype(BF)


def _swa_sample(q, kt, vt, k_new, v_new, bias, bias_new, sinks, *, scale):
    nb, nh, dh = q.shape
    n_kv, _, w = kt.shape[1:]
    bb = _pick(nb, (16, 8))
    return pl.pallas_call(
        functools.partial(_swa_sample_kernel, n_kv=n_kv, scale=scale),
        grid=(nb // bb,),
        in_specs=[
            pl.BlockSpec((bb, nh, dh), lambda i: (i, 0, 0)),
            pl.BlockSpec((bb, n_kv, dh, w), lambda i: (i, 0, 0, 0)),
            pl.BlockSpec((bb, n_kv, dh, w), lambda i: (i, 0, 0, 0)),
            pl.BlockSpec((bb, n_kv, dh), lambda i: (i, 0, 0)),
            pl.BlockSpec((bb, n_kv, dh), lambda i: (i, 0, 0)),
            pl.BlockSpec(bias.shape, lambda i: (0, 0, 0)),
            pl.BlockSpec(bias_new.shape, lambda i: (0, 0, 0)),
            pl.BlockSpec(sinks.shape, lambda i: (0, 0, 0)),
        ],
        out_specs=pl.BlockSpec((bb, nh, dh), lambda i: (i, 0, 0)),
        out_shape=jax.ShapeDtypeStruct((nb, nh, dh), BF),
        compiler_params=_params("parallel"),
        name="swa_sample_attention",
    )(q, kt, vt, k_new, v_new, bias, bias_new, sinks)


def _final_norm_kernel(x_ref, g_ref, o_ref):
    o_ref[...] = _rms(x_ref[...], g_ref[...])


def _final_norm(x, g):
    m, d = x.shape
    tm = _pick(m, (512, 256, 128))
    return pl.pallas_call(
        _final_norm_kernel,
        grid=(m // tm,),
        in_specs=[pl.BlockSpec((tm, d), lambda i: (i, 0)), pl.BlockSpec((1, d), lambda i: (0, 0))],
        out_specs=pl.BlockSpec((tm, d), lambda i: (i, 0)),
        out_shape=jax.ShapeDtypeStruct((m, d), F32),
        compiler_params=_params("parallel"),
        name="final_norm",
    )(x, g)


def _rope_tables(pos):
    half = MLA_ROPE // 2
    inv = ROPE_THETA ** (-jnp.arange(half, dtype=F32) / half)
    ang = pos.astype(F32)[:, None] * inv
    cos, sin = jnp.cos(ang), jnp.sin(ang)
    pad = jnp.zeros((pos.shape[0], LANES - MLA_ROPE), F32)
    return jnp.concatenate([cos, cos, pad], axis=1), jnp.concatenate([-sin, sin, pad], axis=1)


def _swap_halves(w):
    half = w.shape[-1] // 2
    return jnp.concatenate([w[..., half:], w[..., :half]], axis=-1)


def _pad_lanes(w):
    return jnp.pad(w, [(0, 0)] * (w.ndim - 1) + [(0, LANES - w.shape[-1])])


def _t5_bucket(dist):
    max_exact = NUM_BUCKETS // 2
    n = jnp.maximum(dist, 0)
    nf = jnp.maximum(n, 1).astype(F32)
    large = max_exact + (jnp.log(nf / max_exact) / math.log(MAX_DISTANCE / max_exact)
                         * (NUM_BUCKETS - max_exact)).astype(jnp.int32)
    large = jnp.minimum(large, NUM_BUCKETS - 1)
    return jnp.where(n < max_exact, n, large)


def kernel(x_prompt, x_sample, c_prompt, c_sample, cache_mla, state_swa_k, state_swa_v, page_table, ada_w, ada_b, norm_g, ffn_w_in, ffn_w_out, mla_w_dq, mla_q_norm, mla_w_uq, mla_w_dkv, mla_kv_norm, mla_w_uk, mla_w_uv, mla_w_o, kv_ada_w, kv_ada_b, kv_norm_g, kv_w, swa_w_q, swa_w_o, swa_sinks, rel_bias_table, final_norm_g):
    batch, seq, d = x_prompt.shape
    dec_batch, dec_seq, _ = x_sample.shape
    assert dec_seq == 1
    depth = ada_w.shape[0]
    n_a = mla_w_dq.shape[0]
    n_pages = page_table.shape[1]
    past_len = n_pages * PAGE_SIZE
    ql = mla_w_dq.shape[2]
    kl = mla_kv_norm.shape[1]
    nh = mla_w_uk.shape[2]
    swa_heads = swa_w_q.shape[2] // SWA_HEAD_DIM
    n_kv = swa_heads // SWA_GROUP
    win_s = state_swa_k.shape[1]
    assert win_s == WINDOW and seq % BLOCK == 0
    mla_scale = float((MLA_NOPE + MLA_ROPE) ** -0.5)
    swa_scale = float(SWA_HEAD_DIM ** -0.5)
    mp = batch * seq

    c_all = jnp.concatenate([c_sample, c_prompt], axis=0)
    n_seq = c_all.shape[0]
    c_all = jnp.pad(c_all, ((0, (-n_seq) % 16), (0, 0)))
    mod_s3 = _ada(c_all, ada_w, ada_b[:, None, :])
    kvmod_s3 = _ada(c_all, kv_ada_w[None], kv_ada_b[None, None, :])
    mod_p4 = mod_s3[:, dec_batch:dec_batch + batch, None, :]
    kvmod_p4 = kvmod_s3[:, dec_batch:dec_batch + batch, None, :]

    cos_p, sin_p = _rope_tables(jnp.arange(seq, dtype=jnp.int32))
    cos_s, sin_s = _rope_tables(jnp.full((1,), past_len, jnp.int32))

    qi = jnp.arange(BLOCK, dtype=jnp.int32)
    km = jnp.arange(2 * BLOCK, dtype=jnp.int32) - BLOCK
    d_p = qi[:, None] - km[None, :]
    win_mask = (d_p >= 0) & (d_p <= WINDOW)
    onehot_p = (_t5_bucket(d_p)[:, :, None] == jnp.arange(NUM_BUCKETS, dtype=jnp.int32)).astype(F32)
    bias_p = jnp.einsum("qkb,bh->hqk", onehot_p, rel_bias_table.astype(F32),
                        precision=lax.Precision.HIGHEST)
    bias_pairs = bias_p.reshape(swa_heads // 2, 2, BLOCK, 2 * BLOCK).transpose(0, 2, 1, 3)
    pairs_per_kv = SWA_GROUP // 2
    bias_t = jnp.transpose(bias_pairs.reshape(n_kv, pairs_per_kv * BLOCK, 4 * BLOCK), (0, 2, 1))
    mask_first = win_mask & (km >= 0)[None, :]
    mask_t = jnp.stack([jnp.tile(mask_first, (pairs_per_kv, 2)).T,
                        jnp.tile(win_mask, (pairs_per_kv, 2)).T]).astype(F32)
    d_s = jnp.concatenate([jnp.arange(win_s, 0, -1, dtype=jnp.int32), jnp.zeros((1,), jnp.int32)])
    bias_s = jnp.transpose(rel_bias_table[_t5_bucket(d_s)], (1, 0)).astype(F32)
    bias_s_win = bias_s[:, :win_s].reshape(n_kv, SWA_GROUP, win_s)
    bias_s_new = bias_s[:, win_s:].reshape(n_kv, SWA_GROUP, 1)
    state_kt = jnp.transpose(state_swa_k, (0, 2, 3, 1))
    state_vt = jnp.transpose(state_swa_v, (0, 2, 3, 1))

    cache_t = jnp.transpose(cache_mla, (0, 1, 3, 2))

    x_p = x_prompt.reshape(mp, d)
    x_s = x_sample.reshape(dec_batch, d)
    rows_p, rows_s = [], []
    kv4_p = k_p = v_p = k_new = v_new = None

    for l in range(depth):
        if l == n_a:
            kv_k = kv_w[:, :n_kv * SWA_HEAD_DIM].reshape(d, n_kv, SWA_HEAD_DIM)
            kv_v = kv_w[:, n_kv * SWA_HEAD_DIM:].reshape(d, n_kv, SWA_HEAD_DIM)
            zero = jnp.zeros_like(kv_k)
            lo = lambda w: jnp.concatenate([w, zero], axis=-1).reshape(d, n_kv * LANES)
            hi = lambda w: jnp.concatenate([zero, w], axis=-1).reshape(d, n_kv * LANES)
            nkv = 2 * n_kv * SWA_HEAD_DIM
            w4 = 4 * n_kv * LANES
            w_sh = jnp.concatenate([lo(kv_k), hi(kv_k), lo(kv_v), hi(kv_v), kv_w], axis=1).astype(BF)
            kvg = kv_norm_g[None]
            kv4_p, kv_f32_p = _mod_mm(x_p, kvg, _Mod(kvmod_p4, 0, 0), _Mod(kvmod_p4, 0, 1), w_sh,
                                      [(BF, 0, w4), (F32, w4, w4 + nkv)], seq_rows=seq)
            k_p = kv_f32_p[:, :nkv // 2].reshape(batch, seq, n_kv, SWA_HEAD_DIM)
            v_p = kv_f32_p[:, nkv // 2:].reshape(batch, seq, n_kv, SWA_HEAD_DIM)
            (kv_f32_s,) = _mod_mm(x_s, kvg, _Mod(kvmod_s3, 0, 0), _Mod(kvmod_s3, 0, 1), w_sh[:, w4:],
                                  [(F32, 0, nkv)], seq_rows=1)
            k_new = kv_f32_s[:, :nkv // 2].reshape(dec_batch, n_kv, SWA_HEAD_DIM)
            v_new = kv_f32_s[:, nkv // 2:].reshape(dec_batch, n_kv, SWA_HEAD_DIM)

        g = norm_g[l]
        mp_l = [_Mod(mod_p4, l, n) for n in range(3 * N_SUB)]
        ms_l = [_Mod(mod_s3, l, n) for n in range(3 * N_SUB)]

        x_s, w_bf = _ffn(x_s, g[0][None], ms_l[0], ms_l[1], ms_l[2], (ffn_w_in, ffn_w_out, l, 0), seq_rows=1)
        x_p = _ffn(x_p, g[0][None], mp_l[0], mp_l[1], mp_l[2], w_bf, seq_rows=seq)

        if l < n_a:
            a = l
            dkv = mla_w_dkv[a]
            w1 = jnp.concatenate([mla_w_dq[a], dkv[:, :kl], _pad_lanes(dkv[:, kl:]),
                                  _pad_lanes(_swap_halves(dkv[:, kl:]))], axis=1).astype(BF)
            wuq = mla_w_uq[a].reshape(ql, nh, MLA_NOPE + MLA_ROPE)
            wuq_r = wuq[:, :, MLA_NOPE:]
            w2t = jnp.concatenate([wuq[:, :, :MLA_NOPE].reshape(ql, nh * MLA_NOPE),
                                   _pad_lanes(wuq_r).reshape(ql, nh * LANES),
                                   _pad_lanes(_swap_halves(wuq_r)).reshape(ql, nh * LANES)], axis=1).T.astype(BF)
            w3 = jnp.transpose(mla_w_uk[a], (1, 0, 2)).astype(BF)
            wuv = jnp.transpose(mla_w_uv[a], (1, 0, 2)).astype(BF)
            wuvt = jnp.transpose(mla_w_uv[a], (1, 2, 0)).astype(BF)
            w_o = mla_w_o[a].astype(BF)
            qn, kn = mla_q_norm[a][None], mla_kv_norm[a][None]

            row_p, kcat_p, latt_p, qt_p = _mla_pre(x_p, g[1][None], mp_l[3], mp_l[4], cos_p, sin_p,
                                                   w1, qn, kn, w2t, w3, seq_rows=seq)
            oh_t_p = _mla_flash(qt_p, kcat_p, latt_p, wuvt, batch=batch, seq=seq, scale=mla_scale)
            x_p = _mm_resid(oh_t_p, w_o, x_p, mp_l[5], seq_rows=seq, a_transposed=True)

            row_s, kcat_s, _, qt_s = _mla_pre(x_s, g[1][None], ms_l[3], ms_l[4], cos_s, sin_s,
                                              w1, qn, kn, w2t, w3, seq_rows=1)
            q_s = jnp.transpose(qt_s.reshape(qt_s.shape[0], kl + LANES, nh, -1), (0, 3, 2, 1))
            q_s = q_s.reshape(dec_batch, nh, kl + LANES)
            o_lat_s = _mla_decode(page_table, q_s, kcat_s[:, None, :],
                                  cache_t, layer=a, scale=mla_scale)
            oh_s = _uv(jnp.transpose(o_lat_s, (1, 0, 2)), wuv)
            x_s = _mm_resid(oh_s, w_o, x_s, ms_l[5], seq_rows=1)
            rows_p.append(row_p.reshape(batch, seq, kl + MLA_ROPE))
            rows_s.append(row_s.reshape(dec_batch, 1, kl + MLA_ROPE))
        else:
            b = l - n_a
            w_q = swa_w_q[b].astype(BF)
            w_o = swa_w_o[b].astype(BF)
            nq_cols = swa_heads * SWA_HEAD_DIM
            (qt_p,) = _mod_mm(x_p, g[1][None], mp_l[3], mp_l[4], swa_w_q[b].T.astype(BF), [(BF, 0, nq_cols)],
                              seq_rows=seq, out_transposed=True)
            sink_cols = jnp.repeat(swa_sinks[b].reshape(n_kv, pairs_per_kv, 2).transpose(0, 2, 1), BLOCK,
                                   axis=2)[:, :, None, :]
            ot_p = _swa_prompt(qt_p, kv4_p, bias_t, mask_t, sink_cols, batch=batch, seq=seq, n_kv=n_kv,
                               scale=swa_scale)
            x_p = _mm_resid(ot_p, w_o, x_p, mp_l[5], seq_rows=seq, a_transposed=True)

            (q_s,) = _mod_mm(x_s, g[1][None], ms_l[3], ms_l[4], w_q, [(BF, 0, nq_cols)], seq_rows=1)
            o_s = _swa_sample(q_s.reshape(dec_batch, swa_heads, SWA_HEAD_DIM), state_kt, state_vt, k_new, v_new,
                              bias_s_win, bias_s_new, swa_sinks[b].reshape(n_kv, SWA_GROUP, 1), scale=swa_scale)
            x_s = _mm_resid(o_s.reshape(dec_batch, swa_heads * SWA_HEAD_DIM), w_o, x_s, ms_l[5], seq_rows=1)

        x_s, w_bf = _ffn(x_s, g[2][None], ms_l[6], ms_l[7], ms_l[8], (ffn_w_in, ffn_w_out, l, 1), seq_rows=1)
        x_p = _ffn(x_p, g[2][None], mp_l[6], mp_l[7], mp_l[8], w_bf, seq_rows=seq)

    y_prompt = _final_norm(x_p, final_norm_g[None]).reshape(batch, seq, d)
    y_sample = _final_norm(x_s, final_norm_g[None]).reshape(dec_batch, 1, d)
    new_mla_prompt = jnp.stack(rows_p, axis=1)
    new_mla_sample = jnp.stack(rows_s, axis=1)
    win_p = min(WINDOW, seq)
    shift_in = lambda st, new: jnp.transpose(jnp.concatenate([st[..., 1:], new[..., None]], axis=-1), (0, 3, 1, 2))
    return (y_prompt, y_sample, new_mla_prompt, new_mla_sample,
            k_p[:, seq - win_p:], v_p[:, seq - win_p:],
            shift_in(state_kt, k_new), shift_in(state_vt, v_new))
```

```python
import functools
import math
from typing import NamedTuple

import jax
import jax.numpy as jnp
import numpy as np
from jax import lax
from jax.experimental import pallas as pl
from jax.experimental.pallas import tpu as pltpu

EPS = 1e-6
NEG_INF = -1e30
ROPE_THETA = 10000.0
PAGE_SIZE = 128
WINDOW = 128
BLOCK = 128
NUM_BUCKETS = 32
MAX_DISTANCE = 128
MLA_NOPE = 128
MLA_ROPE = 64
MLA_V = 128
SWA_HEAD_DIM = 64
SWA_GROUP = 8
N_SUB = 3

DECODE_SLOTS = 3
LANES = 128
V7X_VMEM_LIMIT_BYTES = 56 * 1024 * 1024

BF = jnp.bfloat16
F32 = jnp.float32


def _params(*sem):
    return pltpu.CompilerParams(dimension_semantics=sem, vmem_limit_bytes=V7X_VMEM_LIMIT_BYTES)


def _pick(n, prefs):
    for p in prefs:
        if p <= n and n % p == 0:
            return p
    return n


def _row_tile(m, seq_rows, prefs):
    return _pick(m if seq_rows == 1 else seq_rows, prefs)


def _const_spec(arr):
    nd = arr.ndim
    return pl.BlockSpec(arr.shape, lambda *_: (0,) * nd, pipeline_mode=pl.Buffered(1))


def _silu(x):
    return x * (1.0 / (1.0 + jnp.exp(-x)))


def _rms(x, g):
    return x * lax.rsqrt(jnp.mean(x * x, axis=-1, keepdims=True) + EPS) * g


def _modulate(x, g, shift, scale):
    return _rms(x, g) * (1.0 + scale) + shift


def _dot(a, b):
    return jnp.dot(a, b, preferred_element_type=F32)


def _dot_nt(a, b):
    return lax.dot_general(a, b, (((1,), (1,)), ((), ())), preferred_element_type=F32)


class _Mod(NamedTuple):
    arr: jax.Array
    layer: int
    n: int


def _mod_spec(mod, tm, seq_rows, d, tn=None, col=None):
    tn = d if tn is None else tn
    base = mod.n * (d // tn)
    cidx = (lambda rest: base + col(rest[0])) if col is not None else (lambda rest: base)
    if mod.arr.ndim == 4:
        return pl.BlockSpec((None, None, 1, tn), lambda i, *rest: (mod.layer, (i * tm) // seq_rows, 0, cidx(rest)))
    return pl.BlockSpec((None, tm, tn), lambda i, *rest: (mod.layer, i, cidx(rest)))


def _pos_spec(arr, tm):
    p = arr.shape[0]
    if p == 1:
        return pl.BlockSpec((1, LANES), lambda i, *_: (0, 0))
    nblk = p // tm
    return pl.BlockSpec((tm, LANES), lambda i, *_: (i % nblk, 0))


def _ada_kernel(c_ref, w_ref, b_ref, o_ref):
    a = _silu(c_ref[...]).astype(BF)
    o_ref[...] = _dot(a, w_ref[...].astype(BF)) + b_ref[...]


def _ada(c, w, b):
    nl, d, n = w.shape
    m = c.shape[0]
    tn = _pick(n, (1024, 512, 256, 128))
    return pl.pallas_call(
        _ada_kernel,
        grid=(nl, n // tn),
        in_specs=[
            pl.BlockSpec((m, d), lambda l, j: (0, 0)),
            pl.BlockSpec((None, d, tn), lambda l, j: (l, 0, j)),
            pl.BlockSpec((None, 1, tn), lambda l, j: (l, 0, j)),
        ],
        out_specs=pl.BlockSpec((None, m, tn), lambda l, j: (l, 0, j)),
        out_shape=jax.ShapeDtypeStruct((nl, m, n), F32),
        compiler_params=_params("parallel", "parallel"),
        name="ada_params",
    )(c, w, b)


def _ffn_kernel(x_ref, g_ref, sh_ref, sc_ref, wg_ref, wu_ref, wo_ref, xc_ref, gt_ref, o_ref, *rest, nf, tf,
                emit_bf16):
    if emit_bf16:
        wg_out, wu_out, wo_out, h_ref, a_ref = rest
    else:
        h_ref, a_ref = rest
    j = pl.program_id(1)

    @pl.when(j == 0)
    def _():
        h_ref[...] = _modulate(x_ref[...], g_ref[...], sh_ref[...], sc_ref[...]).astype(BF)

    @pl.when(j < nf)
    def _():
        h = h_ref[...]
        wg = wg_ref[...].astype(BF)
        wu = wu_ref[...].astype(BF)
        if emit_bf16:
            wg_out[...] = wg
            wu_out[...] = wu
        gate = _dot(h, wg)
        up = _dot(h, wu)
        a_ref[j] = (_silu(gate) * up).astype(BF)

    @pl.when(j >= nf)
    def _():
        if emit_bf16:
            wo_out[...] = wo_ref[...].astype(BF)
            wo = wo_out
        else:
            wo = wo_ref
        y = _dot(a_ref[0], wo[0:tf, :])
        for c in range(1, nf):
            y = y + _dot(a_ref[c], wo[c * tf:(c + 1) * tf, :])
        o_ref[...] = xc_ref[...] + 0.5 * gt_ref[...] * y


def _ffn(x, g, shift, scale, gate, weights, *, seq_rows):
    emit_bf16 = len(weights) == 4
    m, d = x.shape
    tm = _row_tile(m, seq_rows, (1024, 512, 256, 128))
    if emit_bf16:
        w_in, w_out, layer, half = weights
        f = w_out.shape[2]
        assert m == tm, "the bf16 weight copies are written by a single row tile"
    else:
        wg, wu, wo = weights
        f = wo.shape[0]
    tf = _pick(f, (512, 256, 128))
    tn = _pick(d, (256, 128))
    nf = f // tf
    sh_spec = _mod_spec(shift, tm, seq_rows, d)
    sc_spec = _mod_spec(scale, tm, seq_rows, d)
    col = lambda j: jnp.maximum(j - nf, 0)
    fcol = lambda j: jnp.minimum(j, nf - 1)
    gate_spec = _mod_spec(gate, tm, seq_rows, d, tn=tn, col=col)
    if emit_bf16:
        w_specs = [
            pl.BlockSpec((None, None, d, tf), lambda i, j: (layer, half, 0, fcol(j))),
            pl.BlockSpec((None, None, d, tf), lambda i, j: (layer, half, 0, fcol(j) + nf)),
            pl.BlockSpec((None, None, f, tn), lambda i, j: (layer, half, 0, col(j))),
        ]
        w_args = (w_in, w_in, w_out)
    else:
        w_specs = [
            pl.BlockSpec((d, tf), lambda i, j: (0, fcol(j))),
            pl.BlockSpec((d, tf), lambda i, j: (0, fcol(j))),
            pl.BlockSpec((f, tn), lambda i, j: (0, col(j))),
        ]
        w_args = (wg, wu, wo)
    out_specs = [pl.BlockSpec((tm, tn), lambda i, j: (i, col(j)))]
    out_shape = [jax.ShapeDtypeStruct((m, d), F32)]
    if emit_bf16:
        out_specs += [
            pl.BlockSpec((d, tf), lambda i, j: (0, fcol(j))),
            pl.BlockSpec((d, tf), lambda i, j: (0, fcol(j))),
            pl.BlockSpec((f, tn), lambda i, j: (0, col(j))),
        ]
        out_shape += [jax.ShapeDtypeStruct((d, f), BF), jax.ShapeDtypeStruct((d, f), BF),
                      jax.ShapeDtypeStruct((f, d), BF)]
    outs = pl.pallas_call(
        functools.partial(_ffn_kernel, nf=nf, tf=tf, emit_bf16=emit_bf16),
        grid=(m // tm, nf + d // tn),
        in_specs=[
            pl.BlockSpec((tm, d), lambda i, j: (i, 0), pipeline_mode=pl.Buffered(1)),
            pl.BlockSpec((1, d), lambda i, j: (0, 0)),
            sh_spec, sc_spec,
            *w_specs,
            pl.BlockSpec((tm, tn), lambda i, j: (i, col(j))),
            gate_spec,
        ],
        out_specs=out_specs,
        out_shape=out_shape,
        scratch_shapes=[pltpu.VMEM((tm, d), BF), pltpu.VMEM((nf, tm, tf), BF)],
        compiler_params=_params("parallel", "arbitrary"),
        name="ffn_half",
    )(x, g, shift.arr, scale.arr, *w_args, x, gate.arr)
    return (outs[0], tuple(outs[1:])) if emit_bf16 else outs[0]


def _mod_mm_kernel(x_ref, g_ref, sh_ref, sc_ref, w_ref, *o_refs, out_transposed, ranges):
    h = _modulate(x_ref[...], g_ref[...], sh_ref[...], sc_ref[...]).astype(BF)
    y = _dot_nt(w_ref[...], h) if out_transposed else _dot(h, w_ref[...])
    for o_ref, (lo, hi) in zip(o_refs, ranges):
        o_ref[...] = (y[lo:hi] if out_transposed else y[:, lo:hi]).astype(o_ref.dtype)


def _mod_mm(x, g, shift, scale, w, outs, *, seq_rows, out_transposed=False):
    m, d = x.shape
    tm = _row_tile(m, seq_rows, (512, 256, 128))
    sh_spec = _mod_spec(shift, tm, seq_rows, d)
    sc_spec = _mod_spec(scale, tm, seq_rows, d)
    if out_transposed:
        out_specs = [pl.BlockSpec((hi - lo, tm), lambda i: (0, i)) for _, lo, hi in outs]
        out_shape = [jax.ShapeDtypeStruct((hi - lo, m), dt) for dt, lo, hi in outs]
    else:
        out_specs = [pl.BlockSpec((tm, hi - lo), lambda i: (i, 0)) for _, lo, hi in outs]
        out_shape = [jax.ShapeDtypeStruct((m, hi - lo), dt) for dt, lo, hi in outs]
    return pl.pallas_call(
        functools.partial(_mod_mm_kernel, out_transposed=out_transposed, ranges=[(lo, hi) for _, lo, hi in outs]),
        grid=(m // tm,),
        in_specs=[
            pl.BlockSpec((tm, d), lambda i: (i, 0)),
            pl.BlockSpec((1, d), lambda i: (0, 0)),
            sh_spec, sc_spec,
            _const_spec(w),
        ],
        out_specs=out_specs,
        out_shape=out_shape,
        compiler_params=_params("parallel"),
        name="mod_matmul",
    )(x, g, shift.arr, scale.arr, w)


def _mm_resid_kernel(a_ref, w_ref, x_ref, gt_ref, o_ref, *, a_transposed):
    if a_transposed:
        y = lax.dot_general(a_ref[...], w_ref[...], (((0,), (0,)), ((), ())), preferred_element_type=F32)
    else:
        y = _dot(a_ref[...], w_ref[...])
    o_ref[...] = x_ref[...] + gt_ref[...] * y


def _mm_resid(a, w, x, gate, *, seq_rows, a_transposed=False):
    m, n = x.shape
    k = w.shape[0]
    tm = _row_tile(m, seq_rows, (512, 256, 128))
    return pl.pallas_call(
        functools.partial(_mm_resid_kernel, a_transposed=a_transposed),
        grid=(m // tm,),
        in_specs=[
            pl.BlockSpec((k, tm), lambda i: (0, i)) if a_transposed else pl.BlockSpec((tm, k), lambda i: (i, 0)),
            _const_spec(w),
            pl.BlockSpec((tm, n), lambda i: (i, 0)),
            _mod_spec(gate, tm, seq_rows, n),
        ],
        out_specs=pl.BlockSpec((tm, n), lambda i: (i, 0)),
        out_shape=jax.ShapeDtypeStruct((m, n), F32),
        compiler_params=_params("parallel"),
        name="matmul_residual",
    )(a, w, x, gate.arr)


def _mla_pre_kernel(x_ref, g_ref, sh_ref, sc_ref, cos_ref, sin_ref, cost_ref, sint_ref, w1_ref, qn_ref, kn_ref,
                    w2t_ref, w3_ref, row_ref, kcat_ref, latt_ref, qt_ref, *, ql, kl, nh):
    tm = x_ref.shape[0]
    cos_t = cos_ref[...]
    sin_t = sin_ref[...]
    h = _modulate(x_ref[...], g_ref[...], sh_ref[...], sc_ref[...]).astype(BF)
    p1 = _dot(h, w1_ref[...])
    cq = _rms(p1[:, :ql], qn_ref[...])
    lat = _rms(p1[:, ql:ql + kl], kn_ref[...])
    a = ql + kl
    kr = p1[:, a:a + LANES] * cos_t + p1[:, a + LANES:a + 2 * LANES] * sin_t
    row_ref[:, :kl] = lat
    row_ref[:, kl:] = kr[:, :MLA_ROPE]
    kcat_ref[:, :kl] = lat.astype(BF)
    kcat_ref[:, kl:] = kr.astype(BF)
    latt_ref[...] = lat.T.astype(BF)
    q2t = _dot(w2t_ref[...], cq.T.astype(BF))
    cos_c = cost_ref[...]
    sin_c = sint_ref[...]
    hn = nh * LANES
    for hd in range(nh):
        lo = hd * LANES
        qn_t = q2t[lo:lo + LANES].astype(BF)
        qt_ref[:kl, hd * tm:(hd + 1) * tm] = _dot(w3_ref[hd], qn_t).astype(BF)
        qr_t = q2t[hn + lo:hn + lo + LANES] * cos_c + q2t[2 * hn + lo:2 * hn + lo + LANES] * sin_c
        qt_ref[kl:, hd * tm:(hd + 1) * tm] = qr_t.astype(BF)


def _mla_pre(x, g, shift, scale, cos_t, sin_t, w1, qnorm, knorm, w2t, w3, *, seq_rows):
    m, d = x.shape
    ql = qnorm.shape[1]
    kl = knorm.shape[1]
    nh = w3.shape[0]
    c = kl + LANES
    tm = _row_tile(m, seq_rows, (256, 128))
    sh_spec = _mod_spec(shift, tm, seq_rows, d)
    sc_spec = _mod_spec(scale, tm, seq_rows, d)
    ps = _pos_spec(cos_t, tm)
    cos_c, sin_c = cos_t.T, sin_t.T
    npos = cos_t.shape[0]
    if npos == 1:
        pst = pl.BlockSpec((LANES, 1), lambda i: (0, 0))
    else:
        pst = pl.BlockSpec((LANES, tm), lambda i: (0, i % (npos // tm)))
    return pl.pallas_call(
        functools.partial(_mla_pre_kernel, ql=ql, kl=kl, nh=nh),
        grid=(m // tm,),
        in_specs=[
            pl.BlockSpec((tm, d), lambda i: (i, 0)),
            pl.BlockSpec((1, d), lambda i: (0, 0)),
            sh_spec, sc_spec, ps, ps, pst, pst,
            _const_spec(w1), _const_spec(qnorm), _const_spec(knorm), _const_spec(w2t), _const_spec(w3),
        ],
        out_specs=[
            pl.BlockSpec((tm, kl + MLA_ROPE), lambda i: (i, 0)),
            pl.BlockSpec((tm, c), lambda i: (i, 0)),
            pl.BlockSpec((kl, tm), lambda i: (0, i)),
            pl.BlockSpec((None, c, nh * tm), lambda i: (i, 0, 0)),
        ],
        out_shape=[
            jax.ShapeDtypeStruct((m, kl + MLA_ROPE), F32),
            jax.ShapeDtypeStruct((m, c), BF),
            jax.ShapeDtypeStruct((kl, m), BF),
            jax.ShapeDtypeStruct((m // tm, c, nh * tm), BF),
        ],
        compiler_params=_params("parallel"),
        name="mla_project",
    )(x, g, shift.arr, scale.arr, cos_t, sin_t, cos_c, sin_c, w1, qnorm, knorm, w2t, w3)


def _mla_flash_kernel(it_ref, jt_ref, qt_ref, k_ref, vt_ref, wuvt_ref, o_ref, m_sc, l_sc, acc_sc, *,
                      tq, tk, nh, exp2_scale):
    t = pl.program_id(1)
    i = it_ref[t]
    j = jt_ref[t]
    cols = nh * tq
    j_last = ((i + 1) * tq - 1) // tk

    @pl.when(j == 0)
    def _():
        m_sc[...] = jnp.full(m_sc.shape, NEG_INF, F32)
        l_sc[...] = jnp.zeros(l_sc.shape, F32)
        acc_sc[...] = jnp.zeros(acc_sc.shape, F32)

    def step(masked, nkeys):
        s = _dot(k_ref[:nkeys], qt_ref[...])
        if masked:
            kpos = j * tk + lax.broadcasted_iota(jnp.int32, (nkeys, cols), 0)
            qpos = i * tq + (lax.broadcasted_iota(jnp.int32, (nkeys, cols), 1) & (tq - 1))
            s = jnp.where(kpos <= qpos, s, NEG_INF)
        m_old = m_sc[...]
        m_new = jnp.maximum(m_old, jnp.max(s, axis=0, keepdims=True))
        alpha = jnp.exp2((m_old - m_new) * exp2_scale)
        p = jnp.exp2((s - m_new) * exp2_scale)
        l_sc[...] = alpha * l_sc[...] + jnp.sum(p, axis=0, keepdims=True)
        acc_sc[...] = alpha * acc_sc[...] + _dot(vt_ref[:, :nkeys], p.astype(BF))
        m_sc[...] = m_new

    needs_mask = (j + 1) * tk - 1 > i * tq
    half_block = (i + 1) * tq <= j * tk + tk // 2

    @pl.when(needs_mask & half_block)
    def _():
        step(True, tk // 2)

    @pl.when(needs_mask & jnp.logical_not(half_block))
    def _():
        step(True, tk)

    @pl.when(jnp.logical_not(needs_mask))
    def _():
        step(False, tk)

    @pl.when(j == j_last)
    def _():
        o_t = (acc_sc[...] * (1.0 / l_sc[...])).astype(BF)
        for hd in range(nh):
            o_ref[hd * MLA_V:(hd + 1) * MLA_V, :] = _dot(wuvt_ref[hd], o_t[:, hd * tq:(hd + 1) * tq]).astype(BF)


def _mla_flash(qt, kcat, latt, wuvt, *, batch, seq, scale):
    nblk, c, cols = qt.shape
    nh, _, kl = wuvt.shape
    tq = cols // nh
    m = nblk * tq
    tk = _pick(seq, (512, 256, 128))
    nq = seq // tq
    nk = seq // tk
    sched = [(i, j) for i in range(nq) for j in range(((i + 1) * tq - 1) // tk + 1)]
    it = jnp.asarray(np.array([s[0] for s in sched], np.int32))
    jt = jnp.asarray(np.array([s[1] for s in sched], np.int32))
    grid_spec = pltpu.PrefetchScalarGridSpec(
        num_scalar_prefetch=2,
        grid=(batch, len(sched)),
        in_specs=[
            pl.BlockSpec((None, c, cols), lambda b, t, it_ref, jt_ref: (b * nq + it_ref[t], 0, 0)),
            pl.BlockSpec((tk, c), lambda b, t, it_ref, jt_ref: (b * nk + jt_ref[t], 0)),
            pl.BlockSpec((kl, tk), lambda b, t, it_ref, jt_ref: (0, b * nk + jt_ref[t])),
            pl.BlockSpec(wuvt.shape, lambda b, t, it_ref, jt_ref: (0, 0, 0), pipeline_mode=pl.Buffered(1)),
        ],
        out_specs=pl.BlockSpec((nh * MLA_V, tq), lambda b, t, it_ref, jt_ref: (0, b * nq + it_ref[t])),
        scratch_shapes=[
            pltpu.VMEM((1, cols), F32),
            pltpu.VMEM((1, cols), F32),
            pltpu.VMEM((kl, cols), F32),
        ],
    )
    return pl.pallas_call(
        functools.partial(_mla_flash_kernel, tq=tq, tk=tk, nh=nh, exp2_scale=scale * math.log2(math.e)),
        grid_spec=grid_spec,
        out_shape=jax.ShapeDtypeStruct((nh * MLA_V, m), BF),
        compiler_params=_params("parallel", "arbitrary"),
        name="mla_prompt_attention",
    )(it, jt, qt, kcat, latt, wuvt)


def _mla_decode_kernel(pt_ref, q_ref, new_ref, cache_ref, o_ref, pages, sems, kbuf, m_sc, l_sc, acc_sc, *,
                       npg, kl, scale, layer, chunks_per_seq, n_chunks):
    c = pl.program_id(1)
    t = pl.program_id(0) * chunks_per_seq + c

    def chunk_copy(chunk, slot, k):
        page = pt_ref[chunk * npg + k]
        return pltpu.make_async_copy(cache_ref.at[page, layer], pages.at[slot, k], sems.at[slot])

    def start_chunk(chunk):
        slot = chunk % DECODE_SLOTS
        for k in range(npg):
            chunk_copy(chunk, slot, k).start()

    @pl.when(t == 0)
    def _():
        for first in range(min(DECODE_SLOTS - 1, n_chunks)):
            start_chunk(first)

    @pl.when(t + (DECODE_SLOTS - 1) < n_chunks)
    def _():
        start_chunk(t + (DECODE_SLOTS - 1))

    slot = t % DECODE_SLOTS
    for k in range(npg):
        chunk_copy(t, slot, k).wait()

    @pl.when(c == 0)
    def _():
        m_sc[...] = jnp.full(m_sc.shape, NEG_INF, F32)
        l_sc[...] = jnp.zeros(l_sc.shape, F32)
        acc_sc[...] = jnp.zeros(acc_sc.shape, F32)

    for k in range(npg):
        kbuf[:, k * PAGE_SIZE:(k + 1) * PAGE_SIZE] = pages[slot, k].astype(BF)

    q = q_ref[...]
    keys_t = kbuf[...]
    s = (_dot(q[:, :kl], keys_t[:kl]) + _dot(q[:, kl:kl + MLA_ROPE], keys_t[kl:])) * scale
    m_old = m_sc[...]
    m_new = jnp.maximum(m_old, jnp.max(s, axis=-1, keepdims=True))
    alpha = jnp.exp(m_old - m_new)
    p = jnp.exp(s - m_new)
    l_sc[...] = alpha * l_sc[...] + jnp.sum(p, axis=-1, keepdims=True)
    acc_sc[...] = alpha * acc_sc[...] + _dot_nt(p.astype(BF), keys_t[:kl])
    m_sc[...] = m_new

    @pl.when(c == pl.num_programs(1) - 1)
    def _():
        new = new_ref[...].astype(F32)
        s_new = jnp.sum(q.astype(F32) * new, axis=-1, keepdims=True) * scale
        m_old = m_sc[...]
        m_new = jnp.maximum(m_old, s_new)
        alpha = jnp.exp(m_old - m_new)
        p_new = jnp.exp(s_new - m_new)
        l_new = alpha * l_sc[...] + p_new
        acc = alpha * acc_sc[...] + p_new.astype(BF).astype(F32) * new[:, :kl]
        o_ref[...] = (acc * (1.0 / l_new)).astype(BF)


def _mla_decode(page_table, qcat, kcat, cache_t, *, layer, scale):
    nb, nh, c = qcat.shape
    lat = cache_t.shape[2]
    kl = lat - MLA_ROPE
    n_pages = page_table.shape[1]
    npg = _pick(n_pages, (32, 16, 8, 4, 2, 1))
    chunks_per_seq = n_pages // npg
    pt = page_table.reshape(-1)
    grid_spec = pltpu.PrefetchScalarGridSpec(
        num_scalar_prefetch=1,
        grid=(nb, chunks_per_seq),
        in_specs=[
            pl.BlockSpec((None, nh, c), lambda b, ch, pt_ref: (b, 0, 0)),
            pl.BlockSpec((None, 1, c), lambda b, ch, pt_ref: (b, 0, 0)),
            pl.BlockSpec(memory_space=pl.ANY),
        ],
        out_specs=pl.BlockSpec((None, nh, kl), lambda b, ch, pt_ref: (b, 0, 0)),
        scratch_shapes=[
            pltpu.VMEM((DECODE_SLOTS, npg, lat, PAGE_SIZE), F32),
            pltpu.SemaphoreType.DMA((DECODE_SLOTS,)),
            pltpu.VMEM((lat, npg * PAGE_SIZE), BF),
            pltpu.VMEM((nh, 1), F32),
            pltpu.VMEM((nh, 1), F32),
            pltpu.VMEM((nh, kl), F32),
        ],
    )
    return pl.pallas_call(
        functools.partial(_mla_decode_kernel, npg=npg, kl=kl, scale=scale, layer=layer,
                          chunks_per_seq=chunks_per_seq, n_chunks=nb * chunks_per_seq),
        grid_spec=grid_spec,
        out_shape=jax.ShapeDtypeStruct((nb, nh, kl), BF),
        compiler_params=_params("arbitrary", "arbitrary"),
        name="mla_sample_attention",
    )(pt, qcat, kcat, cache_t)


def _uv_kernel(o_ref, wuv_ref, out_ref):
    nh = o_ref.shape[0]
    for hd in range(nh):
        out_ref[:, hd * MLA_V:(hd + 1) * MLA_V] = _dot(o_ref[hd], wuv_ref[hd]).astype(BF)


def _uv(o_lat, wuv):
    nh, m, kl = o_lat.shape
    return pl.pallas_call(
        _uv_kernel,
        grid=(1,),
        in_specs=[
            pl.BlockSpec(o_lat.shape, lambda i: (0, 0, 0)),
            pl.BlockSpec(wuv.shape, lambda i: (0, 0, 0)),
        ],
        out_specs=pl.BlockSpec((m, nh * MLA_V), lambda i: (0, 0)),
        out_shape=jax.ShapeDtypeStruct((m, nh * MLA_V), BF),
        compiler_params=_params("arbitrary"),
        name="mla_value_up",
    )(o_lat, wuv)


def _swa_prompt_kernel(qt_ref, kvp_ref, kvc_ref, bias_ref, mask_ref, sink_ref, o_ref, *, n_kv, scale):
    kvw = n_kv * LANES
    pairs_per_kv = SWA_GROUP // 2
    mask = mask_ref[...] > 0.5
    nkeys = 2 * BLOCK
    for c in range(n_kv):
        def sect(s):
            lo = s * kvw + c * LANES
            return jnp.concatenate([kvp_ref[:, lo:lo + LANES], kvc_ref[:, lo:lo + LANES]], axis=0)
        k2 = jnp.concatenate([sect(0), sect(1)], axis=0)
        v2 = jnp.concatenate([sect(2), sect(3)], axis=0)
        pr0 = c * pairs_per_kv
        qc = jnp.concatenate([qt_ref[(pr0 + pp) * LANES:(pr0 + pp + 1) * LANES, :] for pp in range(pairs_per_kv)],
                             axis=1)
        s = _dot(k2, qc) * scale + bias_ref[c]
        s = jnp.where(mask, s, NEG_INF)
        halves = []
        for hf in range(2):
            sh = s[hf * nkeys:(hf + 1) * nkeys]
            sink = sink_ref[c, hf]
            m = jnp.maximum(jnp.max(sh, axis=0, keepdims=True), sink)
            e = jnp.exp(sh - m)
            den = jnp.sum(e, axis=0, keepdims=True) + jnp.exp(sink - m)
            halves.append((e * (1.0 / den)).astype(BF))
        p = jnp.concatenate(halves, axis=0)
        o_t = lax.dot_general(v2, p, (((0,), (0,)), ((), ())), preferred_element_type=F32).astype(BF)
        for pp in range(pairs_per_kv):
            o_ref[(pr0 + pp) * LANES:(pr0 + pp + 1) * LANES, :] = o_t[:, pp * BLOCK:(pp + 1) * BLOCK]


def _swa_prompt(qt, kv4, bias_t, mask_t, sink_cols, *, batch, seq, n_kv, scale):
    dq, m = qt.shape
    nb = seq // BLOCK
    w4 = kv4.shape[1]
    return pl.pallas_call(
        functools.partial(_swa_prompt_kernel, n_kv=n_kv, scale=scale),
        grid=(batch, nb),
        in_specs=[
            pl.BlockSpec((dq, BLOCK), lambda b, j: (0, b * nb + j)),
            pl.BlockSpec((BLOCK, w4), lambda b, j: (b * nb + jnp.maximum(j - 1, 0), 0)),
            pl.BlockSpec((BLOCK, w4), lambda b, j: (b * nb + j, 0)),
            _const_spec(bias_t),
            pl.BlockSpec((None,) + mask_t.shape[1:], lambda b, j: (jnp.minimum(j, 1), 0, 0)),
            _const_spec(sink_cols),
        ],
        out_specs=pl.BlockSpec((dq, BLOCK), lambda b, j: (0, b * nb + j)),
        out_shape=jax.ShapeDtypeStruct((dq, m), BF),
        compiler_params=_params("parallel", "arbitrary"),
        name="swa_prompt_attention",
    )(qt, kv4, kv4, bias_t, mask_t, sink_cols)


def _swa_sample_kernel(q_ref, kt_ref, vt_ref, kn_ref, vn_ref, bias_ref, bias_new_ref, sink_ref, o_ref, *, n_kv, scale):
    for c in range(n_kv):
        q = q_ref[:, c * SWA_GROUP:(c + 1) * SWA_GROUP, :]
        kt = kt_ref[:, c].astype(BF)
        vt = vt_ref[:, c].astype(BF)
        kn = kn_ref[:, c].astype(BF).astype(F32)[:, None, :]
        vn = vn_ref[:, c].astype(BF).astype(F32)[:, None, :]
        s = jnp.einsum("bgd,bdk->bgk", q, kt, preferred_element_type=F32) * scale + bias_ref[c][None]
        s_new = jnp.sum(q.astype(F32) * kn, axis=-1, keepdims=True) * scale + bias_new_ref[c][None]
        sink = sink_ref[c][None]
        m = jnp.maximum(jnp.maximum(jnp.max(s, axis=-1, keepdims=True), s_new), sink)
        e = jnp.exp(s - m)
        e_new = jnp.exp(s_new - m)
        inv = 1.0 / (jnp.sum(e, axis=-1, keepdims=True) + e_new + jnp.exp(sink - m))
        p = (e * inv).astype(BF)
        p_new = (e_new * inv).astype(BF).astype(F32)
        o = jnp.einsum("bgk,bdk->bgd", p, vt, preferred_element_type=F32) + p_new * vn
        o_ref[:, c * SWA_GROUP:(c + 1) * SWA_GROUP, :] = o.astype(BF)


def _swa_sample(q, kt, vt, k_new, v_new, bias, bias_new, sinks, *, scale):
    nb, nh, dh = q.shape
    n_kv, _, w = kt.shape[1:]
    bb = _pick(nb, (16, 8))
    return pl.pallas_call(
        functools.partial(_swa_sample_kernel, n_kv=n_kv, scale=scale),
        grid=(nb // bb,),
        in_specs=[
            pl.BlockSpec((bb, nh, dh), lambda i: (i, 0, 0)),
            pl.BlockSpec((bb, n_kv, dh, w), lambda i: (i, 0, 0, 0)),
            pl.BlockSpec((bb, n_kv, dh, w), lambda i: (i, 0, 0, 0)),
            pl.BlockSpec((bb, n_kv, dh), lambda i: (i, 0, 0)),
            pl.BlockSpec((bb, n_kv, dh), lambda i: (i, 0, 0)),
            pl.BlockSpec(bias.shape, lambda i: (0, 0, 0)),
            pl.BlockSpec(bias_new.shape, lambda i: (0, 0, 0)),
            pl.BlockSpec(sinks.shape, lambda i: (0, 0, 0)),
        ],
        out_specs=pl.BlockSpec((bb, nh, dh), lambda i: (i, 0, 0)),
        out_shape=jax.ShapeDtypeStruct((nb, nh, dh), BF),
        compiler_params=_params("parallel"),
        name="swa_sample_attention",
    )(q, kt, vt, k_new, v_new, bias, bias_new, sinks)


def _final_norm_kernel(x_ref, g_ref, o_ref):
    o_ref[...] = _rms(x_ref[...], g_ref[...])


def _final_norm(x, g):
    m, d = x.shape
    tm = _pick(m, (512, 256, 128))
    return pl.pallas_call(
        _final_norm_kernel,
        grid=(m // tm,),
        in_specs=[pl.BlockSpec((tm, d), lambda i: (i, 0)), pl.BlockSpec((1, d), lambda i: (0, 0))],
        out_specs=pl.BlockSpec((tm, d), lambda i: (i, 0)),
        out_shape=jax.ShapeDtypeStruct((m, d), F32),
        compiler_params=_params("parallel"),
        name="final_norm",
    )(x, g)


def _rope_tables(pos):
    half = MLA_ROPE // 2
    inv = ROPE_THETA ** (-jnp.arange(half, dtype=F32) / half)
    ang = pos.astype(F32)[:, None] * inv
    cos, sin = jnp.cos(ang), jnp.sin(ang)
    pad = jnp.zeros((pos.shape[0], LANES - MLA_ROPE), F32)
    return jnp.concatenate([cos, cos, pad], axis=1), jnp.concatenate([-sin, sin, pad], axis=1)


def _swap_halves(w):
    half = w.shape[-1] // 2
    return jnp.concatenate([w[..., half:], w[..., :half]], axis=-1)


def _pad_lanes(w):
    return jnp.pad(w, [(0, 0)] * (w.ndim - 1) + [(0, LANES - w.shape[-1])])


def _t5_bucket(dist):
    max_exact = NUM_BUCKETS // 2
    n = jnp.maximum(dist, 0)
    nf = jnp.maximum(n, 1).astype(F32)
    large = max_exact + (jnp.log(nf / max_exact) / math.log(MAX_DISTANCE / max_exact)
                         * (NUM_BUCKETS - max_exact)).astype(jnp.int32)
    large = jnp.minimum(large, NUM_BUCKETS - 1)
    return jnp.where(n < max_exact, n, large)


def kernel(x_prompt, x_sample, c_prompt, c_sample, cache_mla, state_swa_k, state_swa_v, page_table, ada_w, ada_b, norm_g, ffn_w_in, ffn_w_out, mla_w_dq, mla_q_norm, mla_w_uq, mla_w_dkv, mla_kv_norm, mla_w_uk, mla_w_uv, mla_w_o, kv_ada_w, kv_ada_b, kv_norm_g, kv_w, swa_w_q, swa_w_o, swa_sinks, rel_bias_table, final_norm_g):
    batch, seq, d = x_prompt.shape
    dec_batch, dec_seq, _ = x_sample.shape
    assert dec_seq == 1
    depth = ada_w.shape[0]
    n_a = mla_w_dq.shape[0]
    n_pages = page_table.shape[1]
    past_len = n_pages * PAGE_SIZE
    ql = mla_w_dq.shape[2]
    kl = mla_kv_norm.shape[1]
    nh = mla_w_uk.shape[2]
    swa_heads = swa_w_q.shape[2] // SWA_HEAD_DIM
    n_kv = swa_heads // SWA_GROUP
    win_s = state_swa_k.shape[1]
    assert win_s == WINDOW and seq % BLOCK == 0
    mla_scale = float((MLA_NOPE + MLA_ROPE) ** -0.5)
    swa_scale = float(SWA_HEAD_DIM ** -0.5)
    mp = batch * seq

    c_all = jnp.concatenate([c_sample, c_prompt], axis=0)
    n_seq = c_all.shape[0]
    c_all = jnp.pad(c_all, ((0, (-n_seq) % 16), (0, 0)))
    mod_s3 = _ada(c_all, ada_w, ada_b[:, None, :])
    kvmod_s3 = _ada(c_all, kv_ada_w[None], kv_ada_b[None, None, :])
    mod_p4 = mod_s3[:, dec_batch:dec_batch + batch, None, :]
    kvmod_p4 = kvmod_s3[:, dec_batch:dec_batch + batch, None, :]

    cos_p, sin_p = _rope_tables(jnp.arange(seq, dtype=jnp.int32))
    cos_s, sin_s = _rope_tables(jnp.full((1,), past_len, jnp.int32))

    qi = jnp.arange(BLOCK, dtype=jnp.int32)
    km = jnp.arange(2 * BLOCK, dtype=jnp.int32) - BLOCK
    d_p = qi[:, None] - km[None, :]
    win_mask = (d_p >= 0) & (d_p <= WINDOW)
    onehot_p = (_t5_bucket(d_p)[:, :, None] == jnp.arange(NUM_BUCKETS, dtype=jnp.int32)).astype(F32)
    bias_p = jnp.einsum("qkb,bh->hqk", onehot_p, rel_bias_table.astype(F32),
                        precision=lax.Precision.HIGHEST)
    bias_pairs = bias_p.reshape(swa_heads // 2, 2, BLOCK, 2 * BLOCK).transpose(0, 2, 1, 3)
    pairs_per_kv = SWA_GROUP // 2
    bias_t = jnp.transpose(bias_pairs.reshape(n_kv, pairs_per_kv * BLOCK, 4 * BLOCK), (0, 2, 1))
    mask_first = win_mask & (km >= 0)[None, :]
    mask_t = jnp.stack([jnp.tile(mask_first, (pairs_per_kv, 2)).T,
                        jnp.tile(win_mask, (pairs_per_kv, 2)).T]).astype(F32)
    d_s = jnp.concatenate([jnp.arange(win_s, 0, -1, dtype=jnp.int32), jnp.zeros((1,), jnp.int32)])
    bias_s = jnp.transpose(rel_bias_table[_t5_bucket(d_s)], (1, 0)).astype(F32)
    bias_s_win = bias_s[:, :win_s].reshape(n_kv, SWA_GROUP, win_s)
    bias_s_new = bias_s[:, win_s:].reshape(n_kv, SWA_GROUP, 1)
    state_kt = jnp.transpose(state_swa_k, (0, 2, 3, 1))
    state_vt = jnp.transpose(state_swa_v, (0, 2, 3, 1))

    cache_t = jnp.transpose(cache_mla, (0, 1, 3, 2))

    x_p = x_prompt.reshape(mp, d)
    x_s = x_sample.reshape(dec_batch, d)
    rows_p, rows_s = [], []
    kv4_p = k_p = v_p = k_new = v_new = None

    for l in range(depth):
        if l == n_a:
            kv_k = kv_w[:, :n_kv * SWA_HEAD_DIM].reshape(d, n_kv, SWA_HEAD_DIM)
            kv_v = kv_w[:, n_kv * SWA_HEAD_DIM:].reshape(d, n_kv, SWA_HEAD_DIM)
            zero = jnp.zeros_like(kv_k)
            lo = lambda w: jnp.concatenate([w, zero], axis=-1).reshape(d, n_kv * LANES)
            hi = lambda w: jnp.concatenate([zero, w], axis=-1).reshape(d, n_kv * LANES)
            nkv = 2 * n_kv * SWA_HEAD_DIM
            w4 = 4 * n_kv * LANES
            w_sh = jnp.concatenate([lo(kv_k), hi(kv_k), lo(kv_v), hi(kv_v), kv_w], axis=1).astype(BF)
            kvg = kv_norm_g[None]
            kv4_p, kv_f32_p = _mod_mm(x_p, kvg, _Mod(kvmod_p4, 0, 0), _Mod(kvmod_p4, 0, 1), w_sh,
                                      [(BF, 0, w4), (F32, w4, w4 + nkv)], seq_rows=seq)
            k_p = kv_f32_p[:, :nkv // 2].reshape(batch, seq, n_kv, SWA_HEAD_DIM)
            v_p = kv_f32_p[:, nkv // 2:].reshape(batch, seq, n_kv, SWA_HEAD_DIM)
            (kv_f32_s,) = _mod_mm(x_s, kvg, _Mod(kvmod_s3, 0, 0), _Mod(kvmod_s3, 0, 1), w_sh[:, w4:],
                                  [(F32, 0, nkv)], seq_rows=1)
            k_new = kv_f32_s[:, :nkv // 2].reshape(dec_batch, n_kv, SWA_HEAD_DIM)
            v_new = kv_f32_s[:, nkv // 2:].reshape(dec_batch, n_kv, SWA_HEAD_DIM)

        g = norm_g[l]
        mp_l = [_Mod(mod_p4, l, n) for n in range(3 * N_SUB)]
        ms_l = [_Mod(mod_s3, l, n) for n in range(3 * N_SUB)]

        x_s, w_bf = _ffn(x_s, g[0][None], ms_l[0], ms_l[1], ms_l[2], (ffn_w_in, ffn_w_out, l, 0), seq_rows=1)
        x_p = _ffn(x_p, g[0][None], mp_l[0], mp_l[1], mp_l[2], w_bf, seq_rows=seq)

        if l < n_a:
            a = l
            dkv = mla_w_dkv[a]
            w1 = jnp.concatenate([mla_w_dq[a], dkv[:, :kl], _pad_lanes(dkv[:, kl:]),
                                  _pad_lanes(_swap_halves(dkv[:, kl:]))], axis=1).astype(BF)
            wuq = mla_w_uq[a].reshape(ql, nh, MLA_NOPE + MLA_ROPE)
            wuq_r = wuq[:, :, MLA_NOPE:]
            w2t = jnp.concatenate([wuq[:, :, :MLA_NOPE].reshape(ql, nh * MLA_NOPE),
                                   _pad_lanes(wuq_r).reshape(ql, nh * LANES),
                                   _pad_lanes(_swap_halves(wuq_r)).reshape(ql, nh * LANES)], axis=1).T.astype(BF)
            w3 = jnp.transpose(mla_w_uk[a], (1, 0, 2)).astype(BF)
            wuv = jnp.transpose(mla_w_uv[a], (1, 0, 2)).astype(BF)
            wuvt = jnp.transpose(mla_w_uv[a], (1, 2, 0)).astype(BF)
            w_o = mla_w_o[a].astype(BF)
            qn, kn = mla_q_norm[a][None], mla_kv_norm[a][None]

            row_p, kcat_p, latt_p, qt_p = _mla_pre(x_p, g[1][None], mp_l[3], mp_l[4], cos_p, sin_p,
                                                   w1, qn, kn, w2t, w3, seq_rows=seq)
            oh_t_p = _mla_flash(qt_p, kcat_p, latt_p, wuvt, batch=batch, seq=seq, scale=mla_scale)
            x_p = _mm_resid(oh_t_p, w_o, x_p, mp_l[5], seq_rows=seq, a_transposed=True)

            row_s, kcat_s, _, qt_s = _mla_pre(x_s, g[1][None], ms_l[3], ms_l[4], cos_s, sin_s,
                                              w1, qn, kn, w2t, w3, seq_rows=1)
            q_s = jnp.transpose(qt_s.reshape(qt_s.shape[0], kl + LANES, nh, -1), (0, 3, 2, 1))
            q_s = q_s.reshape(dec_batch, nh, kl + LANES)
            o_lat_s = _mla_decode(page_table, q_s, kcat_s[:, None, :],
                                  cache_t, layer=a, scale=mla_scale)
            oh_s = _uv(jnp.transpose(o_lat_s, (1, 0, 2)), wuv)
            x_s = _mm_resid(oh_s, w_o, x_s, ms_l[5], seq_rows=1)
            rows_p.append(row_p.reshape(batch, seq, kl + MLA_ROPE))
            rows_s.append(row_s.reshape(dec_batch, 1, kl + MLA_ROPE))
        else:
            b = l - n_a
            w_q = swa_w_q[b].astype(BF)
            w_o = swa_w_o[b].astype(BF)
            nq_cols = swa_heads * SWA_HEAD_DIM
            (qt_p,) = _mod_mm(x_p, g[1][None], mp_l[3], mp_l[4], swa_w_q[b].T.astype(BF), [(BF, 0, nq_cols)],
                              seq_rows=seq, out_transposed=True)
            sink_cols = jnp.repeat(swa_sinks[b].reshape(n_kv, pairs_per_kv, 2).transpose(0, 2, 1), BLOCK,
                                   axis=2)[:, :, None, :]
            ot_p = _swa_prompt(qt_p, kv4_p, bias_t, mask_t, sink_cols, batch=batch, seq=seq, n_kv=n_kv,
                               scale=swa_scale)
            x_p = _mm_resid(ot_p, w_o, x_p, mp_l[5], seq_rows=seq, a_transposed=True)

            (q_s,) = _mod_mm(x_s, g[1][None], ms_l[3], ms_l[4], w_q, [(BF, 0, nq_cols)], seq_rows=1)
            o_s = _swa_sample(q_s.reshape(dec_batch, swa_heads, SWA_HEAD_DIM), state_kt, state_vt, k_new, v_new,
                              bias_s_win, bias_s_new, swa_sinks[b].reshape(n_kv, SWA_GROUP, 1), scale=swa_scale)
            x_s = _mm_resid(o_s.reshape(dec_batch, swa_heads * SWA_HEAD_DIM), w_o, x_s, ms_l[5], seq_rows=1)

        x_s, w_bf = _ffn(x_s, g[2][None], ms_l[6], ms_l[7], ms_l[8], (ffn_w_in, ffn_w_out, l, 1), seq_rows=1)
        x_p = _ffn(x_p, g[2][None], mp_l[6], mp_l[7], mp_l[8], w_bf, seq_rows=seq)

    y_prompt = _final_norm(x_p, final_norm_g[None]).reshape(batch, seq, d)
    y_sample = _final_norm(x_s, final_norm_g[None]).reshape(dec_batch, 1, d)
    new_mla_prompt = jnp.stack(rows_p, axis=1)
    new_mla_sample = jnp.stack(rows_s, axis=1)
    win_p = min(WINDOW, seq)
    shift_in = lambda st, new: jnp.transpose(jnp.concatenate([st[..., 1:], new[..., None]], axis=-1), (0, 3, 1, 2))
    return (y_prompt, y_sample, new_mla_prompt, new_mla_sample,
            k_p[:, seq - win_p:], v_p[:, seq - win_p:],
            shift_in(state_kt, k_new), shift_in(state_vt, v_new))
```

```python
import functools
import math
from typing import NamedTuple

import jax
import jax.numpy as jnp
import numpy as np
from jax import lax
from jax.experimental import pallas as pl
from jax.experimental.pallas import tpu as pltpu

EPS = 1e-6
NEG_INF = -1e30
ROPE_THETA = 10000.0
PAGE_SIZE = 128
WINDOW = 128
BLOCK = 128
NUM_BUCKETS = 32
MAX_DISTANCE = 128
MLA_NOPE = 128
MLA_ROPE = 64
MLA_V = 128
SWA_HEAD_DIM = 64
SWA_GROUP = 8
N_SUB = 3

DECODE_SLOTS = 3
LANES = 128
V7X_VMEM_LIMIT_BYTES = 60 * 1024 * 1024

BF = jnp.bfloat16
F32 = jnp.float32


def _params(*sem):
    return pltpu.CompilerParams(dimension_semantics=sem, vmem_limit_bytes=V7X_VMEM_LIMIT_BYTES)


def _pick(n, prefs):
    for p in prefs:
        if p <= n and n % p == 0:
            return p
    return n


def _row_tile(m, seq_rows, prefs):
    return _pick(m if seq_rows == 1 else seq_rows, prefs)


def _const_spec(arr):
    nd = arr.ndim
    return pl.BlockSpec(arr.shape, lambda *_: (0,) * nd, pipeline_mode=pl.Buffered(1))


def _silu(x):
    return x * (1.0 / (1.0 + jnp.exp(-x)))


def _rms(x, g):
    return x * lax.rsqrt(jnp.mean(x * x, axis=-1, keepdims=True) + EPS) * g


def _modulate(x, g, shift, scale):
    return _rms(x, g) * (1.0 + scale) + shift


def _dot(a, b):
    return jnp.dot(a, b, preferred_element_type=F32)


def _dot_nt(a, b):
    return lax.dot_general(a, b, (((1,), (1,)), ((), ())), preferred_element_type=F32)


class _Mod(NamedTuple):
    arr: jax.Array
    layer: int
    n: int


def _mod_spec(mod, tm, seq_rows, d, tn=None, col=None):
    tn = d if tn is None else tn
    base = mod.n * (d // tn)
    cidx = (lambda rest: base + col(rest[0])) if col is not None else (lambda rest: base)
    if mod.arr.ndim == 4:
        return pl.BlockSpec((None, None, 1, tn), lambda i, *rest: (mod.layer, (i * tm) // seq_rows, 0, cidx(rest)))
    return pl.BlockSpec((None, tm, tn), lambda i, *rest: (mod.layer, i, cidx(rest)))


def _pos_spec(arr, tm):
    p = arr.shape[0]
    if p == 1:
        return pl.BlockSpec((1, LANES), lambda i, *_: (0, 0))
    nblk = p // tm
    return pl.BlockSpec((tm, LANES), lambda i, *_: (i % nblk, 0))


def _ada_kernel(c_ref, w_ref, b_ref, o_ref):
    a = _silu(c_ref[...]).astype(BF)
    o_ref[...] = _dot(a, w_ref[...].astype(BF)) + b_ref[...]


def _ada(c, w, b):
    nl, d, n = w.shape
    m = c.shape[0]
    tn = _pick(n, (1024, 512, 256, 128))
    return pl.pallas_call(
        _ada_kernel,
        grid=(nl, n // tn),
        in_specs=[
            pl.BlockSpec((m, d), lambda l, j: (0, 0)),
            pl.BlockSpec((None, d, tn), lambda l, j: (l, 0, j)),
            pl.BlockSpec((None, 1, tn), lambda l, j: (l, 0, j)),
        ],
        out_specs=pl.BlockSpec((None, m, tn), lambda l, j: (l, 0, j)),
        out_shape=jax.ShapeDtypeStruct((nl, m, n), F32),
        compiler_params=_params("parallel", "parallel"),
        name="ada_params",
    )(c, w, b)


def _ffn_kernel(x_ref, g_ref, sh_ref, sc_ref, wg_ref, wu_ref, wo_ref, xc_ref, gt_ref, o_ref, *rest, nf, tf,
                emit_bf16):
    if emit_bf16:
        wg_out, wu_out, wo_out, h_ref, a_ref = rest
    else:
        h_ref, a_ref = rest
    j = pl.program_id(1)

    @pl.when(j == 0)
    def _():
        h_ref[...] = _modulate(x_ref[...], g_ref[...], sh_ref[...], sc_ref[...]).astype(BF)

    @pl.when(j < nf)
    def _():
        h = h_ref[...]
        wg = wg_ref[...].astype(BF)
        wu = wu_ref[...].astype(BF)
        if emit_bf16:
            wg_out[...] = wg
            wu_out[...] = wu
        gate = _dot(h, wg)
        up = _dot(h, wu)
        a_ref[j] = (_silu(gate) * up).astype(BF)

    @pl.when(j >= nf)
    def _():
        if emit_bf16:
            wo_out[...] = wo_ref[...].astype(BF)
            wo = wo_out
        else:
            wo = wo_ref
        y = _dot(a_ref[0], wo[0:tf, :])
        for c in range(1, nf):
            y = y + _dot(a_ref[c], wo[c * tf:(c + 1) * tf, :])
        o_ref[...] = xc_ref[...] + 0.5 * gt_ref[...] * y


def _ffn(x, g, shift, scale, gate, weights, *, seq_rows):
    emit_bf16 = len(weights) == 4
    m, d = x.shape
    tm = _row_tile(m, seq_rows, (1024, 512, 256, 128))
    if emit_bf16:
        w_in, w_out, layer, half = weights
        f = w_out.shape[2]
        assert m == tm, "the bf16 weight copies are written by a single row tile"
    else:
        wg, wu, wo = weights
        f = wo.shape[0]
    tf = _pick(f, (512, 256, 128))
    tn = _pick(d, (256, 128))
    nf = f // tf
    sh_spec = _mod_spec(shift, tm, seq_rows, d)
    sc_spec = _mod_spec(scale, tm, seq_rows, d)
    col = lambda j: jnp.maximum(j - nf, 0)
    fcol = lambda j: jnp.minimum(j, nf - 1)
    gate_spec = _mod_spec(gate, tm, seq_rows, d, tn=tn, col=col)
    if emit_bf16:
        w_specs = [
            pl.BlockSpec((None, None, d, tf), lambda i, j: (layer, half, 0, fcol(j))),
            pl.BlockSpec((None, None, d, tf), lambda i, j: (layer, half, 0, fcol(j) + nf)),
            pl.BlockSpec((None, None, f, tn), lambda i, j: (layer, half, 0, col(j))),
        ]
        w_args = (w_in, w_in, w_out)
    else:
        w_specs = [
            pl.BlockSpec((d, tf), lambda i, j: (0, fcol(j))),
            pl.BlockSpec((d, tf), lambda i, j: (0, fcol(j))),
            pl.BlockSpec((f, tn), lambda i, j: (0, col(j))),
        ]
        w_args = (wg, wu, wo)
    out_specs = [pl.BlockSpec((tm, tn), lambda i, j: (i, col(j)))]
    out_shape = [jax.ShapeDtypeStruct((m, d), F32)]
    if emit_bf16:
        out_specs += [
            pl.BlockSpec((d, tf), lambda i, j: (0, fcol(j))),
            pl.BlockSpec((d, tf), lambda i, j: (0, fcol(j))),
            pl.BlockSpec((f, tn), lambda i, j: (0, col(j))),
        ]
        out_shape += [jax.ShapeDtypeStruct((d, f), BF), jax.ShapeDtypeStruct((d, f), BF),
                      jax.ShapeDtypeStruct((f, d), BF)]
    outs = pl.pallas_call(
        functools.partial(_ffn_kernel, nf=nf, tf=tf, emit_bf16=emit_bf16),
        grid=(m // tm, nf + d // tn),
        in_specs=[
            pl.BlockSpec((tm, d), lambda i, j: (i, 0)),
            pl.BlockSpec((1, d), lambda i, j: (0, 0)),
            sh_spec, sc_spec,
            *w_specs,
            pl.BlockSpec((tm, tn), lambda i, j: (i, col(j))),
            gate_spec,
        ],
        out_specs=out_specs,
        out_shape=out_shape,
        scratch_shapes=[pltpu.VMEM((tm, d), BF), pltpu.VMEM((nf, tm, tf), BF)],
        compiler_params=_params("parallel", "arbitrary"),
        name="ffn_half",
    )(x, g, shift.arr, scale.arr, *w_args, x, gate.arr)
    return (outs[0], tuple(outs[1:])) if emit_bf16 else outs[0]


def _mod_mm_kernel(x_ref, g_ref, sh_ref, sc_ref, w_ref, *o_refs, out_transposed, ranges):
    h = _modulate(x_ref[...], g_ref[...], sh_ref[...], sc_ref[...]).astype(BF)
    y = _dot_nt(w_ref[...], h) if out_transposed else _dot(h, w_ref[...])
    for o_ref, (lo, hi) in zip(o_refs, ranges):
        o_ref[...] = (y[lo:hi] if out_transposed else y[:, lo:hi]).astype(o_ref.dtype)


def _mod_mm(x, g, shift, scale, w, outs, *, seq_rows, out_transposed=False):
    m, d = x.shape
    tm = _row_tile(m, seq_rows, (512, 256, 128))
    sh_spec = _mod_spec(shift, tm, seq_rows, d)
    sc_spec = _mod_spec(scale, tm, seq_rows, d)
    if out_transposed:
        out_specs = [pl.BlockSpec((hi - lo, tm), lambda i: (0, i)) for _, lo, hi in outs]
        out_shape = [jax.ShapeDtypeStruct((hi - lo, m), dt) for dt, lo, hi in outs]
    else:
        out_specs = [pl.BlockSpec((tm, hi - lo), lambda i: (i, 0)) for _, lo, hi in outs]
        out_shape = [jax.ShapeDtypeStruct((m, hi - lo), dt) for dt, lo, hi in outs]
    return pl.pallas_call(
        functools.partial(_mod_mm_kernel, out_transposed=out_transposed, ranges=[(lo, hi) for _, lo, hi in outs]),
        grid=(m // tm,),
        in_specs=[
            pl.BlockSpec((tm, d), lambda i: (i, 0)),
            pl.BlockSpec((1, d), lambda i: (0, 0)),
            sh_spec, sc_spec,
            _const_spec(w),
        ],
        out_specs=out_specs,
        out_shape=out_shape,
        compiler_params=_params("parallel"),
        name="mod_matmul",
    )(x, g, shift.arr, scale.arr, w)


def _mm_resid_kernel(a_ref, w_ref, x_ref, gt_ref, o_ref, *, a_transposed):
    if a_transposed:
        y = lax.dot_general(a_ref[...], w_ref[...], (((0,), (0,)), ((), ())), preferred_element_type=F32)
    else:
        y = _dot(a_ref[...], w_ref[...])
    o_ref[...] = x_ref[...] + gt_ref[...] * y


def _mm_resid(a, w, x, gate, *, seq_rows, a_transposed=False):
    m, n = x.shape
    k = w.shape[0]
    tm = _row_tile(m, seq_rows, (512, 256, 128))
    return pl.pallas_call(
        functools.partial(_mm_resid_kernel, a_transposed=a_transposed),
        grid=(m // tm,),
        in_specs=[
            pl.BlockSpec((k, tm), lambda i: (0, i)) if a_transposed else pl.BlockSpec((tm, k), lambda i: (i, 0)),
            _const_spec(w),
            pl.BlockSpec((tm, n), lambda i: (i, 0)),
            _mod_spec(gate, tm, seq_rows, n),
        ],
        out_specs=pl.BlockSpec((tm, n), lambda i: (i, 0)),
        out_shape=jax.ShapeDtypeStruct((m, n), F32),
        compiler_params=_params("parallel"),
        name="matmul_residual",
    )(a, w, x, gate.arr)


def _mla_pre_kernel(x_ref, g_ref, sh_ref, sc_ref, cos_ref, sin_ref, cost_ref, sint_ref, w1_ref, qn_ref, kn_ref,
                    w2t_ref, w3_ref, row_ref, kcat_ref, latt_ref, qt_ref, *, ql, kl, nh):
    tm = x_ref.shape[0]
    cos_t = cos_ref[...]
    sin_t = sin_ref[...]
    h = _modulate(x_ref[...], g_ref[...], sh_ref[...], sc_ref[...]).astype(BF)
    p1 = _dot(h, w1_ref[...])
    cq = _rms(p1[:, :ql], qn_ref[...])
    lat = _rms(p1[:, ql:ql + kl], kn_ref[...])
    a = ql + kl
    kr = p1[:, a:a + LANES] * cos_t + p1[:, a + LANES:a + 2 * LANES] * sin_t
    row_ref[:, :kl] = lat
    row_ref[:, kl:] = kr[:, :MLA_ROPE]
    kcat_ref[:, :kl] = lat.astype(BF)
    kcat_ref[:, kl:] = kr.astype(BF)
    latt_ref[...] = lat.T.astype(BF)
    q2t = _dot(w2t_ref[...], cq.T.astype(BF))
    cos_c = cost_ref[...]
    sin_c = sint_ref[...]
    hn = nh * LANES
    for hd in range(nh):
        lo = hd * LANES
        qn_t = q2t[lo:lo + LANES].astype(BF)
        qt_ref[:kl, hd * tm:(hd + 1) * tm] = _dot(w3_ref[hd], qn_t).astype(BF)
        qr_t = q2t[hn + lo:hn + lo + LANES] * cos_c + q2t[2 * hn + lo:2 * hn + lo + LANES] * sin_c
        qt_ref[kl:, hd * tm:(hd + 1) * tm] = qr_t.astype(BF)


def _mla_pre(x, g, shift, scale, cos_t, sin_t, w1, qnorm, knorm, w2t, w3, *, seq_rows):
    m, d = x.shape
    ql = qnorm.shape[1]
    kl = knorm.shape[1]
    nh = w3.shape[0]
    c = kl + LANES
    tm = _row_tile(m, seq_rows, (256, 128))
    sh_spec = _mod_spec(shift, tm, seq_rows, d)
    sc_spec = _mod_spec(scale, tm, seq_rows, d)
    ps = _pos_spec(cos_t, tm)
    cos_c, sin_c = cos_t.T, sin_t.T
    npos = cos_t.shape[0]
    if npos == 1:
        pst = pl.BlockSpec((LANES, 1), lambda i: (0, 0))
    else:
        pst = pl.BlockSpec((LANES, tm), lambda i: (0, i % (npos // tm)))
    return pl.pallas_call(
        functools.partial(_mla_pre_kernel, ql=ql, kl=kl, nh=nh),
        grid=(m // tm,),
        in_specs=[
            pl.BlockSpec((tm, d), lambda i: (i, 0)),
            pl.BlockSpec((1, d), lambda i: (0, 0)),
            sh_spec, sc_spec, ps, ps, pst, pst,
            _const_spec(w1), _const_spec(qnorm), _const_spec(knorm), _const_spec(w2t), _const_spec(w3),
        ],
        out_specs=[
            pl.BlockSpec((tm, kl + MLA_ROPE), lambda i: (i, 0)),
            pl.BlockSpec((tm, c), lambda i: (i, 0)),
            pl.BlockSpec((kl, tm), lambda i: (0, i)),
            pl.BlockSpec((None, c, nh * tm), lambda i: (i, 0, 0)),
        ],
        out_shape=[
            jax.ShapeDtypeStruct((m, kl + MLA_ROPE), F32),
            jax.ShapeDtypeStruct((m, c), BF),
            jax.ShapeDtypeStruct((kl, m), BF),
            jax.ShapeDtypeStruct((m // tm, c, nh * tm), BF),
        ],
        compiler_params=_params("parallel"),
        name="mla_project",
    )(x, g, shift.arr, scale.arr, cos_t, sin_t, cos_c, sin_c, w1, qnorm, knorm, w2t, w3)


def _mla_flash_kernel(it_ref, jt_ref, qt_ref, k_ref, vt_ref, wuvt_ref, o_ref, m_sc, l_sc, acc_sc, *,
                      tq, tk, nh, exp2_scale):
    t = pl.program_id(1)
    i = it_ref[t]
    j = jt_ref[t]
    cols = nh * tq
    j_last = ((i + 1) * tq - 1) // tk

    @pl.when(j == 0)
    def _():
        m_sc[...] = jnp.full(m_sc.shape, NEG_INF, F32)
        l_sc[...] = jnp.zeros(l_sc.shape, F32)
        acc_sc[...] = jnp.zeros(acc_sc.shape, F32)

    def step(masked, nkeys):
        s = _dot(k_ref[:nkeys], qt_ref[...])
        if masked:
            kpos = j * tk + lax.broadcasted_iota(jnp.int32, (nkeys, cols), 0)
            qpos = i * tq + (lax.broadcasted_iota(jnp.int32, (nkeys, cols), 1) & (tq - 1))
            s = jnp.where(kpos <= qpos, s, NEG_INF)
        m_old = m_sc[...]
        m_new = jnp.maximum(m_old, jnp.max(s, axis=0, keepdims=True))
        alpha = jnp.exp2((m_old - m_new) * exp2_scale)
        p = jnp.exp2((s - m_new) * exp2_scale)
        l_sc[...] = alpha * l_sc[...] + jnp.sum(p, axis=0, keepdims=True)
        acc_sc[...] = alpha * acc_sc[...] + _dot(vt_ref[:, :nkeys], p.astype(BF))
        m_sc[...] = m_new

    needs_mask = (j + 1) * tk - 1 > i * tq
    half_block = (i + 1) * tq <= j * tk + tk // 2

    @pl.when(needs_mask & half_block)
    def _():
        step(True, tk // 2)

    @pl.when(needs_mask & jnp.logical_not(half_block))
    def _():
        step(True, tk)

    @pl.when(jnp.logical_not(needs_mask))
    def _():
        step(False, tk)

    @pl.when(j == j_last)
    def _():
        o_t = (acc_sc[...] * (1.0 / l_sc[...])).astype(BF)
        for hd in range(nh):
            o_ref[hd * MLA_V:(hd + 1) * MLA_V, :] = _dot(wuvt_ref[hd], o_t[:, hd * tq:(hd + 1) * tq]).astype(BF)


def _mla_flash(qt, kcat, latt, wuvt, *, batch, seq, scale):
    nblk, c, cols = qt.shape
    nh, _, kl = wuvt.shape
    tq = cols // nh
    m = nblk * tq
    tk = _pick(seq, (512, 256, 128))
    nq = seq // tq
    nk = seq // tk
    sched = [(i, j) for i in range(nq) for j in range(((i + 1) * tq - 1) // tk + 1)]
    it = jnp.asarray(np.array([s[0] for s in sched], np.int32))
    jt = jnp.asarray(np.array([s[1] for s in sched], np.int32))
    grid_spec = pltpu.PrefetchScalarGridSpec(
        num_scalar_prefetch=2,
        grid=(batch, len(sched)),
        in_specs=[
            pl.BlockSpec((None, c, cols), lambda b, t, it_ref, jt_ref: (b * nq + it_ref[t], 0, 0)),
            pl.BlockSpec((tk, c), lambda b, t, it_ref, jt_ref: (b * nk + jt_ref[t], 0)),
            pl.BlockSpec((kl, tk), lambda b, t, it_ref, jt_ref: (0, b * nk + jt_ref[t])),
            pl.BlockSpec(wuvt.shape, lambda b, t, it_ref, jt_ref: (0, 0, 0), pipeline_mode=pl.Buffered(1)),
        ],
        out_specs=pl.BlockSpec((nh * MLA_V, tq), lambda b, t, it_ref, jt_ref: (0, b * nq + it_ref[t])),
        scratch_shapes=[
            pltpu.VMEM((1, cols), F32),
            pltpu.VMEM((1, cols), F32),
            pltpu.VMEM((kl, cols), F32),
        ],
    )
    return pl.pallas_call(
        functools.partial(_mla_flash_kernel, tq=tq, tk=tk, nh=nh, exp2_scale=scale * math.log2(math.e)),
        grid_spec=grid_spec,
        out_shape=jax.ShapeDtypeStruct((nh * MLA_V, m), BF),
        compiler_params=_params("parallel", "arbitrary"),
        name="mla_prompt_attention",
    )(it, jt, qt, kcat, latt, wuvt)


def _mla_decode_kernel(pt_ref, q_ref, new_ref, cache_ref, o_ref, pages, sems, kbuf, m_sc, l_sc, acc_sc, *,
                       npg, kl, scale, layer, chunks_per_seq, n_chunks):
    c = pl.program_id(1)
    t = pl.program_id(0) * chunks_per_seq + c

    def chunk_copy(chunk, slot, k):
        page = pt_ref[chunk * npg + k]
        return pltpu.make_async_copy(cache_ref.at[page, layer], pages.at[slot, k], sems.at[slot])

    def start_chunk(chunk):
        slot = chunk % DECODE_SLOTS
        for k in range(npg):
            chunk_copy(chunk, slot, k).start()

    @pl.when(t == 0)
    def _():
        for first in range(min(DECODE_SLOTS - 1, n_chunks)):
            start_chunk(first)

    @pl.when(t + (DECODE_SLOTS - 1) < n_chunks)
    def _():
        start_chunk(t + (DECODE_SLOTS - 1))

    slot = t % DECODE_SLOTS
    for k in range(npg):
        chunk_copy(t, slot, k).wait()

    @pl.when(c == 0)
    def _():
        m_sc[...] = jnp.full(m_sc.shape, NEG_INF, F32)
        l_sc[...] = jnp.zeros(l_sc.shape, F32)
        acc_sc[...] = jnp.zeros(acc_sc.shape, F32)

    for k in range(npg):
        kbuf[:, k * PAGE_SIZE:(k + 1) * PAGE_SIZE] = pages[slot, k].astype(BF)

    q = q_ref[...]
    keys_t = kbuf[...]
    s = (_dot(q[:, :kl], keys_t[:kl]) + _dot(q[:, kl:kl + MLA_ROPE], keys_t[kl:])) * scale
    m_old = m_sc[...]
    m_new = jnp.maximum(m_old, jnp.max(s, axis=-1, keepdims=True))
    alpha = jnp.exp(m_old - m_new)
    p = jnp.exp(s - m_new)
    l_sc[...] = alpha * l_sc[...] + jnp.sum(p, axis=-1, keepdims=True)
    acc_sc[...] = alpha * acc_sc[...] + _dot_nt(p.astype(BF), keys_t[:kl])
    m_sc[...] = m_new

    @pl.when(c == pl.num_programs(1) - 1)
    def _():
        new = new_ref[...].astype(F32)
        s_new = jnp.sum(q.astype(F32) * new, axis=-1, keepdims=True) * scale
        m_old = m_sc[...]
        m_new = jnp.maximum(m_old, s_new)
        alpha = jnp.exp(m_old - m_new)
        p_new = jnp.exp(s_new - m_new)
        l_new = alpha * l_sc[...] + p_new
        acc = alpha * acc_sc[...] + p_new.astype(BF).astype(F32) * new[:, :kl]
        o_ref[...] = (acc * (1.0 / l_new)).astype(BF)


def _mla_decode(page_table, qcat, kcat, cache_t, *, layer, scale):
    nb, nh, c = qcat.shape
    lat = cache_t.shape[2]
    kl = lat - MLA_ROPE
    n_pages = page_table.shape[1]
    npg = _pick(n_pages, (32, 16, 8, 4, 2, 1))
    chunks_per_seq = n_pages // npg
    pt = page_table.reshape(-1)
    grid_spec = pltpu.PrefetchScalarGridSpec(
        num_scalar_prefetch=1,
        grid=(nb, chunks_per_seq),
        in_specs=[
            pl.BlockSpec((None, nh, c), lambda b, ch, pt_ref: (b, 0, 0)),
            pl.BlockSpec((None, 1, c), lambda b, ch, pt_ref: (b, 0, 0)),
            pl.BlockSpec(memory_space=pl.ANY),
        ],
        out_specs=pl.BlockSpec((None, nh, kl), lambda b, ch, pt_ref: (b, 0, 0)),
        scratch_shapes=[
            pltpu.VMEM((DECODE_SLOTS, npg, lat, PAGE_SIZE), F32),
            pltpu.SemaphoreType.DMA((DECODE_SLOTS,)),
            pltpu.VMEM((lat, npg * PAGE_SIZE), BF),
            pltpu.VMEM((nh, 1), F32),
            pltpu.VMEM((nh, 1), F32),
            pltpu.VMEM((nh, kl), F32),
        ],
    )
    return pl.pallas_call(
        functools.partial(_mla_decode_kernel, npg=npg, kl=kl, scale=scale, layer=layer,
                          chunks_per_seq=chunks_per_seq, n_chunks=nb * chunks_per_seq),
        grid_spec=grid_spec,
        out_shape=jax.ShapeDtypeStruct((nb, nh, kl), BF),
        compiler_params=_params("arbitrary", "arbitrary"),
        name="mla_sample_attention",
    )(pt, qcat, kcat, cache_t)


def _uv_kernel(o_ref, wuv_ref, out_ref):
    nh = o_ref.shape[0]
    for hd in range(nh):
        out_ref[:, hd * MLA_V:(hd + 1) * MLA_V] = _dot(o_ref[hd], wuv_ref[hd]).astype(BF)


def _uv(o_lat, wuv):
    nh, m, kl = o_lat.shape
    return pl.pallas_call(
        _uv_kernel,
        grid=(1,),
        in_specs=[
            pl.BlockSpec(o_lat.shape, lambda i: (0, 0, 0)),
            pl.BlockSpec(wuv.shape, lambda i: (0, 0, 0)),
        ],
        out_specs=pl.BlockSpec((m, nh * MLA_V), lambda i: (0, 0)),
        out_shape=jax.ShapeDtypeStruct((m, nh * MLA_V), BF),
        compiler_params=_params("arbitrary"),
        name="mla_value_up",
    )(o_lat, wuv)


def _swa_prompt_kernel(qt_ref, kvp_ref, kvc_ref, bias_ref, mask_ref, sink_ref, o_ref, *, n_kv, scale):
    kvw = n_kv * LANES
    pairs_per_kv = SWA_GROUP // 2
    mask = mask_ref[...] > 0.5
    nkeys = 2 * BLOCK
    for c in range(n_kv):
        def sect(s):
            lo = s * kvw + c * LANES
            return jnp.concatenate([kvp_ref[:, lo:lo + LANES], kvc_ref[:, lo:lo + LANES]], axis=0)
        k2 = jnp.concatenate([sect(0), sect(1)], axis=0)
        v2 = jnp.concatenate([sect(2), sect(3)], axis=0)
        pr0 = c * pairs_per_kv
        qc = jnp.concatenate([qt_ref[(pr0 + pp) * LANES:(pr0 + pp + 1) * LANES, :] for pp in range(pairs_per_kv)],
                             axis=1)
        s = _dot(k2, qc) * scale + bias_ref[c]
        s = jnp.where(mask, s, NEG_INF)
        halves = []
        for hf in range(2):
            sh = s[hf * nkeys:(hf + 1) * nkeys]
            sink = sink_ref[c, hf]
            m = jnp.maximum(jnp.max(sh, axis=0, keepdims=True), sink)
            e = jnp.exp(sh - m)
            den = jnp.sum(e, axis=0, keepdims=True) + jnp.exp(sink - m)
            halves.append((e * (1.0 / den)).astype(BF))
        p = jnp.concatenate(halves, axis=0)
        o_t = lax.dot_general(v2, p, (((0,), (0,)), ((), ())), preferred_element_type=F32).astype(BF)
        for pp in range(pairs_per_kv):
            o_ref[(pr0 + pp) * LANES:(pr0 + pp + 1) * LANES, :] = o_t[:, pp * BLOCK:(pp + 1) * BLOCK]


def _swa_prompt(qt, kv4, bias_t, mask_t, sink_cols, *, batch, seq, n_kv, scale):
    dq, m = qt.shape
    nb = seq // BLOCK
    w4 = kv4.shape[1]
    return pl.pallas_call(
        functools.partial(_swa_prompt_kernel, n_kv=n_kv, scale=scale),
        grid=(batch, nb),
        in_specs=[
            pl.BlockSpec((dq, BLOCK), lambda b, j: (0, b * nb + j)),
            pl.BlockSpec((BLOCK, w4), lambda b, j: (b * nb + jnp.maximum(j - 1, 0), 0)),
            pl.BlockSpec((BLOCK, w4), lambda b, j: (b * nb + j, 0)),
            _const_spec(bias_t),
            pl.BlockSpec((None,) + mask_t.shape[1:], lambda b, j: (jnp.minimum(j, 1), 0, 0)),
            _const_spec(sink_cols),
        ],
        out_specs=pl.BlockSpec((dq, BLOCK), lambda b, j: (0, b * nb + j)),
        out_shape=jax.ShapeDtypeStruct((dq, m), BF),
        compiler_params=_params("parallel", "arbitrary"),
        name="swa_prompt_attention",
    )(qt, kv4, kv4, bias_t, mask_t, sink_cols)


def _swa_sample_kernel(q_ref, kt_ref, vt_ref, kn_ref, vn_ref, bias_ref, bias_new_ref, sink_ref, o_ref, *, n_kv, scale):
    for c in range(n_kv):
        q = q_ref[:, c * SWA_GROUP:(c + 1) * SWA_GROUP, :]
        kt = kt_ref[:, c].astype(BF)
        vt = vt_ref[:, c].astype(BF)
        kn = kn_ref[:, c].astype(BF).astype(F32)[:, None, :]
        vn = vn_ref[:, c].astype(BF).astype(F32)[:, None, :]
        s = jnp.einsum("bgd,bdk->bgk", q, kt, preferred_element_type=F32) * scale + bias_ref[c][None]
        s_new = jnp.sum(q.astype(F32) * kn, axis=-1, keepdims=True) * scale + bias_new_ref[c][None]
        sink = sink_ref[c][None]
        m = jnp.maximum(jnp.maximum(jnp.max(s, axis=-1, keepdims=True), s_new), sink)
        e = jnp.exp(s - m)
        e_new = jnp.exp(s_new - m)
        inv = 1.0 / (jnp.sum(e, axis=-1, keepdims=True) + e_new + jnp.exp(sink - m))
        p = (e * inv).astype(BF)
        p_new = (e_new * inv).astype(BF).astype(F32)
        o = jnp.einsum("bgk,bdk->bgd", p, vt, preferred_element_type=F32) + p_new * vn
        o_ref[:, c * SWA_GROUP:(c + 1) * SWA_GROUP, :] = o.astype(BF)


def _swa_sample(q, kt, vt, k_new, v_new, bias, bias_new, sinks, *, scale):
    nb, nh, dh = q.shape
    n_kv, _, w = kt.shape[1:]
    bb = _pick(nb, (16, 8))
    return pl.pallas_call(
        functools.partial(_swa_sample_kernel, n_kv=n_kv, scale=scale),
        grid=(nb // bb,),
        in_specs=[
            pl.BlockSpec((bb, nh, dh), lambda i: (i, 0, 0)),
            pl.BlockSpec((bb, n_kv, dh, w), lambda i: (i, 0, 0, 0)),
            pl.BlockSpec((bb, n_kv, dh, w), lambda i: (i, 0, 0, 0)),
            pl.BlockSpec((bb, n_kv, dh), lambda i: (i, 0, 0)),
            pl.BlockSpec((bb, n_kv, dh), lambda i: (i, 0, 0)),
            pl.BlockSpec(bias.shape, lambda i: (0, 0, 0)),
            pl.BlockSpec(bias_new.shape, lambda i: (0, 0, 0)),
            pl.BlockSpec(sinks.shape, lambda i: (0, 0, 0)),
        ],
        out_specs=pl.BlockSpec((bb, nh, dh), lambda i: (i, 0, 0)),
        out_shape=jax.ShapeDtypeStruct((nb, nh, dh), BF),
        compiler_params=_params("parallel"),
        name="swa_sample_attention",
    )(q, kt, vt, k_new, v_new, bias, bias_new, sinks)


def _final_norm_kernel(x_ref, g_ref, o_ref):
    o_ref[...] = _rms(x_ref[...], g_ref[...])


def _final_norm(x, g):
    m, d = x.shape
    tm = _pick(m, (512, 256, 128))
    return pl.pallas_call(
        _final_norm_kernel,
        grid=(m // tm,),
        in_specs=[pl.BlockSpec((tm, d), lambda i: (i, 0)), pl.BlockSpec((1, d), lambda i: (0, 0))],
        out_specs=pl.BlockSpec((tm, d), lambda i: (i, 0)),
        out_shape=jax.ShapeDtypeStruct((m, d), F32),
        compiler_params=_params("parallel"),
        name="final_norm",
    )(x, g)


def _rope_tables(pos):
    half = MLA_ROPE // 2
    inv = ROPE_THETA ** (-jnp.arange(half, dtype=F32) / half)
    ang = pos.astype(F32)[:, None] * inv
    cos, sin = jnp.cos(ang), jnp.sin(ang)
    pad = jnp.zeros((pos.shape[0], LANES - MLA_ROPE), F32)
    return jnp.concatenate([cos, cos, pad], axis=1), jnp.concatenate([-sin, sin, pad], axis=1)


def _swap_halves(w):
    half = w.shape[-1] // 2
    return jnp.concatenate([w[..., half:], w[..., :half]], axis=-1)


def _pad_lanes(w):
    return jnp.pad(w, [(0, 0)] * (w.ndim - 1) + [(0, LANES - w.shape[-1])])


def _t5_bucket(dist):
    max_exact = NUM_BUCKETS // 2
    n = jnp.maximum(dist, 0)
    nf = jnp.maximum(n, 1).astype(F32)
    large = max_exact + (jnp.log(nf / max_exact) / math.log(MAX_DISTANCE / max_exact)
                         * (NUM_BUCKETS - max_exact)).astype(jnp.int32)
    large = jnp.minimum(large, NUM_BUCKETS - 1)
    return jnp.where(n < max_exact, n, large)


def kernel(x_prompt, x_sample, c_prompt, c_sample, cache_mla, state_swa_k, state_swa_v, page_table, ada_w, ada_b, norm_g, ffn_w_in, ffn_w_out, mla_w_dq, mla_q_norm, mla_w_uq, mla_w_dkv, mla_kv_norm, mla_w_uk, mla_w_uv, mla_w_o, kv_ada_w, kv_ada_b, kv_norm_g, kv_w, swa_w_q, swa_w_o, swa_sinks, rel_bias_table, final_norm_g):
    batch, seq, d = x_prompt.shape
    dec_batch, dec_seq, _ = x_sample.shape
    assert dec_seq == 1
    depth = ada_w.shape[0]
    n_a = mla_w_dq.shape[0]
    n_pages = page_table.shape[1]
    past_len = n_pages * PAGE_SIZE
    ql = mla_w_dq.shape[2]
    kl = mla_kv_norm.shape[1]
    nh = mla_w_uk.shape[2]
    swa_heads = swa_w_q.shape[2] // SWA_HEAD_DIM
    n_kv = swa_heads // SWA_GROUP
    win_s = state_swa_k.shape[1]
    assert win_s == WINDOW and seq % BLOCK == 0
    mla_scale = float((MLA_NOPE + MLA_ROPE) ** -0.5)
    swa_scale = float(SWA_HEAD_DIM ** -0.5)
    mp = batch * seq

    c_all = jnp.concatenate([c_sample, c_prompt], axis=0)
    n_seq = c_all.shape[0]
    c_all = jnp.pad(c_all, ((0, (-n_seq) % 16), (0, 0)))
    mod_s3 = _ada(c_all, ada_w, ada_b[:, None, :])
    kvmod_s3 = _ada(c_all, kv_ada_w[None], kv_ada_b[None, None, :])
    mod_p4 = mod_s3[:, dec_batch:dec_batch + batch, None, :]
    kvmod_p4 = kvmod_s3[:, dec_batch:dec_batch + batch, None, :]

    cos_p, sin_p = _rope_tables(jnp.arange(seq, dtype=jnp.int32))
    cos_s, sin_s = _rope_tables(jnp.full((1,), past_len, jnp.int32))

    qi = jnp.arange(BLOCK, dtype=jnp.int32)
    km = jnp.arange(2 * BLOCK, dtype=jnp.int32) - BLOCK
    d_p = qi[:, None] - km[None, :]
    win_mask = (d_p >= 0) & (d_p <= WINDOW)
    onehot_p = (_t5_bucket(d_p)[:, :, None] == jnp.arange(NUM_BUCKETS, dtype=jnp.int32)).astype(F32)
    bias_p = jnp.einsum("qkb,bh->hqk", onehot_p, rel_bias_table.astype(F32),
                        precision=lax.Precision.HIGHEST)
    bias_pairs = bias_p.reshape(swa_heads // 2, 2, BLOCK, 2 * BLOCK).transpose(0, 2, 1, 3)
    pairs_per_kv = SWA_GROUP // 2
    bias_t = jnp.transpose(bias_pairs.reshape(n_kv, pairs_per_kv * BLOCK, 4 * BLOCK), (0, 2, 1))
    mask_first = win_mask & (km >= 0)[None, :]
    mask_t = jnp.stack([jnp.tile(mask_first, (pairs_per_kv, 2)).T,
                        jnp.tile(win_mask, (pairs_per_kv, 2)).T]).astype(F32)
    d_s = jnp.concatenate([jnp.arange(win_s, 0, -1, dtype=jnp.int32), jnp.zeros((1,), jnp.int32)])
    bias_s = jnp.transpose(rel_bias_table[_t5_bucket(d_s)], (1, 0)).astype(F32)
    bias_s_win = bias_s[:, :win_s].reshape(n_kv, SWA_GROUP, win_s)
    bias_s_new = bias_s[:, win_s:].reshape(n_kv, SWA_GROUP, 1)
    state_kt = jnp.transpose(state_swa_k, (0, 2, 3, 1))
    state_vt = jnp.transpose(state_swa_v, (0, 2, 3, 1))

    cache_t = jnp.transpose(cache_mla, (0, 1, 3, 2))

    x_p = x_prompt.reshape(mp, d)
    x_s = x_sample.reshape(dec_batch, d)
    rows_p, rows_s = [], []
    kv4_p = k_p = v_p = k_new = v_new = None

    for l in range(depth):
        if l == n_a:
            kv_k = kv_w[:, :n_kv * SWA_HEAD_DIM].reshape(d, n_kv, SWA_HEAD_DIM)
            kv_v = kv_w[:, n_kv * SWA_HEAD_DIM:].reshape(d, n_kv, SWA_HEAD_DIM)
            zero = jnp.zeros_like(kv_k)
            lo = lambda w: jnp.concatenate([w, zero], axis=-1).reshape(d, n_kv * LANES)
            hi = lambda w: jnp.concatenate([zero, w], axis=-1).reshape(d, n_kv * LANES)
            nkv = 2 * n_kv * SWA_HEAD_DIM
            w4 = 4 * n_kv * LANES
            w_sh = jnp.concatenate([lo(kv_k), hi(kv_k), lo(kv_v), hi(kv_v), kv_w], axis=1).astype(BF)
            kvg = kv_norm_g[None]
            kv4_p, kv_f32_p = _mod_mm(x_p, kvg, _Mod(kvmod_p4, 0, 0), _Mod(kvmod_p4, 0, 1), w_sh,
                                      [(BF, 0, w4), (F32, w4, w4 + nkv)], seq_rows=seq)
            k_p = kv_f32_p[:, :nkv // 2].reshape(batch, seq, n_kv, SWA_HEAD_DIM)
            v_p = kv_f32_p[:, nkv // 2:].reshape(batch, seq, n_kv, SWA_HEAD_DIM)
            (kv_f32_s,) = _mod_mm(x_s, kvg, _Mod(kvmod_s3, 0, 0), _Mod(kvmod_s3, 0, 1), w_sh[:, w4:],
                                  [(F32, 0, nkv)], seq_rows=1)
            k_new = kv_f32_s[:, :nkv // 2].reshape(dec_batch, n_kv, SWA_HEAD_DIM)
            v_new = kv_f32_s[:, nkv // 2:].reshape(dec_batch, n_kv, SWA_HEAD_DIM)

        g = norm_g[l]
        mp_l = [_Mod(mod_p4, l, n) for n in range(3 * N_SUB)]
        ms_l = [_Mod(mod_s3, l, n) for n in range(3 * N_SUB)]

        x_s, w_bf = _ffn(x_s, g[0][None], ms_l[0], ms_l[1], ms_l[2], (ffn_w_in, ffn_w_out, l, 0), seq_rows=1)
        x_p = _ffn(x_p, g[0][None], mp_l[0], mp_l[1], mp_l[2], w_bf, seq_rows=seq)

        if l < n_a:
            a = l
            dkv = mla_w_dkv[a]
            w1 = jnp.concatenate([mla_w_dq[a], dkv[:, :kl], _pad_lanes(dkv[:, kl:]),
                                  _pad_lanes(_swap_halves(dkv[:, kl:]))], axis=1).astype(BF)
            wuq = mla_w_uq[a].reshape(ql, nh, MLA_NOPE + MLA_ROPE)
            wuq_r = wuq[:, :, MLA_NOPE:]
            w2t = jnp.concatenate([wuq[:, :, :MLA_NOPE].reshape(ql, nh * MLA_NOPE),
                                   _pad_lanes(wuq_r).reshape(ql, nh * LANES),
                                   _pad_lanes(_swap_halves(wuq_r)).reshape(ql, nh * LANES)], axis=1).T.astype(BF)
            w3 = jnp.transpose(mla_w_uk[a], (1, 0, 2)).astype(BF)
            wuv = jnp.transpose(mla_w_uv[a], (1, 0, 2)).astype(BF)
            wuvt = jnp.transpose(mla_w_uv[a], (1, 2, 0)).astype(BF)
            w_o = mla_w_o[a].astype(BF)
            qn, kn = mla_q_norm[a][None], mla_kv_norm[a][None]

            row_p, kcat_p, latt_p, qt_p = _mla_pre(x_p, g[1][None], mp_l[3], mp_l[4], cos_p, sin_p,
                                                   w1, qn, kn, w2t, w3, seq_rows=seq)
            oh_t_p = _mla_flash(qt_p, kcat_p, latt_p, wuvt, batch=batch, seq=seq, scale=mla_scale)
            x_p = _mm_resid(oh_t_p, w_o, x_p, mp_l[5], seq_rows=seq, a_transposed=True)

            row_s, kcat_s, _, qt_s = _mla_pre(x_s, g[1][None], ms_l[3], ms_l[4], cos_s, sin_s,
                                              w1, qn, kn, w2t, w3, seq_rows=1)
            q_s = jnp.transpose(qt_s.reshape(qt_s.shape[0], kl + LANES, nh, -1), (0, 3, 2, 1))
            q_s = q_s.reshape(dec_batch, nh, kl + LANES)
            o_lat_s = _mla_decode(page_table, q_s, kcat_s[:, None, :],
                                  cache_t, layer=a, scale=mla_scale)
            oh_s = _uv(jnp.transpose(o_lat_s, (1, 0, 2)), wuv)
            x_s = _mm_resid(oh_s, w_o, x_s, ms_l[5], seq_rows=1)
            rows_p.append(row_p.reshape(batch, seq, kl + MLA_ROPE))
            rows_s.append(row_s.reshape(dec_batch, 1, kl + MLA_ROPE))
        else:
            b = l - n_a
            w_q = swa_w_q[b].astype(BF)
            w_o = swa_w_o[b].astype(BF)
            nq_cols = swa_heads * SWA_HEAD_DIM
            (qt_p,) = _mod_mm(x_p, g[1][None], mp_l[3], mp_l[4], swa_w_q[b].T.astype(BF), [(BF, 0, nq_cols)],
                              seq_rows=seq, out_transposed=True)
            sink_cols = jnp.repeat(swa_sinks[b].reshape(n_kv, pairs_per_kv, 2).transpose(0, 2, 1), BLOCK,
                                   axis=2)[:, :, None, :]
            ot_p = _swa_prompt(qt_p, kv4_p, bias_t, mask_t, sink_cols, batch=batch, seq=seq, n_kv=n_kv,
                               scale=swa_scale)
            x_p = _mm_resid(ot_p, w_o, x_p, mp_l[5], seq_rows=seq, a_transposed=True)

            (q_s,) = _mod_mm(x_s, g[1][None], ms_l[3], ms_l[4], w_q, [(BF, 0, nq_cols)], seq_rows=1)
            o_s = _swa_sample(q_s.reshape(dec_batch, swa_heads, SWA_HEAD_DIM), state_kt, state_vt, k_new, v_new,
                              bias_s_win, bias_s_new, swa_sinks[b].reshape(n_kv, SWA_GROUP, 1), scale=swa_scale)
            x_s = _mm_resid(o_s.reshape(dec_batch, swa_heads * SWA_HEAD_DIM), w_o, x_s, ms_l[5], seq_rows=1)

        x_s, w_bf = _ffn(x_s, g[2][None], ms_l[6], ms_l[7], ms_l[8], (ffn_w_in, ffn_w_out, l, 1), seq_rows=1)
        x_p = _ffn(x_p, g[2][None], mp_l[6], mp_l[7], mp_l[8], w_bf, seq_rows=seq)

    y_prompt = _final_norm(x_p, final_norm_g[None]).reshape(batch, seq, d)
    y_sample = _final_norm(x_s, final_norm_g[None]).reshape(dec_batch, 1, d)
    new_mla_prompt = jnp.stack(rows_p, axis=1)
    new_mla_sample = jnp.stack(rows_s, axis=1)
    win_p = min(WINDOW, seq)
    shift_in = lambda st, new: jnp.transpose(jnp.concatenate([st[..., 1:], new[..., None]], axis=-1), (0, 3, 1, 2))
    return (y_prompt, y_sample, new_mla_prompt, new_mla_sample,
            k_p[:, seq - win_p:], v_p[:, seq - win_p:],
            shift_in(state_kt, k_new), shift_in(state_vt, v_new))
```

```python
import functools
import math
from typing import NamedTuple

import jax
import jax.numpy as jnp
import numpy as np
from jax import lax
from jax.experimental import pallas as pl
from jax.experimental.pallas import tpu as pltpu

EPS = 1e-6
NEG_INF = -1e30
ROPE_THETA = 10000.0
PAGE_SIZE = 128
WINDOW = 128
BLOCK = 128
NUM_BUCKETS = 32
MAX_DISTANCE = 128
MLA_NOPE = 128
MLA_ROPE = 64
MLA_V = 128
SWA_HEAD_DIM = 64
SWA_GROUP = 8
N_SUB = 3

DECODE_SLOTS = 3
LANES = 128
V7X_VMEM_LIMIT_BYTES = 56 * 1024 * 1024

BF = jnp.bfloat16
F32 = jnp.float32


def _params(*sem):
    return pltpu.CompilerParams(dimension_semantics=sem, vmem_limit_bytes=V7X_VMEM_LIMIT_BYTES)


def _pick(n, prefs):
    for p in prefs:
        if p <= n and n % p == 0:
            return p
    return n


def _row_tile(m, seq_rows, prefs):
    return _pick(m if seq_rows == 1 else seq_rows, prefs)


def _const_spec(arr):
    nd = arr.ndim
    return pl.BlockSpec(arr.shape, lambda *_: (0,) * nd, pipeline_mode=pl.Buffered(1))


def _silu(x):
    return x * (1.0 / (1.0 + jnp.exp(-x)))


def _rms(x, g):
    return x * lax.rsqrt(jnp.mean(x * x, axis=-1, keepdims=True) + EPS) * g


def _modulate(x, g, shift, scale):
    return _rms(x, g) * (1.0 + scale) + shift


def _dot(a, b):
    return jnp.dot(a, b, preferred_element_type=F32)


def _dot_nt(a, b):
    return lax.dot_general(a, b, (((1,), (1,)), ((), ())), preferred_element_type=F32)


class _Mod(NamedTuple):
    arr: jax.Array
    layer: int
    n: int


def _mod_spec(mod, tm, seq_rows, d, tn=None, col=None):
    tn = d if tn is None else tn
    base = mod.n * (d // tn)
    cidx = (lambda rest: base + col(rest[0])) if col is not None else (lambda rest: base)
    if mod.arr.ndim == 4:
        return pl.BlockSpec((None, None, 1, tn), lambda i, *rest: (mod.layer, (i * tm) // seq_rows, 0, cidx(rest)))
    return pl.BlockSpec((None, tm, tn), lambda i, *rest: (mod.layer, i, cidx(rest)))


def _pos_spec(arr, tm):
    p = arr.shape[0]
    if p == 1:
        return pl.BlockSpec((1, LANES), lambda i, *_: (0, 0))
    nblk = p // tm
    return pl.BlockSpec((tm, LANES), lambda i, *_: (i % nblk, 0))


def _ada_kernel(c_ref, w_ref, b_ref, o_ref):
    a = _silu(c_ref[...]).astype(BF)
    o_ref[...] = _dot(a, w_ref[...].astype(BF)) + b_ref[...]


def _ada(c, w, b):
    nl, d, n = w.shape
    m = c.shape[0]
    tn = _pick(n, (1024, 512, 256, 128))
    return pl.pallas_call(
        _ada_kernel,
        grid=(nl, n // tn),
        in_specs=[
            pl.BlockSpec((m, d), lambda l, j: (0, 0)),
            pl.BlockSpec((None, d, tn), lambda l, j: (l, 0, j)),
            pl.BlockSpec((None, 1, tn), lambda l, j: (l, 0, j)),
        ],
        out_specs=pl.BlockSpec((None, m, tn), lambda l, j: (l, 0, j)),
        out_shape=jax.ShapeDtypeStruct((nl, m, n), F32),
        compiler_params=_params("parallel", "parallel"),
        name="ada_params",
    )(c, w, b)


def _ffn_kernel(x_ref, g_ref, sh_ref, sc_ref, wg_ref, wu_ref, wo_ref, xc_ref, gt_ref, o_ref, *rest, nf, tf,
                emit_bf16):
    if emit_bf16:
        wg_out, wu_out, wo_out, h_ref, a_ref = rest
    else:
        h_ref, a_ref = rest
    j = pl.program_id(1)

    @pl.when(j == 0)
    def _():
        h_ref[...] = _modulate(x_ref[...], g_ref[...], sh_ref[...], sc_ref[...]).astype(BF)

    @pl.when(j < nf)
    def _():
        h = h_ref[...]
        wg = wg_ref[...].astype(BF)
        wu = wu_ref[...].astype(BF)
        if emit_bf16:
            wg_out[...] = wg
            wu_out[...] = wu
        gate = _dot(h, wg)
        up = _dot(h, wu)
        a_ref[j] = (_silu(gate) * up).astype(BF)

    @pl.when(j >= nf)
    def _():
        if emit_bf16:
            wo_out[...] = wo_ref[...].astype(BF)
            wo = wo_out
        else:
            wo = wo_ref
        y = _dot(a_ref[0], wo[0:tf, :])
        for c in range(1, nf):
            y = y + _dot(a_ref[c], wo[c * tf:(c + 1) * tf, :])
        o_ref[...] = xc_ref[...] + 0.5 * gt_ref[...] * y


def _ffn(x, g, shift, scale, gate, weights, *, seq_rows):
    emit_bf16 = len(weights) == 4
    m, d = x.shape
    tm = _row_tile(m, seq_rows, (1024, 512, 256, 128))
    if emit_bf16:
        w_in, w_out, layer, half = weights
        f = w_out.shape[2]
        assert m == tm, "the bf16 weight copies are written by a single row tile"
    else:
        wg, wu, wo = weights
        f = wo.shape[0]
    tf = _pick(f, (512, 256, 128))
    tn = _pick(d, (256, 128))
    nf = f // tf
    sh_spec = _mod_spec(shift, tm, seq_rows, d)
    sc_spec = _mod_spec(scale, tm, seq_rows, d)
    col = lambda j: jnp.maximum(j - nf, 0)
    fcol = lambda j: jnp.minimum(j, nf - 1)
    gate_spec = _mod_spec(gate, tm, seq_rows, d, tn=tn, col=col)
    if emit_bf16:
        w_specs = [
            pl.BlockSpec((None, None, d, tf), lambda i, j: (layer, half, 0, fcol(j))),
            pl.BlockSpec((None, None, d, tf), lambda i, j: (layer, half, 0, fcol(j) + nf)),
            pl.BlockSpec((None, None, f, tn), lambda i, j: (layer, half, 0, col(j))),
        ]
        w_args = (w_in, w_in, w_out)
    else:
        w_specs = [
            pl.BlockSpec((d, tf), lambda i, j: (0, fcol(j))),
            pl.BlockSpec((d, tf), lambda i, j: (0, fcol(j))),
            pl.BlockSpec((f, tn), lambda i, j: (0, col(j))),
        ]
        w_args = (wg, wu, wo)
    out_specs = [pl.BlockSpec((tm, tn), lambda i, j: (i, col(j)))]
    out_shape = [jax.ShapeDtypeStruct((m, d), F32)]
    if emit_bf16:
        out_specs += [
            pl.BlockSpec((d, tf), lambda i, j: (0, fcol(j))),
            pl.BlockSpec((d, tf), lambda i, j: (0, fcol(j))),
            pl.BlockSpec((f, tn), lambda i, j: (0, col(j))),
        ]
        out_shape += [jax.ShapeDtypeStruct((d, f), BF), jax.ShapeDtypeStruct((d, f), BF),
                      jax.ShapeDtypeStruct((f, d), BF)]
    outs = pl.pallas_call(
        functools.partial(_ffn_kernel, nf=nf, tf=tf, emit_bf16=emit_bf16),
        grid=(m // tm, nf + d // tn),
        in_specs=[
            pl.BlockSpec((tm, d), lambda i, j: (i, 0), pipeline_mode=pl.Buffered(1)),
            pl.BlockSpec((1, d), lambda i, j: (0, 0)),
            sh_spec, sc_spec,
            *w_specs,
            pl.BlockSpec((tm, tn), lambda i, j: (i, col(j))),
            gate_spec,
        ],
        out_specs=out_specs,
        out_shape=out_shape,
        scratch_shapes=[pltpu.VMEM((tm, d), BF), pltpu.VMEM((nf, tm, tf), BF)],
        compiler_params=_params("parallel", "arbitrary"),
        name="ffn_half",
    )(x, g, shift.arr, scale.arr, *w_args, x, gate.arr)
    return (outs[0], tuple(outs[1:])) if emit_bf16 else outs[0]


def _mod_mm_kernel(x_ref, g_ref, sh_ref, sc_ref, w_ref, *o_refs, out_transposed, ranges):
    h = _modulate(x_ref[...], g_ref[...], sh_ref[...], sc_ref[...]).astype(BF)
    y = _dot_nt(w_ref[...], h) if out_transposed else _dot(h, w_ref[...])
    for o_ref, (lo, hi) in zip(o_refs, ranges):
        o_ref[...] = (y[lo:hi] if out_transposed else y[:, lo:hi]).astype(o_ref.dtype)


def _mod_mm(x, g, shift, scale, w, outs, *, seq_rows, out_transposed=False):
    m, d = x.shape
    tm = _row_tile(m, seq_rows, (512, 256, 128))
    sh_spec = _mod_spec(shift, tm, seq_rows, d)
    sc_spec = _mod_spec(scale, tm, seq_rows, d)
    if out_transposed:
        out_specs = [pl.BlockSpec((hi - lo, tm), lambda i: (0, i)) for _, lo, hi in outs]
        out_shape = [jax.ShapeDtypeStruct((hi - lo, m), dt) for dt, lo, hi in outs]
    else:
        out_specs = [pl.BlockSpec((tm, hi - lo), lambda i: (i, 0)) for _, lo, hi in outs]
        out_shape = [jax.ShapeDtypeStruct((m, hi - lo), dt) for dt, lo, hi in outs]
    return pl.pallas_call(
        functools.partial(_mod_mm_kernel, out_transposed=out_transposed, ranges=[(lo, hi) for _, lo, hi in outs]),
        grid=(m // tm,),
        in_specs=[
            pl.BlockSpec((tm, d), lambda i: (i, 0)),
            pl.BlockSpec((1, d), lambda i: (0, 0)),
            sh_spec, sc_spec,
            _const_spec(w),
        ],
        out_specs=out_specs,
        out_shape=out_shape,
        compiler_params=_params("parallel"),
        name="mod_matmul",
    )(x, g, shift.arr, scale.arr, w)


def _mm_resid_kernel(a_ref, w_ref, x_ref, gt_ref, o_ref, *, a_transposed):
    if a_transposed:
        y = lax.dot_general(a_ref[...], w_ref[...], (((0,), (0,)), ((), ())), preferred_element_type=F32)
    else:
        y = _dot(a_ref[...], w_ref[...])
    o_ref[...] = x_ref[...] + gt_ref[...] * y


def _mm_resid(a, w, x, gate, *, seq_rows, a_transposed=False):
    m, n = x.shape
    k = w.shape[0]
    tm = _row_tile(m, seq_rows, (512, 256, 128))
    return pl.pallas_call(
        functools.partial(_mm_resid_kernel, a_transposed=a_transposed),
        grid=(m // tm,),
        in_specs=[
            pl.BlockSpec((k, tm), lambda i: (0, i)) if a_transposed else pl.BlockSpec((tm, k), lambda i: (i, 0)),
            _const_spec(w),
            pl.BlockSpec((tm, n), lambda i: (i, 0)),
            _mod_spec(gate, tm, seq_rows, n),
        ],
        out_specs=pl.BlockSpec((tm, n), lambda i: (i, 0)),
        out_shape=jax.ShapeDtypeStruct((m, n), F32),
        compiler_params=_params("parallel"),
        name="matmul_residual",
    )(a, w, x, gate.arr)


def _mla_pre_kernel(x_ref, g_ref, sh_ref, sc_ref, cos_ref, sin_ref, cost_ref, sint_ref, w1_ref, qn_ref, kn_ref,
                    w2t_ref, w3_ref, row_ref, kcat_ref, latt_ref, qt_ref, *, ql, kl, nh):
    tm = x_ref.shape[0]
    cos_t = cos_ref[...]
    sin_t = sin_ref[...]
    h = _modulate(x_ref[...], g_ref[...], sh_ref[...], sc_ref[...]).astype(BF)
    p1 = _dot(h, w1_ref[...])
    cq = _rms(p1[:, :ql], qn_ref[...])
    lat = _rms(p1[:, ql:ql + kl], kn_ref[...])
    a = ql + kl
    kr = p1[:, a:a + LANES] * cos_t + p1[:, a + LANES:a + 2 * LANES] * sin_t
    row_ref[:, :kl] = lat
    row_ref[:, kl:] = kr[:, :MLA_ROPE]
    kcat_ref[:, :kl] = lat.astype(BF)
    kcat_ref[:, kl:] = kr.astype(BF)
    latt_ref[...] = lat.T.astype(BF)
    q2t = _dot(w2t_ref[...], cq.T.astype(BF))
    cos_c = cost_ref[...]
    sin_c = sint_ref[...]
    hn = nh * LANES
    for hd in range(nh):
        lo = hd * LANES
        qn_t = q2t[lo:lo + LANES].astype(BF)
        qt_ref[:kl, hd * tm:(hd + 1) * tm] = _dot(w3_ref[hd], qn_t).astype(BF)
        qr_t = q2t[hn + lo:hn + lo + LANES] * cos_c + q2t[2 * hn + lo:2 * hn + lo + LANES] * sin_c
        qt_ref[kl:, hd * tm:(hd + 1) * tm] = qr_t.astype(BF)


def _mla_pre(x, g, shift, scale, cos_t, sin_t, w1, qnorm, knorm, w2t, w3, *, seq_rows):
    m, d = x.shape
    ql = qnorm.shape[1]
    kl = knorm.shape[1]
    nh = w3.shape[0]
    c = kl + LANES
    tm = _row_tile(m, seq_rows, (256, 128))
    sh_spec = _mod_spec(shift, tm, seq_rows, d)
    sc_spec = _mod_spec(scale, tm, seq_rows, d)
    ps = _pos_spec(cos_t, tm)
    cos_c, sin_c = cos_t.T, sin_t.T
    npos = cos_t.shape[0]
    if npos == 1:
        pst = pl.BlockSpec((LANES, 1), lambda i: (0, 0))
    else:
        pst = pl.BlockSpec((LANES, tm), lambda i: (0, i % (npos // tm)))
    return pl.pallas_call(
        functools.partial(_mla_pre_kernel, ql=ql, kl=kl, nh=nh),
        grid=(m // tm,),
        in_specs=[
            pl.BlockSpec((tm, d), lambda i: (i, 0)),
            pl.BlockSpec((1, d), lambda i: (0, 0)),
            sh_spec, sc_spec, ps, ps, pst, pst,
            _const_spec(w1), _const_spec(qnorm), _const_spec(knorm), _const_spec(w2t), _const_spec(w3),
        ],
        out_specs=[
            pl.BlockSpec((tm, kl + MLA_ROPE), lambda i: (i, 0)),
            pl.BlockSpec((tm, c), lambda i: (i, 0)),
            pl.BlockSpec((kl, tm), lambda i: (0, i)),
            pl.BlockSpec((None, c, nh * tm), lambda i: (i, 0, 0)),
        ],
        out_shape=[
            jax.ShapeDtypeStruct((m, kl + MLA_ROPE), F32),
            jax.ShapeDtypeStruct((m, c), BF),
            jax.ShapeDtypeStruct((kl, m), BF),
            jax.ShapeDtypeStruct((m // tm, c, nh * tm), BF),
        ],
        compiler_params=_params("parallel"),
        name="mla_project",
    )(x, g, shift.arr, scale.arr, cos_t, sin_t, cos_c, sin_c, w1, qnorm, knorm, w2t, w3)


def _mla_flash_kernel(it_ref, jt_ref, qt_ref, k_ref, vt_ref, wuvt_ref, o_ref, m_sc, l_sc, acc_sc, *,
                      tq, tk, nh, exp2_scale):
    t = pl.program_id(1)
    i = it_ref[t]
    j = jt_ref[t]
    cols = nh * tq
    j_last = ((i + 1) * tq - 1) // tk

    @pl.when(j == 0)
    def _():
        m_sc[...] = jnp.full(m_sc.shape, NEG_INF, F32)
        l_sc[...] = jnp.zeros(l_sc.shape, F32)
        acc_sc[...] = jnp.zeros(acc_sc.shape, F32)

    def step(masked, nkeys):
        s = _dot(k_ref[:nkeys], qt_ref[...])
        if masked:
            kpos = j * tk + lax.broadcasted_iota(jnp.int32, (nkeys, cols), 0)
            qpos = i * tq + (lax.broadcasted_iota(jnp.int32, (nkeys, cols), 1) & (tq - 1))
            s = jnp.where(kpos <= qpos, s, NEG_INF)
        m_old = m_sc[...]
        m_new = jnp.maximum(m_old, jnp.max(s, axis=0, keepdims=True))
        alpha = jnp.exp2((m_old - m_new) * exp2_scale)
        p = jnp.exp2((s - m_new) * exp2_scale)
        l_sc[...] = alpha * l_sc[...] + jnp.sum(p, axis=0, keepdims=True)
        acc_sc[...] = alpha * acc_sc[...] + _dot(vt_ref[:, :nkeys], p.astype(BF))
        m_sc[...] = m_new

    needs_mask = (j + 1) * tk - 1 > i * tq
    half_block = (i + 1) * tq <= j * tk + tk // 2

    @pl.when(needs_mask & half_block)
    def _():
        step(True, tk // 2)

    @pl.when(needs_mask & jnp.logical_not(half_block))
    def _():
        step(True, tk)

    @pl.when(jnp.logical_not(needs_mask))
    def _():
        step(False, tk)

    @pl.when(j == j_last)
    def _():
        o_t = (acc_sc[...] * (1.0 / l_sc[...])).astype(BF)
        for hd in range(nh):
            o_ref[hd * MLA_V:(hd + 1) * MLA_V, :] = _dot(wuvt_ref[hd], o_t[:, hd * tq:(hd + 1) * tq]).astype(BF)


def _mla_flash(qt, kcat, latt, wuvt, *, batch, seq, scale):
    nblk, c, cols = qt.shape
    nh, _, kl = wuvt.shape
    tq = cols // nh
    m = nblk * tq
    tk = _pick(seq, (512, 256, 128))
    nq = seq // tq
    nk = seq // tk
    sched = [(i, j) for i in range(nq) for j in range(((i + 1) * tq - 1) // tk + 1)]
    it = jnp.asarray(np.array([s[0] for s in sched], np.int32))
    jt = jnp.asarray(np.array([s[1] for s in sched], np.int32))
    grid_spec = pltpu.PrefetchScalarGridSpec(
        num_scalar_prefetch=2,
        grid=(batch, len(sched)),
        in_specs=[
            pl.BlockSpec((None, c, cols), lambda b, t, it_ref, jt_ref: (b * nq + it_ref[t], 0, 0)),
            pl.BlockSpec((tk, c), lambda b, t, it_ref, jt_ref: (b * nk + jt_ref[t], 0)),
            pl.BlockSpec((kl, tk), lambda b, t, it_ref, jt_ref: (0, b * nk + jt_ref[t])),
            pl.BlockSpec(wuvt.shape, lambda b, t, it_ref, jt_ref: (0, 0, 0), pipeline_mode=pl.Buffered(1)),
        ],
        out_specs=pl.BlockSpec((nh * MLA_V, tq), lambda b, t, it_ref, jt_ref: (0, b * nq + it_ref[t])),
        scratch_shapes=[
            pltpu.VMEM((1, cols), F32),
            pltpu.VMEM((1, cols), F32),
            pltpu.VMEM((kl, cols), F32),
        ],
    )
    return pl.pallas_call(
        functools.partial(_mla_flash_kernel, tq=tq, tk=tk, nh=nh, exp2_scale=scale * math.log2(math.e)),
        grid_spec=grid_spec,
        out_shape=jax.ShapeDtypeStruct((nh * MLA_V, m), BF),
        compiler_params=_params("parallel", "arbitrary"),
        name="mla_prompt_attention",
    )(it, jt, qt, kcat, latt, wuvt)


def _mla_decode_kernel(pt_ref, q_ref, new_ref, cache_ref, o_ref, pages, sems, kbuf, m_sc, l_sc, acc_sc, *,
                       npg, kl, scale, layer, chunks_per_seq, n_chunks):
    c = pl.program_id(1)
    t = pl.program_id(0) * chunks_per_seq + c

    def chunk_copy(chunk, slot, k):
        page = pt_ref[chunk * npg + k]
        return pltpu.make_async_copy(cache_ref.at[page, layer], pages.at[slot, k], sems.at[slot])

    def start_chunk(chunk):
        slot = chunk % DECODE_SLOTS
        for k in range(npg):
            chunk_copy(chunk, slot, k).start()

    @pl.when(t == 0)
    def _():
        for first in range(min(DECODE_SLOTS - 1, n_chunks)):
            start_chunk(first)

    @pl.when(t + (DECODE_SLOTS - 1) < n_chunks)
    def _():
        start_chunk(t + (DECODE_SLOTS - 1))

    slot = t % DECODE_SLOTS
    for k in range(npg):
        chunk_copy(t, slot, k).wait()

    @pl.when(c == 0)
    def _():
        m_sc[...] = jnp.full(m_sc.shape, NEG_INF, F32)
        l_sc[...] = jnp.zeros(l_sc.shape, F32)
        acc_sc[...] = jnp.zeros(acc_sc.shape, F32)

    for k in range(npg):
        kbuf[:, k * PAGE_SIZE:(k + 1) * PAGE_SIZE] = pages[slot, k].astype(BF)

    q = q_ref[...]
    keys_t = kbuf[...]
    s = (_dot(q[:, :kl], keys_t[:kl]) + _dot(q[:, kl:kl + MLA_ROPE], keys_t[kl:])) * scale
    m_old = m_sc[...]
    m_new = jnp.maximum(m_old, jnp.max(s, axis=-1, keepdims=True))
    alpha = jnp.exp(m_old - m_new)
    p = jnp.exp(s - m_new)
    l_sc[...] = alpha * l_sc[...] + jnp.sum(p, axis=-1, keepdims=True)
    acc_sc[...] = alpha * acc_sc[...] + _dot_nt(p.astype(BF), keys_t[:kl])
    m_sc[...] = m_new

    @pl.when(c == pl.num_programs(1) - 1)
    def _():
        new = new_ref[...].astype(F32)
        s_new = jnp.sum(q.astype(F32) * new, axis=-1, keepdims=True) * scale
        m_old = m_sc[...]
        m_new = jnp.maximum(m_old, s_new)
        alpha = jnp.exp(m_old - m_new)
        p_new = jnp.exp(s_new - m_new)
        l_new = alpha * l_sc[...] + p_new
        acc = alpha * acc_sc[...] + p_new.astype(BF).astype(F32) * new[:, :kl]
        o_ref[...] = (acc * (1.0 / l_new)).astype(BF)


def _mla_decode(page_table, qcat, kcat, cache_t, *, layer, scale):
    nb, nh, c = qcat.shape
    lat = cache_t.shape[2]
    kl = lat - MLA_ROPE
    n_pages = page_table.shape[1]
    npg = _pick(n_pages, (32, 16, 8, 4, 2, 1))
    chunks_per_seq = n_pages // npg
    pt = page_table.reshape(-1)
    grid_spec = pltpu.PrefetchScalarGridSpec(
        num_scalar_prefetch=1,
        grid=(nb, chunks_per_seq),
        in_specs=[
            pl.BlockSpec((None, nh, c), lambda b, ch, pt_ref: (b, 0, 0)),
            pl.BlockSpec((None, 1, c), lambda b, ch, pt_ref: (b, 0, 0)),
            pl.BlockSpec(memory_space=pl.ANY),
        ],
        out_specs=pl.BlockSpec((None, nh, kl), lambda b, ch, pt_ref: (b, 0, 0)),
        scratch_shapes=[
            pltpu.VMEM((DECODE_SLOTS, npg, lat, PAGE_SIZE), F32),
            pltpu.SemaphoreType.DMA((DECODE_SLOTS,)),
            pltpu.VMEM((lat, npg * PAGE_SIZE), BF),
            pltpu.VMEM((nh, 1), F32),
            pltpu.VMEM((nh, 1), F32),
            pltpu.VMEM((nh, kl), F32),
        ],
    )
    return pl.pallas_call(
        functools.partial(_mla_decode_kernel, npg=npg, kl=kl, scale=scale, layer=layer,
                          chunks_per_seq=chunks_per_seq, n_chunks=nb * chunks_per_seq),
        grid_spec=grid_spec,
        out_shape=jax.ShapeDtypeStruct((nb, nh, kl), BF),
        compiler_params=_params("arbitrary", "arbitrary"),
        name="mla_sample_attention",
    )(pt, qcat, kcat, cache_t)


def _uv_resid_kernel(o_ref, wuv_ref, wo_ref, x_ref, gt_ref, out_ref, oh_ref):
    nh = o_ref.shape[0]
    for hd in range(nh):
        oh_ref[:, hd * MLA_V:(hd + 1) * MLA_V] = _dot(o_ref[hd], wuv_ref[hd]).astype(BF)
    out_ref[...] = x_ref[...] + gt_ref[...] * _dot(oh_ref[...], wo_ref[...])


def _uv_resid(o_lat, wuv, w_o, x, gate):
    nh, m, kl = o_lat.shape
    d = x.shape[1]
    return pl.pallas_call(
        _uv_resid_kernel,
        grid=(1,),
        in_specs=[
            pl.BlockSpec(o_lat.shape, lambda i: (0, 0, 0)),
            pl.BlockSpec(wuv.shape, lambda i: (0, 0, 0)),
            pl.BlockSpec(w_o.shape, lambda i: (0, 0)),
            pl.BlockSpec((m, d), lambda i: (0, 0)),
            _mod_spec(gate, m, 1, d),
        ],
        out_specs=pl.BlockSpec((m, d), lambda i: (0, 0)),
        out_shape=jax.ShapeDtypeStruct((m, d), F32),
        scratch_shapes=[pltpu.VMEM((m, nh * MLA_V), BF)],
        compiler_params=_params("arbitrary"),
        name="mla_value_up_residual",
    )(o_lat, wuv, w_o, x, gate.arr)


def _swa_prompt_kernel(qt_ref, kvp_ref, kvc_ref, bias_ref, mask_ref, sink_ref, o_ref, *, n_kv, scale):
    kvw = n_kv * LANES
    pairs_per_kv = SWA_GROUP // 2
    mask = mask_ref[...] > 0.5
    nkeys = 2 * BLOCK
    for c in range(n_kv):
        def sect(s):
            lo = s * kvw + c * LANES
            return jnp.concatenate([kvp_ref[:, lo:lo + LANES], kvc_ref[:, lo:lo + LANES]], axis=0)
        k2 = jnp.concatenate([sect(0), sect(1)], axis=0)
        v2 = jnp.concatenate([sect(2), sect(3)], axis=0)
        pr0 = c * pairs_per_kv
        qc = jnp.concatenate([qt_ref[(pr0 + pp) * LANES:(pr0 + pp + 1) * LANES, :] for pp in range(pairs_per_kv)],
                             axis=1)
        s = _dot(k2, qc) * scale + bias_ref[c]
        s = jnp.where(mask, s, NEG_INF)
        halves = []
        for hf in range(2):
            sh = s[hf * nkeys:(hf + 1) * nkeys]
            sink = sink_ref[c, hf]
            m = jnp.maximum(jnp.max(sh, axis=0, keepdims=True), sink)
            e = jnp.exp(sh - m)
            den = jnp.sum(e, axis=0, keepdims=True) + jnp.exp(sink - m)
            halves.append((e * (1.0 / den)).astype(BF))
        p = jnp.concatenate(halves, axis=0)
        o_t = lax.dot_general(v2, p, (((0,), (0,)), ((), ())), preferred_element_type=F32).astype(BF)
        for pp in range(pairs_per_kv):
            o_ref[(pr0 + pp) * LANES:(pr0 + pp + 1) * LANES, :] = o_t[:, pp * BLOCK:(pp + 1) * BLOCK]


def _swa_prompt(qt, kv4, bias_t, mask_t, sink_cols, *, batch, seq, n_kv, scale):
    dq, m = qt.shape
    nb = seq // BLOCK
    w4 = kv4.shape[1]
    return pl.pallas_call(
        functools.partial(_swa_prompt_kernel, n_kv=n_kv, scale=scale),
        grid=(batch, nb),
        in_specs=[
            pl.BlockSpec((dq, BLOCK), lambda b, j: (0, b * nb + j)),
            pl.BlockSpec((BLOCK, w4), lambda b, j: (b * nb + jnp.maximum(j - 1, 0), 0)),
            pl.BlockSpec((BLOCK, w4), lambda b, j: (b * nb + j, 0)),
            _const_spec(bias_t),
            pl.BlockSpec((None,) + mask_t.shape[1:], lambda b, j: (jnp.minimum(j, 1), 0, 0)),
            _const_spec(sink_cols),
        ],
        out_specs=pl.BlockSpec((dq, BLOCK), lambda b, j: (0, b * nb + j)),
        out_shape=jax.ShapeDtypeStruct((dq, m), BF),
        compiler_params=_params("parallel", "arbitrary"),
        name="swa_prompt_attention",
    )(qt, kv4, kv4, bias_t, mask_t, sink_cols)


def _swa_sample_kernel(q_ref, kt_ref, vt_ref, kn_ref, vn_ref, bias_ref, bias_new_ref, sink_ref, o_ref, *, n_kv, scale):
    for c in range(n_kv):
        q = q_ref[:, c * SWA_GROUP:(c + 1) * SWA_GROUP, :]
        kt = kt_ref[:, c].astype(BF)
        vt = vt_ref[:, c].astype(BF)
        kn = kn_ref[:, c].astype(BF).astype(F32)[:, None, :]
        vn = vn_ref[:, c].astype(BF).astype(F32)[:, None, :]
        s = jnp.einsum("bgd,bdk->bgk", q, kt, preferred_element_type=F32) * scale + bias_ref[c][None]
        s_new = jnp.sum(q.astype(F32) * kn, axis=-1, keepdims=True) * scale + bias_new_ref[c][None]
        sink = sink_ref[c][None]
        m = jnp.maximum(jnp.maximum(jnp.max(s, axis=-1, keepdims=True), s_new), sink)
        e = jnp.exp(s - m)
        e_new = jnp.exp(s_new - m)
        inv = 1.0 / (jnp.sum(e, axis=-1, keepdims=True) + e_new + jnp.exp(sink - m))
        p = (e * inv).astype(BF)
        p_new = (e_new * inv).astype(BF).astype(F32)
        o = jnp.einsum("bgk,bdk->bgd", p, vt, preferred_element_type=F32) + p_new * vn
        o_ref[:, c * SWA_GROUP:(c + 1) * SWA_GROUP, :] = o.astype(BF)


def _swa_sample(q, kt, vt, k_new, v_new, bias, bias_new, sinks, *, scale):
    nb, nh, dh = q.shape
    n_kv, _, w = kt.shape[1:]
    bb = _pick(nb, (16, 8))
    return pl.pallas_call(
        functools.partial(_swa_sample_kernel, n_kv=n_kv, scale=scale),
        grid=(nb // bb,),
        in_specs=[
            pl.BlockSpec((bb, nh, dh), lambda i: (i, 0, 0)),
            pl.BlockSpec((bb, n_kv, dh, w), lambda i: (i, 0, 0, 0)),
            pl.BlockSpec((bb, n_kv, dh, w), lambda i: (i, 0, 0, 0)),
            pl.BlockSpec((bb, n_kv, dh), lambda i: (i, 0, 0)),
            pl.BlockSpec((bb, n_kv, dh), lambda i: (i, 0, 0)),
            pl.BlockSpec(bias.shape, lambda i: (0, 0, 0)),
            pl.BlockSpec(bias_new.shape, lambda i: (0, 0, 0)),
            pl.BlockSpec(sinks.shape, lambda i: (0, 0, 0)),
        ],
        out_specs=pl.BlockSpec((bb, nh, dh), lambda i: (i, 0, 0)),
        out_shape=jax.ShapeDtypeStruct((nb, nh, dh), BF),
        compiler_params=_params("parallel"),
        name="swa_sample_attention",
    )(q, kt, vt, k_new, v_new, bias, bias_new, sinks)


def _final_norm_kernel(x_ref, g_ref, o_ref):
    o_ref[...] = _rms(x_ref[...], g_ref[...])


def _final_norm(x, g):
    m, d = x.shape
    tm = _pick(m, (512, 256, 128))
    return pl.pallas_call(
        _final_norm_kernel,
        grid=(m // tm,),
        in_specs=[pl.BlockSpec((tm, d), lambda i: (i, 0)), pl.BlockSpec((1, d), lambda i: (0, 0))],
        out_specs=pl.BlockSpec((tm, d), lambda i: (i, 0)),
        out_shape=jax.ShapeDtypeStruct((m, d), F32),
        compiler_params=_params("parallel"),
        name="final_norm",
    )(x, g)


def _rope_tables(pos):
    half = MLA_ROPE // 2
    inv = ROPE_THETA ** (-jnp.arange(half, dtype=F32) / half)
    ang = pos.astype(F32)[:, None] * inv
    cos, sin = jnp.cos(ang), jnp.sin(ang)
    pad = jnp.zeros((pos.shape[0], LANES - MLA_ROPE), F32)
    return jnp.concatenate([cos, cos, pad], axis=1), jnp.concatenate([-sin, sin, pad], axis=1)


def _swap_halves(w):
    half = w.shape[-1] // 2
    return jnp.concatenate([w[..., half:], w[..., :half]], axis=-1)


def _pad_lanes(w):
    return jnp.pad(w, [(0, 0)] * (w.ndim - 1) + [(0, LANES - w.shape[-1])])


def _t5_bucket(dist):
    max_exact = NUM_BUCKETS // 2
    n = jnp.maximum(dist, 0)
    nf = jnp.maximum(n, 1).astype(F32)
    large = max_exact + (jnp.log(nf / max_exact) / math.log(MAX_DISTANCE / max_exact)
                         * (NUM_BUCKETS - max_exact)).astype(jnp.int32)
    large = jnp.minimum(large, NUM_BUCKETS - 1)
    return jnp.where(n < max_exact, n, large)


def kernel(x_prompt, x_sample, c_prompt, c_sample, cache_mla, state_swa_k, state_swa_v, page_table, ada_w, ada_b, norm_g, ffn_w_in, ffn_w_out, mla_w_dq, mla_q_norm, mla_w_uq, mla_w_dkv, mla_kv_norm, mla_w_uk, mla_w_uv, mla_w_o, kv_ada_w, kv_ada_b, kv_norm_g, kv_w, swa_w_q, swa_w_o, swa_sinks, rel_bias_table, final_norm_g):
    batch, seq, d = x_prompt.shape
    dec_batch, dec_seq, _ = x_sample.shape
    assert dec_seq == 1
    depth = ada_w.shape[0]
    n_a = mla_w_dq.shape[0]
    n_pages = page_table.shape[1]
    past_len = n_pages * PAGE_SIZE
    ql = mla_w_dq.shape[2]
    kl = mla_kv_norm.shape[1]
    nh = mla_w_uk.shape[2]
    swa_heads = swa_w_q.shape[2] // SWA_HEAD_DIM
    n_kv = swa_heads // SWA_GROUP
    win_s = state_swa_k.shape[1]
    assert win_s == WINDOW and seq % BLOCK == 0
    mla_scale = float((MLA_NOPE + MLA_ROPE) ** -0.5)
    swa_scale = float(SWA_HEAD_DIM ** -0.5)
    mp = batch * seq

    c_all = jnp.concatenate([c_sample, c_prompt], axis=0)
    n_seq = c_all.shape[0]
    c_all = jnp.pad(c_all, ((0, (-n_seq) % 16), (0, 0)))
    mod_s3 = _ada(c_all, ada_w, ada_b[:, None, :])
    kvmod_s3 = _ada(c_all, kv_ada_w[None], kv_ada_b[None, None, :])
    mod_p4 = mod_s3[:, dec_batch:dec_batch + batch, None, :]
    kvmod_p4 = kvmod_s3[:, dec_batch:dec_batch + batch, None, :]

    cos_p, sin_p = _rope_tables(jnp.arange(seq, dtype=jnp.int32))
    cos_s, sin_s = _rope_tables(jnp.full((1,), past_len, jnp.int32))

    qi = jnp.arange(BLOCK, dtype=jnp.int32)
    km = jnp.arange(2 * BLOCK, dtype=jnp.int32) - BLOCK
    d_p = qi[:, None] - km[None, :]
    win_mask = (d_p >= 0) & (d_p <= WINDOW)
    onehot_p = (_t5_bucket(d_p)[:, :, None] == jnp.arange(NUM_BUCKETS, dtype=jnp.int32)).astype(F32)
    bias_p = jnp.einsum("qkb,bh->hqk", onehot_p, rel_bias_table.astype(F32),
                        precision=lax.Precision.HIGHEST)
    bias_pairs = bias_p.reshape(swa_heads // 2, 2, BLOCK, 2 * BLOCK).transpose(0, 2, 1, 3)
    pairs_per_kv = SWA_GROUP // 2
    bias_t = jnp.transpose(bias_pairs.reshape(n_kv, pairs_per_kv * BLOCK, 4 * BLOCK), (0, 2, 1))
    mask_first = win_mask & (km >= 0)[None, :]
    mask_t = jnp.stack([jnp.tile(mask_first, (pairs_per_kv, 2)).T,
                        jnp.tile(win_mask, (pairs_per_kv, 2)).T]).astype(F32)
    d_s = jnp.concatenate([jnp.arange(win_s, 0, -1, dtype=jnp.int32), jnp.zeros((1,), jnp.int32)])
    bias_s = jnp.transpose(rel_bias_table[_t5_bucket(d_s)], (1, 0)).astype(F32)
    bias_s_win = bias_s[:, :win_s].reshape(n_kv, SWA_GROUP, win_s)
    bias_s_new = bias_s[:, win_s:].reshape(n_kv, SWA_GROUP, 1)
    state_kt = jnp.transpose(state_swa_k, (0, 2, 3, 1))
    state_vt = jnp.transpose(state_swa_v, (0, 2, 3, 1))

    cache_t = jnp.transpose(cache_mla, (0, 1, 3, 2))

    x_p = x_prompt.reshape(mp, d)
    x_s = x_sample.reshape(dec_batch, d)
    rows_p, rows_s = [], []
    kv4_p = k_p = v_p = k_new = v_new = None

    for l in range(depth):
        if l == n_a:
            kv_k = kv_w[:, :n_kv * SWA_HEAD_DIM].reshape(d, n_kv, SWA_HEAD_DIM)
            kv_v = kv_w[:, n_kv * SWA_HEAD_DIM:].reshape(d, n_kv, SWA_HEAD_DIM)
            zero = jnp.zeros_like(kv_k)
            lo = lambda w: jnp.concatenate([w, zero], axis=-1).reshape(d, n_kv * LANES)
            hi = lambda w: jnp.concatenate([zero, w], axis=-1).reshape(d, n_kv * LANES)
            nkv = 2 * n_kv * SWA_HEAD_DIM
            w4 = 4 * n_kv * LANES
            w_sh = jnp.concatenate([lo(kv_k), hi(kv_k), lo(kv_v), hi(kv_v), kv_w], axis=1).astype(BF)
            kvg = kv_norm_g[None]
            kv4_p, kv_f32_p = _mod_mm(x_p, kvg, _Mod(kvmod_p4, 0, 0), _Mod(kvmod_p4, 0, 1), w_sh,
                                      [(BF, 0, w4), (F32, w4, w4 + nkv)], seq_rows=seq)
            k_p = kv_f32_p[:, :nkv // 2].reshape(batch, seq, n_kv, SWA_HEAD_DIM)
            v_p = kv_f32_p[:, nkv // 2:].reshape(batch, seq, n_kv, SWA_HEAD_DIM)
            (kv_f32_s,) = _mod_mm(x_s, kvg, _Mod(kvmod_s3, 0, 0), _Mod(kvmod_s3, 0, 1), w_sh[:, w4:],
                                  [(F32, 0, nkv)], seq_rows=1)
            k_new = kv_f32_s[:, :nkv // 2].reshape(dec_batch, n_kv, SWA_HEAD_DIM)
            v_new = kv_f32_s[:, nkv // 2:].reshape(dec_batch, n_kv, SWA_HEAD_DIM)

        g = norm_g[l]
        mp_l = [_Mod(mod_p4, l, n) for n in range(3 * N_SUB)]
        ms_l = [_Mod(mod_s3, l, n) for n in range(3 * N_SUB)]

        x_s, w_bf = _ffn(x_s, g[0][None], ms_l[0], ms_l[1], ms_l[2], (ffn_w_in, ffn_w_out, l, 0), seq_rows=1)
        x_p = _ffn(x_p, g[0][None], mp_l[0], mp_l[1], mp_l[2], w_bf, seq_rows=seq)

        if l < n_a:
            a = l
            dkv = mla_w_dkv[a]
            w1 = jnp.concatenate([mla_w_dq[a], dkv[:, :kl], _pad_lanes(dkv[:, kl:]),
                                  _pad_lanes(_swap_halves(dkv[:, kl:]))], axis=1).astype(BF)
            wuq = mla_w_uq[a].reshape(ql, nh, MLA_NOPE + MLA_ROPE)
            wuq_r = wuq[:, :, MLA_NOPE:]
            w2t = jnp.concatenate([wuq[:, :, :MLA_NOPE].reshape(ql, nh * MLA_NOPE),
                                   _pad_lanes(wuq_r).reshape(ql, nh * LANES),
                                   _pad_lanes(_swap_halves(wuq_r)).reshape(ql, nh * LANES)], axis=1).T.astype(BF)
            w3 = jnp.transpose(mla_w_uk[a], (1, 0, 2)).astype(BF)
            wuv = jnp.transpose(mla_w_uv[a], (1, 0, 2)).astype(BF)
            wuvt = jnp.transpose(mla_w_uv[a], (1, 2, 0)).astype(BF)
            w_o = mla_w_o[a].astype(BF)
            qn, kn = mla_q_norm[a][None], mla_kv_norm[a][None]

            row_p, kcat_p, latt_p, qt_p = _mla_pre(x_p, g[1][None], mp_l[3], mp_l[4], cos_p, sin_p,
                                                   w1, qn, kn, w2t, w3, seq_rows=seq)
            oh_t_p = _mla_flash(qt_p, kcat_p, latt_p, wuvt, batch=batch, seq=seq, scale=mla_scale)
            x_p = _mm_resid(oh_t_p, w_o, x_p, mp_l[5], seq_rows=seq, a_transposed=True)

            row_s, kcat_s, _, qt_s = _mla_pre(x_s, g[1][None], ms_l[3], ms_l[4], cos_s, sin_s,
                                              w1, qn, kn, w2t, w3, seq_rows=1)
            q_s = jnp.transpose(qt_s.reshape(qt_s.shape[0], kl + LANES, nh, -1), (0, 3, 2, 1))
            q_s = q_s.reshape(dec_batch, nh, kl + LANES)
            o_lat_s = _mla_decode(page_table, q_s, kcat_s[:, None, :],
                                  cache_t, layer=a, scale=mla_scale)
            x_s = _uv_resid(jnp.transpose(o_lat_s, (1, 0, 2)), wuv, w_o, x_s, ms_l[5])
            rows_p.append(row_p.reshape(batch, seq, kl + MLA_ROPE))
            rows_s.append(row_s.reshape(dec_batch, 1, kl + MLA_ROPE))
        else:
            b = l - n_a
            w_q = swa_w_q[b].astype(BF)
            w_o = swa_w_o[b].astype(BF)
            nq_cols = swa_heads * SWA_HEAD_DIM
            (qt_p,) = _mod_mm(x_p, g[1][None], mp_l[3], mp_l[4], swa_w_q[b].T.astype(BF), [(BF, 0, nq_cols)],
                              seq_rows=seq, out_transposed=True)
            sink_cols = jnp.repeat(swa_sinks[b].reshape(n_kv, pairs_per_kv, 2).transpose(0, 2, 1), BLOCK,
                                   axis=2)[:, :, None, :]
            ot_p = _swa_prompt(qt_p, kv4_p, bias_t, mask_t, sink_cols, batch=batch, seq=seq, n_kv=n_kv,
                               scale=swa_scale)
            x_p = _mm_resid(ot_p, w_o, x_p, mp_l[5], seq_rows=seq, a_transposed=True)

            (q_s,) = _mod_mm(x_s, g[1][None], ms_l[3], ms_l[4], w_q, [(BF, 0, nq_cols)], seq_rows=1)
            o_s = _swa_sample(q_s.reshape(dec_batch, swa_heads, SWA_HEAD_DIM), state_kt, state_vt, k_new, v_new,
                              bias_s_win, bias_s_new, swa_sinks[b].reshape(n_kv, SWA_GROUP, 1), scale=swa_scale)
            x_s = _mm_resid(o_s.reshape(dec_batch, swa_heads * SWA_HEAD_DIM), w_o, x_s, ms_l[5], seq_rows=1)

        x_s, w_bf = _ffn(x_s, g[2][None], ms_l[6], ms_l[7], ms_l[8], (ffn_w_in, ffn_w_out, l, 1), seq_rows=1)
        x_p = _ffn(x_p, g[2][None], mp_l[6], mp_l[7], mp_l[8], w_bf, seq_rows=seq)

    y_prompt = _final_norm(x_p, final_norm_g[None]).reshape(batch, seq, d)
    y_sample = _final_norm(x_s, final_norm_g[None]).reshape(dec_batch, 1, d)
    new_mla_prompt = jnp.stack(rows_p, axis=1)
    new_mla_sample = jnp.stack(rows_s, axis=1)
    win_p = min(WINDOW, seq)
    shift_in = lambda st, new: jnp.transpose(jnp.concatenate([st[..., 1:], new[..., None]], axis=-1), (0, 3, 1, 2))
    return (y_prompt, y_sample, new_mla_prompt, new_mla_sample,
            k_p[:, seq - win_p:], v_p[:, seq - win_p:],
            shift_in(state_kt, k_new), shift_in(state_vt, v_new))
```
